```python
import math
import jax, jax.numpy as jnp
from jax import lax
import numpy as np

D_MODEL = 1024
BATCH = 8
SEQ = 2048
DEPTH = 1
DEC_BATCH = 128
DEC_SEQ = 1
PAST_LEN = 2048
PAGE_SIZE = 128

MIX_WIDTH = D_MODEL
SSM_WIDTH = MIX_WIDTH // 2
ATT_WIDTH = MIX_WIDTH - SSM_WIDTH
SSM_GROUP = 16
SSM_GROUPS = SSM_WIDTH // SSM_GROUP
SSM_STATE = 64
DT_MIN = 0.001
DT_MAX = 0.1
DIFF_HEADS = 4
V_DIM = ATT_WIDTH // DIFF_HEADS
QK_DIM = V_DIM // 2
ROPE_THETA = 10000.0
Q_BLOCK = 128
N_EXPERTS = 32
TOP_K = 4
D_EXPERT = D_MODEL
SWIGLU_ALPHA = 1.702
SWIGLU_LIMIT = 7.0
MOE_BLOCK = 128
LN_EPS = 1e-5
RMS_EPS = 1e-5
IN_COLS = SSM_WIDTH + 3 * ATT_WIDTH
DEEPNORM_ALPHA = (2 * DEPTH) ** 0.25
DEEPNORM_BETA = (8 * DEPTH) ** -0.25

kernel_name = "hymba_s5_diffattn_moe_decode_step"

F32 = jnp.float32


def lambda_init_fn(layer_idx):
    return 0.8 - 0.6 * math.exp(-0.3 * layer_idx)


def layer_norm(x, g, b):
    xf = x.astype(F32)
    mu = jnp.mean(xf, axis=-1, keepdims=True)
    var = jnp.mean(jnp.square(xf - mu), axis=-1, keepdims=True)
    return ((xf - mu) * lax.rsqrt(var + LN_EPS) * g.astype(F32) + b.astype(F32)).astype(x.dtype)


def rope(x, pos):
    half = QK_DIM // 2
    inv = ROPE_THETA ** (-jnp.arange(half, dtype=F32) / half)
    ang = pos.astype(F32)[:, None] * inv[None, :]
    cos = jnp.cos(ang)[None, :, None, None, :]
    sin = jnp.sin(ang)[None, :, None, None, :]
    xf = x.astype(F32)
    x1, x2 = xf[..., :half], xf[..., half:]
    return jnp.concatenate([x1 * cos - x2 * sin, x2 * cos + x1 * sin], axis=-1).astype(x.dtype)


def mixer_inputs(x, pos, w_in):
    b_, l_ = x.shape[:2]
    proj = x @ w_in
    u = proj[..., :SSM_WIDTH]
    q = proj[..., SSM_WIDTH:SSM_WIDTH + ATT_WIDTH].reshape(b_, l_, DIFF_HEADS, 2, QK_DIM)
    k = proj[..., SSM_WIDTH + ATT_WIDTH:SSM_WIDTH + 2 * ATT_WIDTH].reshape(b_, l_, DIFF_HEADS, 2, QK_DIM)
    v = proj[..., SSM_WIDTH + 2 * ATT_WIDTH:].reshape(b_, l_, DIFF_HEADS, V_DIM)
    return u, rope(q, pos), rope(k, pos), v


def ssm_discretize(a_re, a_im, log_dt, b_re, b_im):
    a_re = jnp.minimum(a_re.astype(F32), -1e-4)
    a_im = a_im.astype(F32)
    dt = jnp.exp(log_dt.astype(F32))[:, None]
    mag = jnp.exp(a_re * dt)
    lb_re = mag * jnp.cos(a_im * dt)
    lb_im = mag * jnp.sin(a_im * dt)
    den = jnp.square(a_re) + jnp.square(a_im)
    nr, ni = lb_re - 1.0, lb_im
    f_re = (nr * a_re + ni * a_im) / den
    f_im = (ni * a_re - nr * a_im) / den
    b_re = b_re.astype(F32)
    b_im = b_im.astype(F32)
    bb_re = f_re[..., None] * b_re - f_im[..., None] * b_im
    bb_im = f_re[..., None] * b_im + f_im[..., None] * b_re
    return lb_re, lb_im, bb_re, bb_im


def complex_affine_combine(e1, e2):
    a1r, a1i, b1r, b1i = e1
    a2r, a2i, b2r, b2i = e2
    return (a2r * a1r - a2i * a1i,
            a2r * a1i + a2i * a1r,
            a2r * b1r - a2i * b1i + b2r,
            a2r * b1i + a2i * b1r + b2i)


def ssm_mixer(u, s0_re, s0_im, p):
    b_, l_ = u.shape[:2]
    ug = u.reshape(b_, l_, SSM_GROUPS, SSM_GROUP).astype(F32)
    lb_re, lb_im, bb_re, bb_im = ssm_discretize(p['ssm_a_re'], p['ssm_a_im'], p['ssm_log_dt'], p['ssm_b_re'], p['ssm_b_im'])
    b_re = jnp.einsum('blgh,gph->blgp', ug, bb_re)
    b_im = jnp.einsum('blgh,gph->blgp', ug, bb_im)
    s0r = s0_re.astype(F32)
    s0i = s0_im.astype(F32)
    b_re = b_re.at[:, 0].add(lb_re * s0r - lb_im * s0i)
    b_im = b_im.at[:, 0].add(lb_re * s0i + lb_im * s0r)
    a_re = jnp.broadcast_to(lb_re, b_re.shape)
    a_im = jnp.broadcast_to(lb_im, b_re.shape)
    _, _, s_re, s_im = lax.associative_scan(complex_affine_combine, (a_re, a_im, b_re, b_im), axis=1)
    y = (jnp.einsum('blgp,ghp->blgh', s_re, p['ssm_c_re'].astype(F32))
         - jnp.einsum('blgp,ghp->blgh', s_im, p['ssm_c_im'].astype(F32))
         + p['ssm_d'].astype(F32)[None, None] * ug)
    y = jax.nn.gelu(y.reshape(b_, l_, SSM_WIDTH)).astype(u.dtype)
    z = y @ p['w_glu'] + p['b_glu']
    out = z[..., :SSM_WIDTH] * jax.nn.sigmoid(z[..., SSM_WIDTH:])
    return out, s_re[:, -1], s_im[:, -1]


def diff_lambda(p, lam_init):
    return (jnp.exp(jnp.sum(p['lambda_q1'].astype(F32) * p['lambda_k1'].astype(F32)))
            - jnp.exp(jnp.sum(p['lambda_q2'].astype(F32) * p['lambda_k2'].astype(F32)))
            + lam_init)


def diff_attend(q, k, v, q_pos, k_pos, lam):
    s = jnp.einsum('bqhmd,bkhmd->bhmqk', q, k, preferred_element_type=F32) * (QK_DIM ** -0.5)
    mask = k_pos[None, :] <= q_pos[:, None]
    s = jnp.where(mask[None, None, None], s, -jnp.inf)
    pr = jax.nn.softmax(s, axis=-1)
    a = pr[:, :, 0] - lam * pr[:, :, 1]
    return jnp.einsum('bhqk,bkhd->bqhd', a.astype(v.dtype), v)


def diff_attend_prompt(q, k, v, lam):
    b_, l_ = q.shape[:2]
    nb = l_ // Q_BLOCK
    qb = jnp.moveaxis(q.reshape(b_, nb, Q_BLOCK, DIFF_HEADS, 2, QK_DIM), 1, 0)
    pos = jnp.arange(l_, dtype=jnp.int32)
    qpos = pos.reshape(nb, Q_BLOCK)
    out = lax.map(lambda a: diff_attend(a[0], k, v, a[1], pos, lam), (qb, qpos))
    return jnp.moveaxis(out, 0, 1).reshape(b_, l_, DIFF_HEADS, V_DIM)


def diff_head_norm(o, subln_g, lam_init):
    b_, l_ = o.shape[:2]
    of = o.astype(F32)
    of = of * lax.rsqrt(jnp.mean(jnp.square(of), axis=-1, keepdims=True) + RMS_EPS) * subln_g.astype(F32)
    return (of * (1.0 - lam_init)).reshape(b_, l_, ATT_WIDTH).astype(o.dtype)


def moe_ffn(x2d, p):
    t_ = x2d.shape[0]
    logits = (x2d @ p['w_router'] + p['b_router']).astype(F32)
    top_val, top_idx = lax.top_k(logits, TOP_K)
    gates = jax.nn.softmax(top_val, axis=-1)
    n_assign = t_ * TOP_K
    e_flat = top_idx.reshape(n_assign).astype(jnp.int32)
    tok_flat = jnp.arange(n_assign, dtype=jnp.int32) // TOP_K
    g_flat = gates.reshape(n_assign)
    order = jnp.argsort(e_flat, stable=True)
    e_sorted = e_flat[order]
    counts = jnp.bincount(e_flat, length=N_EXPERTS).astype(jnp.int32)
    start = jnp.cumsum(counts) - counts
    padded = ((counts + MOE_BLOCK - 1) // MOE_BLOCK) * MOE_BLOCK
    pend = jnp.cumsum(padded)
    pstart = pend - padded
    dest = pstart[e_sorted] + (jnp.arange(n_assign, dtype=jnp.int32) - start[e_sorted])
    n_blocks = -(-(n_assign + N_EXPERTS * (MOE_BLOCK - 1)) // MOE_BLOCK)
    n_rows = n_blocks * MOE_BLOCK
    tok_buf = jnp.zeros((n_rows,), jnp.int32).at[dest].set(tok_flat[order])
    gate_buf = jnp.zeros((n_rows,), F32).at[dest].set(g_flat[order])
    block_start = jnp.arange(n_blocks, dtype=jnp.int32) * MOE_BLOCK
    block_expert = jnp.minimum(jnp.searchsorted(pend, block_start, side='right'), N_EXPERTS - 1).astype(jnp.int32)
    x_buf = x2d[tok_buf].reshape(n_blocks, MOE_BLOCK, x2d.shape[1])
    w_gu, b_gu, w_dn, b_dn = p['w_gate_up'], p['b_gate_up'], p['w_down'], p['b_down']

    def expert_block(args):
        xb, e = args
        h = (xb @ w_gu[e] + b_gu[e]).astype(F32)
        glu = jnp.minimum(h[:, :D_EXPERT], SWIGLU_LIMIT)
        lin = jnp.clip(h[:, D_EXPERT:], -SWIGLU_LIMIT, SWIGLU_LIMIT)
        act = (glu * jax.nn.sigmoid(SWIGLU_ALPHA * glu) * (lin + 1.0)).astype(xb.dtype)
        return act @ w_dn[e] + b_dn[e]

    y_buf = lax.map(expert_block, (x_buf, block_expert)).reshape(n_rows, x2d.shape[1])
    return jax.ops.segment_sum(y_buf * gate_buf.astype(y_buf.dtype)[:, None], tok_buf, num_segments=t_)


def layer_output(x, ssm_out, att_o, lam_init, p):
    att_out = diff_head_norm(att_o, p['subln_g'], lam_init)
    mix = jnp.concatenate([ssm_out, att_out], axis=-1) @ p['w_out']
    h = layer_norm(DEEPNORM_ALPHA * x + mix, p['ln1_g'], p['ln1_b'])
    b_, l_, d_ = h.shape
    f = moe_ffn(h.reshape(b_ * l_, d_), p).reshape(b_, l_, d_)
    return layer_norm(DEEPNORM_ALPHA * h + f, p['ln2_g'], p['ln2_b'])


def setup_inputs(seed: int = 0) -> dict:
    key = jax.random.key(seed)
    ks = jax.random.split(key, 40)
    n_pages = PAST_LEN // PAGE_SIZE
    n_phys = (DEC_BATCH * n_pages * 5) // 4

    def nrm(k, shape, scale):
        return jax.random.normal(k, shape, F32) * scale

    x_prompt = nrm(ks[0], (BATCH, SEQ, D_MODEL), 1.0)
    x_sample = nrm(ks[1], (DEC_BATCH, DEC_SEQ, D_MODEL), 1.0)
    cache_k = nrm(ks[2], (DEPTH, n_phys, PAGE_SIZE, DIFF_HEADS, 2 * QK_DIM), 1.0)
    cache_v = nrm(ks[3], (DEPTH, n_phys, PAGE_SIZE, DIFF_HEADS, V_DIM), 1.0)
    state_ssm_re = nrm(ks[4], (DEPTH, DEC_BATCH, SSM_GROUPS, SSM_STATE), 0.5)
    state_ssm_im = nrm(ks[5], (DEPTH, DEC_BATCH, SSM_GROUPS, SSM_STATE), 0.5)
    page_table = jax.random.permutation(ks[6], n_phys)[:DEC_BATCH * n_pages].reshape(DEC_BATCH, n_pages).astype(jnp.int32)

    w_in = jnp.concatenate([
        nrm(ks[7], (DEPTH, D_MODEL, SSM_WIDTH + 2 * ATT_WIDTH), D_MODEL ** -0.5),
        nrm(ks[8], (DEPTH, D_MODEL, ATT_WIDTH), D_MODEL ** -0.5 * DEEPNORM_BETA)], axis=-1)
    w_out = nrm(ks[9], (DEPTH, MIX_WIDTH, D_MODEL), MIX_WIDTH ** -0.5 * DEEPNORM_BETA)
    ssm_a_re = -0.5 + nrm(ks[10], (DEPTH, SSM_GROUPS, SSM_STATE), 0.01)
    ssm_a_im = math.pi * jnp.arange(SSM_STATE, dtype=F32)[None, None, :] + nrm(ks[11], (DEPTH, SSM_GROUPS, SSM_STATE), 0.01)
    ssm_log_dt = jax.random.uniform(ks[12], (DEPTH, SSM_GROUPS), F32, math.log(DT_MIN), math.log(DT_MAX))
    ssm_b_re = nrm(ks[13], (DEPTH, SSM_GROUPS, SSM_STATE, SSM_GROUP), SSM_GROUP ** -0.5)
    ssm_b_im = nrm(ks[14], (DEPTH, SSM_GROUPS, SSM_STATE, SSM_GROUP), SSM_GROUP ** -0.5)
    ssm_c_re = nrm(ks[15], (DEPTH, SSM_GROUPS, SSM_GROUP, SSM_STATE), SSM_STATE ** -0.5)
    ssm_c_im = nrm(ks[16], (DEPTH, SSM_GROUPS, SSM_GROUP, SSM_STATE), SSM_STATE ** -0.5)
    ssm_d = nrm(ks[17], (DEPTH, SSM_GROUPS, SSM_GROUP), 1.0)
    w_glu = nrm(ks[18], (DEPTH, SSM_WIDTH, 2 * SSM_WIDTH), SSM_WIDTH ** -0.5)
    b_glu = nrm(ks[19], (DEPTH, 2 * SSM_WIDTH), 0.01)
    lambda_q1 = nrm(ks[20], (DEPTH, QK_DIM), 0.1)
    lambda_k1 = nrm(ks[21], (DEPTH, QK_DIM), 0.1)
    lambda_q2 = nrm(ks[22], (DEPTH, QK_DIM), 0.1)
    lambda_k2 = nrm(ks[23], (DEPTH, QK_DIM), 0.1)
    subln_g = 1.0 + nrm(ks[24], (DEPTH, V_DIM), 0.01)
    ln1_g = 1.0 + nrm(ks[25], (DEPTH, D_MODEL), 0.01)
    ln1_b = nrm(ks[26], (DEPTH, D_MODEL), 0.01)
    w_router = nrm(ks[27], (DEPTH, D_MODEL, N_EXPERTS), D_MODEL ** -0.5)
    b_router = nrm(ks[28], (DEPTH, N_EXPERTS), 0.01)
    w_gate_up = nrm(ks[29], (DEPTH, N_EXPERTS, D_MODEL, 2 * D_EXPERT), D_MODEL ** -0.5 * DEEPNORM_BETA)
    b_gate_up = nrm(ks[30], (DEPTH, N_EXPERTS, 2 * D_EXPERT), 0.01)
    w_down = nrm(ks[31], (DEPTH, N_EXPERTS, D_EXPERT, D_MODEL), D_EXPERT ** -0.5 * DEEPNORM_BETA)
    b_down = nrm(ks[32], (DEPTH, N_EXPERTS, D_MODEL), 0.01)
    ln2_g = 1.0 + nrm(ks[33], (DEPTH, D_MODEL), 0.01)
    ln2_b = nrm(ks[34], (DEPTH, D_MODEL), 0.01)
    return {
        'x_prompt': x_prompt, 'x_sample': x_sample,
        'cache_k': cache_k, 'cache_v': cache_v,
        'state_ssm_re': state_ssm_re, 'state_ssm_im': state_ssm_im,
        'page_table': page_table,
        'w_in': w_in, 'w_out': w_out,
        'ssm_a_re': ssm_a_re, 'ssm_a_im': ssm_a_im, 'ssm_log_dt': ssm_log_dt,
        'ssm_b_re': ssm_b_re, 'ssm_b_im': ssm_b_im, 'ssm_c_re': ssm_c_re, 'ssm_c_im': ssm_c_im,
        'ssm_d': ssm_d, 'w_glu': w_glu, 'b_glu': b_glu,
        'lambda_q1': lambda_q1, 'lambda_k1': lambda_k1, 'lambda_q2': lambda_q2, 'lambda_k2': lambda_k2,
        'subln_g': subln_g, 'ln1_g': ln1_g, 'ln1_b': ln1_b,
        'w_router': w_router, 'b_router': b_router,
        'w_gate_up': w_gate_up, 'b_gate_up': b_gate_up, 'w_down': w_down, 'b_down': b_down,
        'ln2_g': ln2_g, 'ln2_b': ln2_b,
    }


def reference(x_prompt, x_sample, cache_k, cache_v, state_ssm_re, state_ssm_im, page_table,
              w_in, w_out, ssm_a_re, ssm_a_im, ssm_log_dt, ssm_b_re, ssm_b_im, ssm_c_re, ssm_c_im,
              ssm_d, w_glu, b_glu, lambda_q1, lambda_k1, lambda_q2, lambda_k2, subln_g, ln1_g, ln1_b,
              w_router, b_router, w_gate_up, b_gate_up, w_down, b_down, ln2_g, ln2_b):
    b_p, l_p = x_prompt.shape[:2]
    b_s, l_s = x_sample.shape[:2]
    past_len = page_table.shape[1] * PAGE_SIZE
    pos_p = jnp.arange(l_p, dtype=jnp.int32)
    pos_s = past_len + jnp.arange(l_s, dtype=jnp.int32)
    k_pos_s = jnp.arange(past_len + l_s, dtype=jnp.int32)
    hp, hs = x_prompt, x_sample
    kp_l, vp_l, srp_l, sip_l = [], [], [], []
    ks_l, vs_l, srs_l, sis_l = [], [], [], []
    for l in range(DEPTH):
        p = dict(w_in=w_in[l], w_out=w_out[l], ssm_a_re=ssm_a_re[l], ssm_a_im=ssm_a_im[l],
                 ssm_log_dt=ssm_log_dt[l], ssm_b_re=ssm_b_re[l], ssm_b_im=ssm_b_im[l],
                 ssm_c_re=ssm_c_re[l], ssm_c_im=ssm_c_im[l], ssm_d=ssm_d[l], w_glu=w_glu[l], b_glu=b_glu[l],
                 lambda_q1=lambda_q1[l], lambda_k1=lambda_k1[l], lambda_q2=lambda_q2[l], lambda_k2=lambda_k2[l],
                 subln_g=subln_g[l], ln1_g=ln1_g[l], ln1_b=ln1_b[l], w_router=w_router[l], b_router=b_router[l],
                 w_gate_up=w_gate_up[l], b_gate_up=b_gate_up[l], w_down=w_down[l], b_down=b_down[l],
                 ln2_g=ln2_g[l], ln2_b=ln2_b[l])
        lam_init = lambda_init_fn(l)
        lam = diff_lambda(p, lam_init)

        u, q, k, v = mixer_inputs(hp, pos_p, p['w_in'])
        z0 = jnp.zeros((b_p, SSM_GROUPS, SSM_STATE), F32)
        ssm_o, s_re, s_im = ssm_mixer(u, z0, z0, p)
        att_o = diff_attend_prompt(q, k, v, lam)
        hp = layer_output(hp, ssm_o, att_o, lam_init, p)
        kp_l.append(k.reshape(b_p, l_p, DIFF_HEADS, 2 * QK_DIM))
        vp_l.append(v)
        srp_l.append(s_re)
        sip_l.append(s_im)

        u, q, k, v = mixer_inputs(hs, pos_s, p['w_in'])
        ssm_o, s_re, s_im = ssm_mixer(u, state_ssm_re[l], state_ssm_im[l], p)
        k_past = cache_k[l][page_table].reshape(b_s, past_len, DIFF_HEADS, 2, QK_DIM).astype(k.dtype)
        v_past = cache_v[l][page_table].reshape(b_s, past_len, DIFF_HEADS, V_DIM).astype(v.dtype)
        k_all = jnp.concatenate([k_past, k], axis=1)
        v_all = jnp.concatenate([v_past, v], axis=1)
        att_o = diff_attend(q, k_all, v_all, pos_s, k_pos_s, lam)
        hs = layer_output(hs, ssm_o, att_o, lam_init, p)
        ks_l.append(k.reshape(b_s, l_s, DIFF_HEADS, 2 * QK_DIM))
        vs_l.append(v)
        srs_l.append(s_re)
        sis_l.append(s_im)

    return (hp, hs,
            jnp.stack(kp_l), jnp.stack(vp_l), jnp.stack(srp_l), jnp.stack(sip_l),
            jnp.stack(ks_l), jnp.stack(vs_l), jnp.stack(srs_l), jnp.stack(sis_l))
```

```python
import functools
import math

import jax
import jax.numpy as jnp
from jax import lax
from jax.experimental import pallas as pl
from jax.experimental.pallas import tpu as pltpu

F32 = jnp.float32
BF16 = jnp.bfloat16
I32 = jnp.int32

TOP_K = 4
ROPE_THETA = 10000.0
SWIGLU_ALPHA = 1.702
SWIGLU_LIMIT = 7.0
LN_EPS = 1e-5
RMS_EPS = 1e-5
SSM_A_RE_MAX = -1e-4
GELU_C0 = math.sqrt(2.0 / math.pi)
GELU_C1 = 0.044715
NEG_BIG = -1e30

LANES = 128
SUBLANES = 8
MXU_DIM = 256
VMEM_LIMIT_BYTES = 56 * 1024 * 1024

PROJ_ROWS = 512
SSM_STEPS = 64
ATT_ROWS = 256
OUT_ROWS = 512
MOE_ROWS = 256
FINAL_ROWS = 128
SCAN_LANES = 512


def _params(*sem):
    return pltpu.CompilerParams(dimension_semantics=sem, vmem_limit_bytes=VMEM_LIMIT_BYTES)


def _dot(a, b):
    return jnp.dot(a, b, preferred_element_type=F32)


def _dot_nt(a, b):
    return lax.dot_general(a, b, (((1,), (1,)), ((), ())), preferred_element_type=F32)


def _proj_kernel(x_ref, w_ref, cos_ref, sin_ref, u_ref, q_ref, k_ref, v_ref, *, sw, aw, half):
    acc = _dot(x_ref[...].astype(BF16), w_ref[...])
    u_ref[...] = acc[:, :sw]
    c = cos_ref[...]
    s = sin_ref[...]
    lane = lax.broadcasted_iota(I32, c.shape, 1)
    first = (lane % (2 * half)) < half

    def rope(blk):
        swapped = jnp.where(first, pltpu.roll(blk, LANES - half, 1), pltpu.roll(blk, half, 1))
        return blk * c + swapped * s

    for j in range(aw // LANES):
        lo = sw + j * LANES
        q_ref[:, j * LANES:(j + 1) * LANES] = rope(acc[:, lo:lo + LANES])
        lo = sw + aw + j * LANES
        k_ref[:, j * LANES:(j + 1) * LANES] = rope(acc[:, lo:lo + LANES])
    v_ref[...] = acc[:, sw + 2 * aw:]


def _project(x2d, w_bf, cos_t, sin_t, *, nb, rows, sw, aw, half):
    n_rows, d = x2d.shape
    seq = n_rows // nb
    n_tb = seq // rows
    grid = (nb * n_tb,)
    kern = functools.partial(_proj_kernel, sw=sw, aw=aw, half=half)
    return pl.pallas_call(
        kern,
        grid=grid,
        in_specs=[
            pl.BlockSpec((rows, d), lambda i: (i, 0)),
            pl.BlockSpec(w_bf.shape, lambda i: (0, 0)),
            pl.BlockSpec((rows, LANES), lambda i: (i % n_tb, 0)),
            pl.BlockSpec((rows, LANES), lambda i: (i % n_tb, 0)),
        ],
        out_specs=[
            pl.BlockSpec((rows, sw), lambda i: (i % n_tb, i // n_tb)),
            pl.BlockSpec((rows, aw), lambda i: (i, 0)),
            pl.BlockSpec((rows, aw), lambda i: (i, 0)),
            pl.BlockSpec((rows, aw), lambda i: (i, 0)),
        ],
        out_shape=[
            jax.ShapeDtypeStruct((seq, nb * sw), F32),
            jax.ShapeDtypeStruct((n_rows, aw), F32),
            jax.ShapeDtypeStruct((n_rows, aw), F32),
            jax.ShapeDtypeStruct((n_rows, aw), F32),
        ],
        compiler_params=_params("arbitrary"),
        name="proj_rope",
    )(x2d, w_bf, cos_t, sin_t)


def _rope_tables(pos, half):
    inv = ROPE_THETA ** (-jnp.arange(half, dtype=F32) / half)
    ang = pos.astype(F32)[:, None] * inv[None, :]
    reps = LANES // half
    cos = jnp.tile(jnp.cos(ang), (1, reps))
    sin = jnp.sin(ang)
    sin = jnp.tile(jnp.concatenate([-sin, sin], axis=1), (1, reps // 2))
    return cos, sin


def _ssm_kernel(u_ref, s0r_ref, s0i_ref, ar_ref, ai_ref, wb_ref, wc_ref, d_ref, wg_ref, bg_ref,
                o_ref, sr_ref, si_ref, bu_ref, st_ref, ot_ref, *, nb, steps, gp, sw):
    i = pl.program_id(0)
    rows = nb * steps

    @pl.when(i == 0)
    def _():
        st_ref[:, :gp] = s0r_ref[...]
        st_ref[:, gp:] = s0i_ref[...]

    u = u_ref[...]
    ub = u.astype(BF16)
    n_tiles = 2 * gp // MXU_DIM
    per_half = gp // MXU_DIM
    in_tiles = sw // MXU_DIM
    for j in range(n_tiles):
        kt = ((j % per_half) * in_tiles) // per_half
        bu_ref[:, j * MXU_DIM:(j + 1) * MXU_DIM] = _dot(ub[:, kt * MXU_DIM:(kt + 1) * MXU_DIM], wb_ref[j])

    for c in range(gp // SCAN_LANES):
        lo_r = c * SCAN_LANES
        lo_i = gp + c * SCAN_LANES
        ar = jnp.broadcast_to(ar_ref[:, lo_r:lo_r + SCAN_LANES], (nb, SCAN_LANES))
        ai = jnp.broadcast_to(ai_ref[:, lo_r:lo_r + SCAN_LANES], (nb, SCAN_LANES))

        def step(t, carry, lo_r=lo_r, lo_i=lo_i, ar=ar, ai=ai):
            sr, si = carry
            r0 = pl.multiple_of(t * nb, nb)
            br = bu_ref[pl.ds(r0, nb), lo_r:lo_r + SCAN_LANES]
            bi = bu_ref[pl.ds(r0, nb), lo_i:lo_i + SCAN_LANES]
            nr = ar * sr - ai * si + br
            ni = ar * si + ai * sr + bi
            bu_ref[pl.ds(r0, nb), lo_r:lo_r + SCAN_LANES] = nr
            bu_ref[pl.ds(r0, nb), lo_i:lo_i + SCAN_LANES] = ni
            return nr, ni

        sr0 = st_ref[:, lo_r:lo_r + SCAN_LANES]
        si0 = st_ref[:, lo_i:lo_i + SCAN_LANES]
        if steps == 1:
            sr1, si1 = step(0, (sr0, si0))
        else:
            sr1, si1 = lax.fori_loop(0, steps, step, (sr0, si0))
        st_ref[:, lo_r:lo_r + SCAN_LANES] = sr1
        st_ref[:, lo_i:lo_i + SCAN_LANES] = si1

    sr_ref[...] = st_ref[:, :gp]
    si_ref[...] = st_ref[:, gp:]

    out_tiles = sw // MXU_DIM
    kw = gp // out_tiles
    ys = []
    for nt in range(out_tiles):
        s_re = bu_ref[:, nt * kw:(nt + 1) * kw].astype(BF16)
        s_im = bu_ref[:, gp + nt * kw:gp + (nt + 1) * kw].astype(BF16)
        ys.append(_dot(s_re, wc_ref[nt, 0]) + _dot(s_im, wc_ref[nt, 1]))
    y = jnp.concatenate(ys, axis=1) + d_ref[...] * u
    y = 0.5 * y * (1.0 + jnp.tanh(GELU_C0 * (y + GELU_C1 * (y * y * y))))
    z = _dot(y.astype(BF16), wg_ref[...]) + bg_ref[...]
    out = z[:, :sw] * jax.nn.sigmoid(z[:, sw:])
    if steps == 1:
        o_ref[...] = out
    else:
        for j in range(sw // LANES):
            ot_ref[j] = out[:, j * LANES:(j + 1) * LANES]
        for b in range(nb):
            for j in range(sw // LANES):
                o_ref[b, :, j * LANES:(j + 1) * LANES] = ot_ref[j, pl.ds(b, steps, stride=nb), :]


def _ssm(u_tb, s0r, s0i, prm, *, nb, steps, w_glu_bf, b_glu):
    n_rows, sw = u_tb.shape
    seq = n_rows // nb
    gp = prm["a_re"].shape[1]
    rows = nb * steps
    grid = (seq // steps,)
    const2 = lambda i: (0, 0)
    kern = functools.partial(_ssm_kernel, nb=nb, steps=steps, gp=gp, sw=sw)
    if steps == 1:
        o_spec = pl.BlockSpec((nb, sw), const2)
        o_shape = jax.ShapeDtypeStruct((nb, sw), F32)
    else:
        o_spec = pl.BlockSpec((nb, steps, sw), lambda i: (0, i, 0))
        o_shape = jax.ShapeDtypeStruct((nb, seq, sw), F32)
    return pl.pallas_call(
        kern,
        grid=grid,
        in_specs=[
            pl.BlockSpec((rows, sw), lambda i: (i, 0)),
            pl.BlockSpec((nb, gp), const2),
            pl.BlockSpec((nb, gp), const2),
            pl.BlockSpec((1, gp), const2),
            pl.BlockSpec((1, gp), const2),
            pl.BlockSpec(prm["wb"].shape, lambda i: (0, 0, 0)),
            pl.BlockSpec(prm["wc"].shape, lambda i: (0, 0, 0, 0)),
            pl.BlockSpec((1, sw), const2),
            pl.BlockSpec(w_glu_bf.shape, const2),
            pl.BlockSpec((1, 2 * sw), const2),
        ],
        out_specs=[o_spec, pl.BlockSpec((nb, gp), const2), pl.BlockSpec((nb, gp), const2)],
        out_shape=[o_shape, jax.ShapeDtypeStruct((nb, gp), F32), jax.ShapeDtypeStruct((nb, gp), F32)],
        scratch_shapes=[
            pltpu.VMEM((rows, 2 * gp), F32),
            pltpu.VMEM((nb, 2 * gp), F32),
            pltpu.VMEM((sw // LANES, rows, LANES), F32),
        ],
        compiler_params=_params("arbitrary"),
        name="ssm_mixer",
    )(u_tb, s0r, s0i, prm["a_re"], prm["a_im"], prm["wb"], prm["wc"], prm["d"], w_glu_bf, b_glu)


def _ssm_params(a_re, a_im, log_dt, b_re, b_im, c_re, c_im, d):
    g_, p_ = a_re.shape
    h_ = b_re.shape[-1]
    gp = g_ * p_
    sw = g_ * h_
    a_re = jnp.minimum(a_re.astype(F32), SSM_A_RE_MAX)
    a_im = a_im.astype(F32)
    dt = jnp.exp(log_dt.astype(F32))[:, None]
    mag = jnp.exp(a_re * dt)
    lb_re = mag * jnp.cos(a_im * dt)
    lb_im = mag * jnp.sin(a_im * dt)
    den = jnp.square(a_re) + jnp.square(a_im)
    nr, ni = lb_re - 1.0, lb_im
    f_re = (nr * a_re + ni * a_im) / den
    f_im = (ni * a_re - nr * a_im) / den
    b_re = b_re.astype(F32)
    b_im = b_im.astype(F32)
    bb_re = f_re[..., None] * b_re - f_im[..., None] * b_im
    bb_im = f_re[..., None] * b_im + f_im[..., None] * b_re
    eye = jnp.eye(g_, dtype=F32)
    w_in_re = jnp.einsum("gph,gk->ghkp", bb_re, eye).reshape(sw, gp)
    w_in_im = jnp.einsum("gph,gk->ghkp", bb_im, eye).reshape(sw, gp)
    w_in = jnp.concatenate([w_in_re, w_in_im], axis=1)
    per_half = gp // MXU_DIM
    in_tiles = sw // MXU_DIM
    tiles = []
    for j in range(2 * per_half):
        kt = ((j % per_half) * in_tiles) // per_half
        tiles.append(w_in[kt * MXU_DIM:(kt + 1) * MXU_DIM, j * MXU_DIM:(j + 1) * MXU_DIM])
    wb = jnp.stack(tiles).astype(BF16)
    w_out_re = jnp.einsum("ghp,gk->gpkh", c_re.astype(F32), eye).reshape(gp, sw)
    w_out_im = -jnp.einsum("ghp,gk->gpkh", c_im.astype(F32), eye).reshape(gp, sw)
    out_tiles = sw // MXU_DIM
    kw = gp // out_tiles
    wc = jnp.stack([
        jnp.stack([w_out_re[nt * kw:(nt + 1) * kw, nt * MXU_DIM:(nt + 1) * MXU_DIM],
                   w_out_im[nt * kw:(nt + 1) * kw, nt * MXU_DIM:(nt + 1) * MXU_DIM]])
        for nt in range(out_tiles)]).astype(BF16)
    return dict(a_re=lb_re.reshape(1, gp), a_im=lb_im.reshape(1, gp), wb=wb, wc=wc,
                d=d.astype(F32).reshape(1, sw))


def _diff_lambda(lq1, lk1, lq2, lk2, lam_init):
    l1 = jnp.exp(jnp.sum(lq1 * lk1, axis=1, keepdims=True))
    l2 = jnp.exp(jnp.sum(lq2 * lk2, axis=1, keepdims=True))
    return l1 - l2 + lam_init


def _head_norm(o, g, lam_init):
    ms = jnp.mean(o * o, axis=-1, keepdims=True)
    return o * lax.rsqrt(ms + RMS_EPS) * g * (1.0 - lam_init)


def _attn_prompt_kernel(q_ref, k_ref, v_ref, lq1_ref, lk1_ref, lq2_ref, lk2_ref, g_ref, o_ref,
                        kb_ref, vb_ref, *, tq, qk, lam_init, scale):
    qi = pl.program_id(2)

    @pl.when(qi == 0)
    def _():
        kb_ref[...] = k_ref[...].astype(BF16)
        vb_ref[...] = v_ref[...].astype(BF16)

    q = q_ref[...] * scale
    lane = lax.broadcasted_iota(I32, q.shape, 1)
    q1 = jnp.where(lane < qk, q, 0.0).astype(BF16)
    q2 = jnp.where(lane >= qk, q, 0.0).astype(BF16)
    vd = vb_ref.shape[1]

    def update(s, m, l, acc, vj):
        m_new = jnp.maximum(m, jnp.max(s, axis=1, keepdims=True))
        alpha = jnp.exp(m - m_new)
        p = jnp.exp(s - m_new)
        l_new = alpha * l + jnp.sum(p, axis=1, keepdims=True)
        acc_new = alpha * acc + _dot(p.astype(BF16), vj)
        return m_new, l_new, acc_new

    def block(j, carry, masked):
        m1, l1, a1, m2, l2, a2 = carry
        r0 = pl.multiple_of(j * tq, tq)
        kj = kb_ref[pl.ds(r0, tq), :]
        vj = vb_ref[pl.ds(r0, tq), :]
        s1 = _dot_nt(q1, kj)
        s2 = _dot_nt(q2, kj)
        if masked:
            row = lax.broadcasted_iota(I32, s1.shape, 0)
            col = lax.broadcasted_iota(I32, s1.shape, 1)
            keep = col <= row
            s1 = jnp.where(keep, s1, NEG_BIG)
            s2 = jnp.where(keep, s2, NEG_BIG)
        m1, l1, a1 = update(s1, m1, l1, a1, vj)
        m2, l2, a2 = update(s2, m2, l2, a2, vj)
        return m1, l1, a1, m2, l2, a2

    m0 = jnp.full((tq, 1), NEG_BIG, F32)
    l0 = jnp.zeros((tq, 1), F32)
    a0 = jnp.zeros((tq, vd), F32)
    carry = lax.fori_loop(0, qi, lambda j, c: block(j, c, False), (m0, l0, a0, m0, l0, a0))
    m1, l1, a1, m2, l2, a2 = block(qi, carry, True)
    lam = _diff_lambda(lq1_ref[...], lk1_ref[...], lq2_ref[...], lk2_ref[...], lam_init)
    o = a1 / l1 - lam * (a2 / l2)
    o_ref[...] = _head_norm(o, g_ref[...], lam_init)


def _attn_prompt(q, k, v, lam_p, g, *, nb, heads, lam_init):
    n_rows, aw = q.shape
    seq = n_rows // nb
    hd = aw // heads
    qk = hd // 2
    tq = min(ATT_ROWS, seq)
    nq = seq // tq
    kern = functools.partial(_attn_prompt_kernel, tq=tq, qk=qk, lam_init=lam_init, scale=qk ** -0.5)
    small = pl.BlockSpec((1, qk), lambda b, h, i: (0, 0))
    return pl.pallas_call(
        kern,
        grid=(nb, heads, nq),
        in_specs=[
            pl.BlockSpec((tq, hd), lambda b, h, i: (b * nq + i, h)),
            pl.BlockSpec((seq, hd), lambda b, h, i: (b, h)),
            pl.BlockSpec((seq, hd), lambda b, h, i: (b, h)),
            small, small, small, small,
            pl.BlockSpec((1, hd), lambda b, h, i: (0, 0)),
        ],
        out_specs=pl.BlockSpec((tq, hd), lambda b, h, i: (b * nq + i, h)),
        out_shape=jax.ShapeDtypeStruct((n_rows, aw), F32),
        scratch_shapes=[pltpu.VMEM((seq, hd), BF16), pltpu.VMEM((seq, hd), BF16)],
        compiler_params=_params("arbitrary", "arbitrary", "arbitrary"),
        name="diff_attn_prompt",
    )(q, k, v, *lam_p, g)


def _attn_sample_kernel(pt_ref, q_ref, ks_ref, vs_ref, lq1_ref, lk1_ref, lq2_ref, lk2_ref, g_ref, *rest,
                        n_pages, heads, qk, lam_init, scale):
    k_pages = rest[:n_pages]
    v_pages = rest[n_pages:2 * n_pages]
    o_ref = rest[2 * n_pages]
    del pt_ref
    hd = 2 * qk
    aw = heads * hd
    n_maps = 2 * heads
    q = q_ref[...] * scale
    row = lax.broadcasted_iota(I32, (n_maps, aw), 0)
    lane = lax.broadcasted_iota(I32, (n_maps, aw), 1)
    own = (lane // hd == row % heads) & ((lane // qk) % 2 == row // heads)
    qm = jnp.where(own, jnp.broadcast_to(q, (n_maps, aw)), 0.0)
    qmb = qm.astype(BF16)
    s_pages = [_dot_nt(qmb, kp[...].astype(BF16)) for kp in k_pages]
    s_self = jnp.sum(qm * ks_ref[...], axis=1, keepdims=True)
    m = s_self
    for s in s_pages:
        m = jnp.maximum(m, jnp.max(s, axis=1, keepdims=True))
    p_self = jnp.exp(s_self - m)
    p_pages = [jnp.exp(s - m) for s in s_pages]
    l = p_self
    for p in p_pages:
        l = l + jnp.sum(p, axis=1, keepdims=True)
    inv = 1.0 / l
    lam = _diff_lambda(lq1_ref[...], lk1_ref[...], lq2_ref[...], lk2_ref[...], lam_init)

    def mix(p):
        pn = p * inv
        return pn - lam * pltpu.roll(pn, heads, 0)

    acc = jnp.broadcast_to(mix(p_self), (n_maps, aw)) * vs_ref[...]
    for p, vp in zip(p_pages, v_pages):
        acc = acc + _dot(mix(p).astype(BF16), vp[...].astype(BF16))
    pick = (row < heads) & (lane // hd == row)
    o = jnp.sum(jnp.where(pick, acc, 0.0), axis=0, keepdims=True)
    g = g_ref[...]
    for h in range(heads):
        o_ref[:, h * hd:(h + 1) * hd] = _head_norm(o[:, h * hd:(h + 1) * hd], g, lam_init)


def _attn_sample(q, k_new, v_new, cache_k, cache_v, page_table, lam_p, g, *, heads, lam_init):
    bs, aw = q.shape
    n_pages = page_table.shape[1]
    n_phys, page = cache_k.shape[:2]
    hd = aw // heads
    qk = hd // 2
    ck = cache_k.reshape(n_phys, page, aw)
    cv = cache_v.reshape(n_phys, page, aw)
    kern = functools.partial(_attn_sample_kernel, n_pages=n_pages, heads=heads, qk=qk, lam_init=lam_init,
                             scale=qk ** -0.5)
    row_spec = pl.BlockSpec((None, 1, aw), lambda b, pt: (b, 0, 0))
    small = pl.BlockSpec((1, qk), lambda b, pt: (0, 0))

    def page_spec(j):
        return pl.BlockSpec((None, page, aw), lambda b, pt, j=j: (pt[b * n_pages + j], 0, 0))

    grid_spec = pltpu.PrefetchScalarGridSpec(
        num_scalar_prefetch=1,
        grid=(bs,),
        in_specs=[row_spec, row_spec, row_spec, small, small, small, small,
                  pl.BlockSpec((1, hd), lambda b, pt: (0, 0))]
                 + [page_spec(j) for j in range(n_pages)] * 2,
        out_specs=row_spec,
    )
    out = pl.pallas_call(
        kern,
        grid_spec=grid_spec,
        out_shape=jax.ShapeDtypeStruct((bs, 1, aw), F32),
        compiler_params=_params("arbitrary"),
        name="diff_attn_sample",
    )(page_table.reshape(-1).astype(I32), q.reshape(bs, 1, aw), k_new.reshape(bs, 1, aw),
      v_new.reshape(bs, 1, aw), *lam_p, g, *([ck] * n_pages), *([cv] * n_pages))
    return out.reshape(bs, aw)


def _layer_norm(z, g, b):
    mu = jnp.mean(z, axis=-1, keepdims=True)
    zc = z - mu
    var = jnp.mean(zc * zc, axis=-1, keepdims=True)
    return zc * lax.rsqrt(var + LN_EPS) * g + b


def _out_kernel(xp_ref, sp_ref, ap_ref, xs_ref, ss_ref, as_ref, wo_ref, g_ref, b_ref, wrh_ref, wrl_ref,
                br_ref, h_ref, e_ref, gt_ref, *, n_prompt, sw, alpha, top_k):
    i = pl.program_id(0)
    is_s = i == n_prompt
    x = jnp.where(is_s, xs_ref[...], xp_ref[...])
    s = jnp.where(is_s, ss_ref[...], sp_ref[...]).astype(BF16)
    a = jnp.where(is_s, as_ref[...], ap_ref[...]).astype(BF16)
    mix = _dot(s, wo_ref[:sw, :]) + _dot(a, wo_ref[sw:, :])
    h = _layer_norm(alpha * x + mix, g_ref[...], b_ref[...])
    h_ref[...] = h
    h_hi = h.astype(BF16)
    h_lo = (h - h_hi.astype(F32)).astype(BF16)
    lg = (_dot_nt(wrh_ref[...], h_hi) + _dot_nt(wrl_ref[...], h_hi) + _dot_nt(wrh_ref[...], h_lo)
          + br_ref[...])
    eio = lax.broadcasted_iota(I32, lg.shape, 0)
    n_exp = lg.shape[0]
    vals, idxs = [], []
    for _ in range(top_k):
        m = jnp.max(lg, axis=0, keepdims=True)
        idx = jnp.min(jnp.where(lg == m, eio, n_exp), axis=0, keepdims=True)
        vals.append(m)
        idxs.append(idx)
        lg = jnp.where(eio == idx, NEG_BIG, lg)
    tv = jnp.concatenate(vals, axis=0)
    pe = jnp.exp(tv - tv[0:1, :])
    gt_ref[...] = pe / jnp.sum(pe, axis=0, keepdims=True)
    e_ref[...] = jnp.concatenate(idxs, axis=0)


def _out_route(xp, sp, ap, xs, ss, as_, w_out_bf, g, b, wr_hi, wr_lo, br, *, alpha):
    tp, d = xp.shape
    sw = sp.shape[1]
    rows = OUT_ROWS
    n_prompt = tp // rows
    total = tp + rows
    n_exp = wr_hi.shape[0]
    pidx = lambda i: (jnp.minimum(i, n_prompt - 1), 0)
    const = lambda i: (0, 0)
    kern = functools.partial(_out_kernel, n_prompt=n_prompt, sw=sw, alpha=alpha, top_k=TOP_K)
    return pl.pallas_call(
        kern,
        grid=(n_prompt + 1,),
        in_specs=[
            pl.BlockSpec((rows, d), pidx), pl.BlockSpec((rows, sw), pidx), pl.BlockSpec((rows, d - sw), pidx),
            pl.BlockSpec((rows, d), const), pl.BlockSpec((rows, sw), const), pl.BlockSpec((rows, d - sw), const),
            pl.BlockSpec(w_out_bf.shape, const), pl.BlockSpec((1, d), const), pl.BlockSpec((1, d), const),
            pl.BlockSpec((n_exp, d), const), pl.BlockSpec((n_exp, d), const), pl.BlockSpec((n_exp, 1), const),
        ],
        out_specs=[
            pl.BlockSpec((rows, d), lambda i: (i, 0)),
            pl.BlockSpec((TOP_K, rows), lambda i: (0, i)),
            pl.BlockSpec((TOP_K, rows), lambda i: (0, i)),
        ],
        out_shape=[
            jax.ShapeDtypeStruct((total, d), F32),
            jax.ShapeDtypeStruct((TOP_K, total), I32),
            jax.ShapeDtypeStruct((TOP_K, total), F32),
        ],
        compiler_params=_params("arbitrary"),
        name="out_proj_route",
    )(xp, sp, ap, xs, ss, as_, w_out_bf, g, b, wr_hi, wr_lo, br)


def _moe_kernel(be_ref, bv_ref, tok_ref, slot_ref, gate_ref, h_hbm, wgu_ref, bgu_ref, wdn_ref, bdn_ref,
                y_hbm, xbuf, ybuf, wgu_bf, wdn_bf, gsem, ssem, *, de):
    i = pl.program_id(0)
    n_valid = bv_ref[i]

    @pl.when(i == 0)
    def _():
        xbuf[...] = jnp.zeros(xbuf.shape, xbuf.dtype)

    @pl.when(n_valid > 0)
    def _():
        def row_in(r):
            return pltpu.make_async_copy(h_hbm.at[pl.ds(tok_ref[0, r], 1), :], xbuf.at[pl.ds(r, 1), :], gsem)

        def row_out(r):
            return pltpu.make_async_copy(ybuf.at[pl.ds(r, 1), :], y_hbm.at[pl.ds(slot_ref[0, r], 1), :], ssem)

        def each(fn):
            def body(r, c):
                fn(r)
                return c
            lax.fori_loop(0, n_valid, body, 0)

        each(lambda r: row_in(r).start())

        @pl.when((i == 0) | (be_ref[i] != be_ref[jnp.maximum(i - 1, 0)]))
        def _():
            wgu_bf[...] = wgu_ref[...].astype(BF16)
            wdn_bf[...] = wdn_ref[...].astype(BF16)

        each(lambda r: row_in(r).wait())

        x = xbuf[...].astype(BF16)
        hh = _dot(x, wgu_bf[...]) + bgu_ref[...]
        glu = jnp.minimum(hh[:, :de], SWIGLU_LIMIT)
        lin = jnp.clip(hh[:, de:], -SWIGLU_LIMIT, SWIGLU_LIMIT)
        act = glu * jax.nn.sigmoid(SWIGLU_ALPHA * glu) * (lin + 1.0)
        y = _dot(act.astype(BF16), wdn_bf[...]) + bdn_ref[...]
        ybuf[...] = y * gate_ref[...]

        each(lambda r: row_out(r).start())
        each(lambda r: row_out(r).wait())


def _moe(h_all, block_expert, block_valid, tok_buf, slot_buf, gate_buf, w_gu, b_gu, w_dn, b_dn, *, n_slots):
    n_blocks = block_expert.shape[0]
    rows = MOE_ROWS
    d = h_all.shape[1]
    n_exp, _, de2 = w_gu.shape
    de = de2 // 2
    kern = functools.partial(_moe_kernel, de=de)
    smem_rows = pl.BlockSpec((None, 1, rows), lambda i, be, nu: (i, 0, 0), memory_space=pltpu.SMEM)
    grid_spec = pltpu.PrefetchScalarGridSpec(
        num_scalar_prefetch=2,
        grid=(n_blocks,),
        in_specs=[
            smem_rows, smem_rows,
            pl.BlockSpec((rows, 1), lambda i, be, nu: (i, 0)),
            pl.BlockSpec(memory_space=pl.ANY),
            pl.BlockSpec((None, d, de2), lambda i, be, nu: (be[i], 0, 0)),
            pl.BlockSpec((None, 1, de2), lambda i, be, nu: (be[i], 0, 0)),
            pl.BlockSpec((None, de, d), lambda i, be, nu: (be[i], 0, 0)),
            pl.BlockSpec((None, 1, d), lambda i, be, nu: (be[i], 0, 0)),
        ],
        out_specs=pl.BlockSpec(memory_space=pl.ANY),
        scratch_shapes=[
            pltpu.VMEM((rows, d), F32), pltpu.VMEM((rows, d), F32),
            pltpu.VMEM((d, de2), BF16), pltpu.VMEM((de, d), BF16),
            pltpu.SemaphoreType.DMA, pltpu.SemaphoreType.DMA,
        ],
    )
    return pl.pallas_call(
        kern,
        grid_spec=grid_spec,
        out_shape=jax.ShapeDtypeStruct((n_slots, d), F32),
        compiler_params=_params("arbitrary"),
        name="moe_experts",
    )(block_expert, block_valid, tok_buf.reshape(n_blocks, 1, rows), slot_buf.reshape(n_blocks, 1, rows),
      gate_buf.reshape(n_blocks * rows, 1), h_all, w_gu, b_gu.reshape(n_exp, 1, de2), w_dn,
      b_dn.reshape(n_exp, 1, d))


def _route_plan(e_idx, gates, n_tokens, n_exp, rows):
    k_ = e_idx.shape[0]
    n_assign = k_ * n_tokens
    e_flat = e_idx[:, :n_tokens].reshape(n_assign)
    g_flat = gates[:, :n_tokens].reshape(n_assign)
    onehot = (e_flat[:, None] == jnp.arange(n_exp, dtype=I32)[None, :]).astype(I32)
    csum = jnp.cumsum(onehot, axis=0)
    counts = csum[-1]
    rank = jnp.sum((csum - onehot) * onehot, axis=1)
    padded = ((counts + rows - 1) // rows) * rows
    pend = jnp.cumsum(padded)
    pstart = pend - padded
    dest = pstart[e_flat] + rank
    n_blocks = -(-(n_assign + n_exp * (rows - 1)) // rows)
    n_rows = n_blocks * rows
    a_id = jnp.arange(n_assign, dtype=I32)
    tok_buf = jnp.zeros((n_rows,), I32).at[dest].set(a_id % n_tokens)
    slot_buf = jnp.zeros((n_rows,), I32).at[dest].set(a_id)
    gate_buf = jnp.zeros((n_rows,), F32).at[dest].set(g_flat)
    block_start = jnp.arange(n_blocks, dtype=I32) * rows
    block_expert = jnp.minimum(jnp.searchsorted(pend, block_start, side="right"), n_exp - 1).astype(I32)
    block_end = (pstart + counts)[block_expert]
    block_valid = jnp.clip(block_end - block_start, 0, rows).astype(I32)
    return tok_buf, slot_buf, gate_buf, block_expert, block_valid


def _final_kernel(*refs, top_k, n_prompt, alpha):
    y_refs = refs[:top_k]
    h_ref, g_ref, b_ref, op_ref, os_ref = refs[top_k:]
    i = pl.program_id(0)
    f = y_refs[0][...]
    for r in y_refs[1:]:
        f = f + r[...]
    out = _layer_norm(alpha * h_ref[...] + f, g_ref[...], b_ref[...])

    @pl.when(i < n_prompt)
    def _():
        op_ref[...] = out

    @pl.when(i == n_prompt)
    def _():
        os_ref[...] = out


def _final(y_slots, h_all, g, b, *, tp, ts, alpha):
    rows = FINAL_ROWS
    assert ts == rows and tp % rows == 0
    d = h_all.shape[1]
    n_prompt = tp // rows
    n_tok_blocks = n_prompt + 1
    const = lambda i: (0, 0)
    kern = functools.partial(_final_kernel, top_k=TOP_K, n_prompt=n_prompt, alpha=alpha)
    y_specs = [pl.BlockSpec((rows, d), lambda i, k=k: (k * n_tok_blocks + i, 0)) for k in range(TOP_K)]
    return pl.pallas_call(
        kern,
        grid=(n_tok_blocks,),
        in_specs=y_specs + [pl.BlockSpec((rows, d), lambda i: (i, 0)), pl.BlockSpec((1, d), const),
                            pl.BlockSpec((1, d), const)],
        out_specs=[pl.BlockSpec((rows, d), lambda i: (jnp.minimum(i, n_prompt - 1), 0)),
                   pl.BlockSpec((rows, d), const)],
        out_shape=[jax.ShapeDtypeStruct((tp, d), F32), jax.ShapeDtypeStruct((ts, d), F32)],
        compiler_params=_params("arbitrary"),
        name="combine_ln",
    )(*([y_slots] * TOP_K), h_all, g, b)


def _pad_rows(a, rows):
    return jnp.pad(a, ((0, rows - a.shape[0]), (0, 0)))


def kernel(x_prompt, x_sample, cache_k, cache_v, state_ssm_re, state_ssm_im, page_table, w_in, w_out, ssm_a_re, ssm_a_im, ssm_log_dt, ssm_b_re, ssm_b_im, ssm_c_re, ssm_c_im, ssm_d, w_glu, b_glu, lambda_q1, lambda_k1, lambda_q2, lambda_k2, subln_g, ln1_g, ln1_b, w_router, b_router, w_gate_up, b_gate_up, w_down, b_down, ln2_g, ln2_b):
    depth = w_in.shape[0]
    bp, lp, d = x_prompt.shape
    bs, ls, _ = x_sample.shape
    assert ls == 1
    heads = cache_k.shape[3]
    hd = cache_k.shape[4]
    qk = hd // 2
    aw = heads * hd
    groups, n_state = ssm_a_re.shape[1:]
    sw = groups * ssm_b_re.shape[-1]
    gp = groups * n_state
    n_exp = w_router.shape[2]
    past_len = page_table.shape[1] * cache_k.shape[2]
    alpha = (2 * depth) ** 0.25
    tp = bp * lp

    cos_p, sin_p = _rope_tables(jnp.arange(lp, dtype=I32), qk // 2)
    cos_s, sin_s = _rope_tables(jnp.full((bs,), past_len, I32), qk // 2)

    hp = x_prompt.reshape(tp, d)
    hs = x_sample.reshape(bs, d)
    outs = [[] for _ in range(8)]
    for l in range(depth):
        lam_init = 0.8 - 0.6 * math.exp(-0.3 * l)
        w_in_bf = w_in[l].astype(BF16)
        w_out_bf = w_out[l].astype(BF16)
        w_glu_bf = w_glu[l].astype(BF16)
        bg = b_glu[l].reshape(1, 2 * sw)
        lam_p = [p[l].reshape(1, qk).astype(F32) for p in (lambda_q1, lambda_k1, lambda_q2, lambda_k2)]
        g_sub = subln_g[l].reshape(1, hd)
        prm = _ssm_params(ssm_a_re[l], ssm_a_im[l], ssm_log_dt[l], ssm_b_re[l], ssm_b_im[l],
                          ssm_c_re[l], ssm_c_im[l], ssm_d[l])

        u_p, q_p, k_p, v_p = _project(hp, w_in_bf, cos_p, sin_p, nb=bp, rows=min(PROJ_ROWS, lp), sw=sw, aw=aw,
                                      half=qk // 2)
        zero = jnp.zeros((bp, gp), F32)
        ssm_p, sre_p, sim_p = _ssm(u_p.reshape(lp * bp, sw), zero, zero, prm, nb=bp,
                                   steps=min(SSM_STEPS, lp), w_glu_bf=w_glu_bf, b_glu=bg)
        att_p = _attn_prompt(q_p, k_p, v_p, lam_p, g_sub, nb=bp, heads=heads, lam_init=lam_init)

        u_s, q_s, k_s, v_s = _project(hs, w_in_bf, cos_s, sin_s, nb=1, rows=bs, sw=sw, aw=aw, half=qk // 2)
        ssm_s, sre_s, sim_s = _ssm(u_s, state_ssm_re[l].reshape(bs, gp), state_ssm_im[l].reshape(bs, gp), prm,
                                   nb=bs, steps=1, w_glu_bf=w_glu_bf, b_glu=bg)
        att_s = _attn_sample(q_s, k_s, v_s, cache_k[l], cache_v[l], page_table, lam_p, g_sub, heads=heads,
                             lam_init=lam_init)

        wr = w_router[l].T.astype(F32)
        wr_hi = wr.astype(BF16)
        wr_lo = (wr - wr_hi.astype(F32)).astype(BF16)
        h_all, e_idx, gates = _out_route(
            hp, ssm_p.reshape(tp, sw), att_p, _pad_rows(hs, OUT_ROWS), _pad_rows(ssm_s, OUT_ROWS),
            _pad_rows(att_s, OUT_ROWS), w_out_bf, ln1_g[l].reshape(1, d), ln1_b[l].reshape(1, d), wr_hi, wr_lo,
            b_router[l].reshape(n_exp, 1), alpha=alpha)
        n_tok = tp + bs
        tok_buf, slot_buf, gate_buf, block_expert, block_valid = _route_plan(e_idx, gates, n_tok, n_exp, MOE_ROWS)
        y_slots = _moe(h_all, block_expert, block_valid, tok_buf, slot_buf, gate_buf, w_gate_up[l], b_gate_up[l],
                       w_down[l], b_down[l], n_slots=TOP_K * n_tok)
        hp, hs = _final(y_slots, h_all, ln2_g[l].reshape(1, d), ln2_b[l].reshape(1, d), tp=tp, ts=bs, alpha=alpha)

        for lst, val in zip(outs, (
                k_p.reshape(bp, lp, heads, hd), v_p.reshape(bp, lp, heads, hd),
                sre_p.reshape(bp, groups, n_state), sim_p.reshape(bp, groups, n_state),
                k_s.reshape(bs, ls, heads, hd), v_s.reshape(bs, ls, heads, hd),
                sre_s.reshape(bs, groups, n_state), sim_s.reshape(bs, groups, n_state))):
            lst.append(val)

    return (hp.reshape(bp, lp, d), hs.reshape(bs, ls, d)) + tuple(jnp.stack(o) for o in outs)
```

```python
import functools
import math

import jax
import jax.numpy as jnp
from jax import lax
from jax.experimental import pallas as pl
from jax.experimental.pallas import tpu as pltpu

F32 = jnp.float32
BF16 = jnp.bfloat16
I32 = jnp.int32

TOP_K = 4
ROPE_THETA = 10000.0
SWIGLU_ALPHA = 1.702
SWIGLU_LIMIT = 7.0
LN_EPS = 1e-5
RMS_EPS = 1e-5
SSM_A_RE_MAX = -1e-4
GELU_C0 = math.sqrt(2.0 / math.pi)
GELU_C1 = 0.044715
NEG_BIG = -1e30
LOG2E = math.log2(math.e)

LANES = 128
SUBLANES = 8
MXU_DIM = 256
VMEM_LIMIT_BYTES = 56 * 1024 * 1024

PROJ_ROWS = 512
SSM_STEPS = 64
ATT_ROWS = 256
OUT_ROWS = 512
MOE_ROWS = 256
FINAL_ROWS = 128
SCAN_LANES = 512


def _params(*sem):
    return pltpu.CompilerParams(dimension_semantics=sem, vmem_limit_bytes=VMEM_LIMIT_BYTES)


def _dot(a, b):
    return jnp.dot(a, b, preferred_element_type=F32)


def _dot_nt(a, b):
    return lax.dot_general(a, b, (((1,), (1,)), ((), ())), preferred_element_type=F32)


def _proj_kernel(x_ref, w_ref, cos_ref, sin_ref, u_ref, q_ref, k_ref, v_ref, *, nb, tt, sw, aw, half):
    d = x_ref.shape[-1]
    heads = aw // LANES
    acc = _dot(x_ref[...].reshape(nb * tt, d).astype(BF16), w_ref[...])
    c = cos_ref[...]
    s = sin_ref[...]
    lane = lax.broadcasted_iota(I32, c.shape, 1)
    first = (lane % (2 * half)) < half

    def rope(blk):
        swapped = jnp.where(first, pltpu.roll(blk, LANES - half, 1), pltpu.roll(blk, half, 1))
        return blk * c + swapped * s

    for b in range(nb):
        rows = slice(b * tt, (b + 1) * tt)
        for j in range(sw // LANES):
            u_ref[j, pl.ds(b, tt, stride=nb), :] = acc[rows, j * LANES:(j + 1) * LANES]
        q_ref[b] = jnp.concatenate(
            [rope(acc[rows, sw + j * LANES:sw + (j + 1) * LANES]) for j in range(heads)], axis=1)
        for j in range(heads):
            lo = sw + aw + j * LANES
            k_ref[b, pl.ds(j, tt, stride=heads), :] = rope(acc[rows, lo:lo + LANES])
            lo = sw + 2 * aw + j * LANES
            v_ref[b, pl.ds(j, tt, stride=heads), :] = acc[rows, lo:lo + LANES]


def _project(x3, w_bf, cos_t, sin_t, *, tt, sw, aw, half):
    nb, seq, d = x3.shape
    heads = aw // LANES
    kern = functools.partial(_proj_kernel, nb=nb, tt=tt, sw=sw, aw=aw, half=half)
    return pl.pallas_call(
        kern,
        grid=(seq // tt,),
        in_specs=[
            pl.BlockSpec((nb, tt, d), lambda i: (0, i, 0)),
            pl.BlockSpec(w_bf.shape, lambda i: (0, 0)),
            pl.BlockSpec((tt, LANES), lambda i: (i, 0)),
            pl.BlockSpec((tt, LANES), lambda i: (i, 0)),
        ],
        out_specs=[
            pl.BlockSpec((sw // LANES, tt * nb, LANES), lambda i: (0, i, 0)),
            pl.BlockSpec((nb, tt, aw), lambda i: (0, i, 0)),
            pl.BlockSpec((nb, tt * heads, LANES), lambda i: (0, i, 0)),
            pl.BlockSpec((nb, tt * heads, LANES), lambda i: (0, i, 0)),
        ],
        out_shape=[
            jax.ShapeDtypeStruct((sw // LANES, seq * nb, LANES), F32),
            jax.ShapeDtypeStruct((nb, seq, aw), F32),
            jax.ShapeDtypeStruct((nb, seq * heads, LANES), F32),
            jax.ShapeDtypeStruct((nb, seq * heads, LANES), F32),
        ],
        compiler_params=_params("arbitrary"),
        name="proj_rope",
    )(x3, w_bf, cos_t, sin_t)


def _rope_tables(pos, half):
    inv = ROPE_THETA ** (-jnp.arange(half, dtype=F32) / half)
    ang = pos.astype(F32)[:, None] * inv[None, :]
    reps = LANES // half
    cos = jnp.tile(jnp.cos(ang), (1, reps))
    sin = jnp.sin(ang)
    sin = jnp.tile(jnp.concatenate([-sin, sin], axis=1), (1, reps // 2))
    return cos, sin


def _ssm_kernel(u_ref, s0r_ref, s0i_ref, ar_ref, ai_ref, wb_ref, wc_ref, d_ref, wg_ref, bg_ref,
                o_ref, sr_ref, si_ref, bu_ref, st_ref, ot_ref, *, nb, steps, gp, sw):
    i = pl.program_id(0)
    rows = nb * steps

    @pl.when(i == 0)
    def _():
        st_ref[:, :gp] = s0r_ref[...]
        st_ref[:, gp:] = s0i_ref[...]

    u = jnp.concatenate([u_ref[j] for j in range(sw // LANES)], axis=1)
    ub = u.astype(BF16)
    n_tiles = 2 * gp // MXU_DIM
    per_half = gp // MXU_DIM
    in_tiles = sw // MXU_DIM
    for j in range(n_tiles):
        kt = ((j % per_half) * in_tiles) // per_half
        bu_ref[:, j * MXU_DIM:(j + 1) * MXU_DIM] = _dot(ub[:, kt * MXU_DIM:(kt + 1) * MXU_DIM], wb_ref[j])

    for c in range(gp // SCAN_LANES):
        lo_r = c * SCAN_LANES
        lo_i = gp + c * SCAN_LANES
        ar = jnp.broadcast_to(ar_ref[:, lo_r:lo_r + SCAN_LANES], (nb, SCAN_LANES))
        ai = jnp.broadcast_to(ai_ref[:, lo_r:lo_r + SCAN_LANES], (nb, SCAN_LANES))

        def step(t, carry, lo_r=lo_r, lo_i=lo_i, ar=ar, ai=ai):
            sr, si = carry
            r0 = pl.multiple_of(t * nb, nb)
            br = bu_ref[pl.ds(r0, nb), lo_r:lo_r + SCAN_LANES]
            bi = bu_ref[pl.ds(r0, nb), lo_i:lo_i + SCAN_LANES]
            nr = ar * sr - ai * si + br
            ni = ar * si + ai * sr + bi
            bu_ref[pl.ds(r0, nb), lo_r:lo_r + SCAN_LANES] = nr
            bu_ref[pl.ds(r0, nb), lo_i:lo_i + SCAN_LANES] = ni
            return nr, ni

        sr0 = st_ref[:, lo_r:lo_r + SCAN_LANES]
        si0 = st_ref[:, lo_i:lo_i + SCAN_LANES]
        if steps == 1:
            sr1, si1 = step(0, (sr0, si0))
        else:
            sr1, si1 = lax.fori_loop(0, steps, step, (sr0, si0))
        st_ref[:, lo_r:lo_r + SCAN_LANES] = sr1
        st_ref[:, lo_i:lo_i + SCAN_LANES] = si1

    sr_ref[...] = st_ref[:, :gp]
    si_ref[...] = st_ref[:, gp:]

    out_tiles = sw // MXU_DIM
    kw = gp // out_tiles
    ys = []
    for nt in range(out_tiles):
        s_re = bu_ref[:, nt * kw:(nt + 1) * kw].astype(BF16)
        s_im = bu_ref[:, gp + nt * kw:gp + (nt + 1) * kw].astype(BF16)
        ys.append(_dot(s_re, wc_ref[nt, 0]) + _dot(s_im, wc_ref[nt, 1]))
    y = jnp.concatenate(ys, axis=1) + d_ref[...] * u
    y = 0.5 * y * (1.0 + jnp.tanh(GELU_C0 * (y + GELU_C1 * (y * y * y))))
    z = _dot(y.astype(BF16), wg_ref[...]) + bg_ref[...]
    out = z[:, :sw] * jax.nn.sigmoid(z[:, sw:])
    if steps == 1:
        o_ref[...] = out
    else:
        for j in range(sw // LANES):
            ot_ref[j] = out[:, j * LANES:(j + 1) * LANES]
        for b in range(nb):
            for j in range(sw // LANES):
                o_ref[b, :, j * LANES:(j + 1) * LANES] = ot_ref[j, pl.ds(b, steps, stride=nb), :]


def _ssm(u_tb, s0r, s0i, prm, *, nb, steps, w_glu_bf, b_glu):
    n_slabs, n_rows, _ = u_tb.shape
    sw = n_slabs * LANES
    seq = n_rows // nb
    gp = prm["a_re"].shape[1]
    rows = nb * steps
    grid = (seq // steps,)
    const2 = lambda i: (0, 0)
    kern = functools.partial(_ssm_kernel, nb=nb, steps=steps, gp=gp, sw=sw)
    if steps == 1:
        o_spec = pl.BlockSpec((nb, sw), const2)
        o_shape = jax.ShapeDtypeStruct((nb, sw), F32)
    else:
        o_spec = pl.BlockSpec((nb, steps, sw), lambda i: (0, i, 0))
        o_shape = jax.ShapeDtypeStruct((nb, seq, sw), F32)
    return pl.pallas_call(
        kern,
        grid=grid,
        in_specs=[
            pl.BlockSpec((n_slabs, rows, LANES), lambda i: (0, i, 0)),
            pl.BlockSpec((nb, gp), const2),
            pl.BlockSpec((nb, gp), const2),
            pl.BlockSpec((1, gp), const2),
            pl.BlockSpec((1, gp), const2),
            pl.BlockSpec(prm["wb"].shape, lambda i: (0, 0, 0)),
            pl.BlockSpec(prm["wc"].shape, lambda i: (0, 0, 0, 0)),
            pl.BlockSpec((1, sw), const2),
            pl.BlockSpec(w_glu_bf.shape, const2),
            pl.BlockSpec((1, 2 * sw), const2),
        ],
        out_specs=[o_spec, pl.BlockSpec((nb, gp), const2), pl.BlockSpec((nb, gp), const2)],
        out_shape=[o_shape, jax.ShapeDtypeStruct((nb, gp), F32), jax.ShapeDtypeStruct((nb, gp), F32)],
        scratch_shapes=[
            pltpu.VMEM((rows, 2 * gp), F32),
            pltpu.VMEM((nb, 2 * gp), F32),
            pltpu.VMEM((sw // LANES, rows, LANES), F32),
        ],
        compiler_params=_params("arbitrary"),
        name="ssm_mixer",
    )(u_tb, s0r, s0i, prm["a_re"], prm["a_im"], prm["wb"], prm["wc"], prm["d"], w_glu_bf, b_glu)


def _ssm_params(a_re, a_im, log_dt, b_re, b_im, c_re, c_im, d):
    g_, p_ = a_re.shape
    h_ = b_re.shape[-1]
    gp = g_ * p_
    sw = g_ * h_
    a_re = jnp.minimum(a_re.astype(F32), SSM_A_RE_MAX)
    a_im = a_im.astype(F32)
    dt = jnp.exp(log_dt.astype(F32))[:, None]
    mag = jnp.exp(a_re * dt)
    lb_re = mag * jnp.cos(a_im * dt)
    lb_im = mag * jnp.sin(a_im * dt)
    den = jnp.square(a_re) + jnp.square(a_im)
    nr, ni = lb_re - 1.0, lb_im
    f_re = (nr * a_re + ni * a_im) / den
    f_im = (ni * a_re - nr * a_im) / den
    b_re = b_re.astype(F32)
    b_im = b_im.astype(F32)
    bb_re = f_re[..., None] * b_re - f_im[..., None] * b_im
    bb_im = f_re[..., None] * b_im + f_im[..., None] * b_re
    eye = jnp.eye(g_, dtype=F32)
    w_in_re = jnp.einsum("gph,gk->ghkp", bb_re, eye).reshape(sw, gp)
    w_in_im = jnp.einsum("gph,gk->ghkp", bb_im, eye).reshape(sw, gp)
    w_in = jnp.concatenate([w_in_re, w_in_im], axis=1)
    per_half = gp // MXU_DIM
    in_tiles = sw // MXU_DIM
    tiles = []
    for j in range(2 * per_half):
        kt = ((j % per_half) * in_tiles) // per_half
        tiles.append(w_in[kt * MXU_DIM:(kt + 1) * MXU_DIM, j * MXU_DIM:(j + 1) * MXU_DIM])
    wb = jnp.stack(tiles).astype(BF16)
    w_out_re = jnp.einsum("ghp,gk->gpkh", c_re.astype(F32), eye).reshape(gp, sw)
    w_out_im = -jnp.einsum("ghp,gk->gpkh", c_im.astype(F32), eye).reshape(gp, sw)
    out_tiles = sw // MXU_DIM
    kw = gp // out_tiles
    wc = jnp.stack([
        jnp.stack([w_out_re[nt * kw:(nt + 1) * kw, nt * MXU_DIM:(nt + 1) * MXU_DIM],
                   w_out_im[nt * kw:(nt + 1) * kw, nt * MXU_DIM:(nt + 1) * MXU_DIM]])
        for nt in range(out_tiles)]).astype(BF16)
    return dict(a_re=lb_re.reshape(1, gp), a_im=lb_im.reshape(1, gp), wb=wb, wc=wc,
                d=d.astype(F32).reshape(1, sw))


def _diff_lambda(lq1, lk1, lq2, lk2, lam_init):
    l1 = jnp.exp(jnp.sum(lq1 * lk1, axis=1, keepdims=True))
    l2 = jnp.exp(jnp.sum(lq2 * lk2, axis=1, keepdims=True))
    return l1 - l2 + lam_init


def _head_norm(o, g, lam_init):
    ms = jnp.mean(o * o, axis=-1, keepdims=True)
    return o * lax.rsqrt(ms + RMS_EPS) * g * (1.0 - lam_init)


def _attn_prompt_kernel(q_ref, k_ref, v_ref, lq1_ref, lk1_ref, lq2_ref, lk2_ref, g_ref, o_ref,
                        kb_ref, vt_ref, a1_ref, a2_ref, *, tq, qk, heads, lam_init, scale):
    h = pl.program_id(1)
    qi = pl.program_id(2)

    @pl.when(qi == 0)
    def _():
        kb_ref[...] = k_ref[pl.ds(h, kb_ref.shape[0], stride=heads), :].astype(BF16)
        for jb in range(vt_ref.shape[0]):
            vt_ref[jb] = v_ref[pl.ds(jb * tq * heads + h, tq, stride=heads), :].T.astype(BF16)

    q = q_ref[...] * (scale * LOG2E)
    lane = lax.broadcasted_iota(I32, q.shape, 1)
    q1 = jnp.where(lane < qk, q, 0.0).astype(BF16)
    q2 = jnp.where(lane >= qk, q, 0.0).astype(BF16)
    a1_ref[...] = jnp.zeros(a1_ref.shape, F32)
    a2_ref[...] = jnp.zeros(a2_ref.shape, F32)

    def update(st, m, l, acc_ref, vtj):
        m_new = jnp.maximum(m, jnp.max(st, axis=0, keepdims=True))
        alpha = jnp.exp2(m - m_new)
        p = jnp.exp2(st - m_new)
        acc_ref[...] = alpha * acc_ref[...] + _dot(vtj, p.astype(BF16))
        return m_new, alpha * l + jnp.sum(p, axis=0, keepdims=True)

    def block(j, carry, masked):
        m1, l1, m2, l2 = carry
        kj = kb_ref[pl.ds(pl.multiple_of(j * tq, tq), tq), :]
        vtj = vt_ref[j]
        s1 = _dot_nt(kj, q1)
        s2 = _dot_nt(kj, q2)
        if masked:
            key = lax.broadcasted_iota(I32, s1.shape, 0)
            qry = lax.broadcasted_iota(I32, s1.shape, 1)
            keep = key <= qry
            s1 = jnp.where(keep, s1, NEG_BIG)
            s2 = jnp.where(keep, s2, NEG_BIG)
        m1, l1 = update(s1, m1, l1, a1_ref, vtj)
        m2, l2 = update(s2, m2, l2, a2_ref, vtj)
        return m1, l1, m2, l2

    m0 = jnp.full((1, tq), NEG_BIG, F32)
    l0 = jnp.zeros((1, tq), F32)
    carry = lax.fori_loop(0, qi, lambda j, c: block(j, c, False), (m0, l0, m0, l0))
    _, l1, _, l2 = block(qi, carry, True)
    lam = _diff_lambda(lq1_ref[...], lk1_ref[...], lq2_ref[...], lk2_ref[...], lam_init)
    ot = a1_ref[...] * (1.0 / l1) - lam * (a2_ref[...] * (1.0 / l2))
    ms = jnp.mean(ot * ot, axis=0, keepdims=True)
    ot = ot * lax.rsqrt(ms + RMS_EPS) * g_ref[...] * (1.0 - lam_init)
    o_ref[...] = ot.T


def _attn_prompt(q, k, v, lam_p, g_col, *, heads, lam_init):
    nb, seq, aw = q.shape
    hd = aw // heads
    qk = hd // 2
    tq = min(ATT_ROWS, seq)
    nq = seq // tq
    kern = functools.partial(_attn_prompt_kernel, tq=tq, qk=qk, heads=heads, lam_init=lam_init, scale=qk ** -0.5)
    small = pl.BlockSpec((1, qk), lambda b, h, i: (0, 0))
    return pl.pallas_call(
        kern,
        grid=(nb, heads, nq),
        in_specs=[
            pl.BlockSpec((None, tq, hd), lambda b, h, i: (b, i, h)),
            pl.BlockSpec((None, seq * heads, hd), lambda b, h, i: (b, 0, 0)),
            pl.BlockSpec((None, seq * heads, hd), lambda b, h, i: (b, 0, 0)),
            small, small, small, small,
            pl.BlockSpec((hd, 1), lambda b, h, i: (0, 0)),
        ],
        out_specs=pl.BlockSpec((tq, hd), lambda b, h, i: (b * nq + i, h)),
        out_shape=jax.ShapeDtypeStruct((nb * seq, aw), F32),
        scratch_shapes=[pltpu.VMEM((seq, hd), BF16), pltpu.VMEM((nq, hd, tq), BF16),
                        pltpu.VMEM((hd, tq), F32), pltpu.VMEM((hd, tq), F32)],
        compiler_params=_params("arbitrary", "arbitrary", "arbitrary"),
        name="diff_attn_prompt",
    )(q, k, v, *lam_p, g_col)


def _attn_sample_kernel(pt_ref, q_ref, ks_ref, vs_ref, lq1_ref, lk1_ref, lq2_ref, lk2_ref, g_ref, *rest,
                        n_pages, heads, qk, lam_init, scale):
    k_pages = rest[:n_pages]
    v_pages = rest[n_pages:2 * n_pages]
    o_ref = rest[2 * n_pages]
    del pt_ref
    hd = 2 * qk
    aw = heads * hd
    n_maps = 2 * heads
    q = q_ref[...] * scale
    row = lax.broadcasted_iota(I32, (n_maps, aw), 0)
    lane = lax.broadcasted_iota(I32, (n_maps, aw), 1)
    own = (lane // hd == row % heads) & ((lane // qk) % 2 == row // heads)
    qm = jnp.where(own, jnp.broadcast_to(q, (n_maps, aw)), 0.0)
    qmb = qm.astype(BF16)
    s_pages = [_dot_nt(qmb, kp[...].astype(BF16)) for kp in k_pages]
    s_self = jnp.sum(qm * ks_ref[...], axis=1, keepdims=True)
    m = s_self
    for s in s_pages:
        m = jnp.maximum(m, jnp.max(s, axis=1, keepdims=True))
    p_self = jnp.exp(s_self - m)
    p_pages = [jnp.exp(s - m) for s in s_pages]
    l = p_self
    for p in p_pages:
        l = l + jnp.sum(p, axis=1, keepdims=True)
    inv = 1.0 / l
    lam = _diff_lambda(lq1_ref[...], lk1_ref[...], lq2_ref[...], lk2_ref[...], lam_init)

    def mix(p):
        pn = p * inv
        return pn - lam * pltpu.roll(pn, heads, 0)

    acc = jnp.broadcast_to(mix(p_self), (n_maps, aw)) * vs_ref[...]
    for p, vp in zip(p_pages, v_pages):
        acc = acc + _dot(mix(p).astype(BF16), vp[...].astype(BF16))
    pick = (row < heads) & (lane // hd == row)
    o = jnp.sum(jnp.where(pick, acc, 0.0), axis=0, keepdims=True)
    g = g_ref[...]
    for h in range(heads):
        o_ref[:, h * hd:(h + 1) * hd] = _head_norm(o[:, h * hd:(h + 1) * hd], g, lam_init)


def _attn_sample(q, k_new, v_new, cache_k, cache_v, page_rows, lam_p, g, *, heads, lam_init):
    bs, aw = q.shape
    n_pages = page_rows.shape[1]
    page = cache_k.shape[1]
    hd = aw // heads
    qk = hd // 2
    kern = functools.partial(_attn_sample_kernel, n_pages=n_pages, heads=heads, qk=qk, lam_init=lam_init,
                             scale=qk ** -0.5)
    row_spec = pl.BlockSpec((None, 1, aw), lambda b, pt: (b, 0, 0))
    small = pl.BlockSpec((1, qk), lambda b, pt: (0, 0))

    def page_spec(j):
        return pl.BlockSpec((None, page, aw), lambda b, pt, j=j: (pt[b * n_pages + j], 0, 0))

    grid_spec = pltpu.PrefetchScalarGridSpec(
        num_scalar_prefetch=1,
        grid=(bs,),
        in_specs=[row_spec, row_spec, row_spec, small, small, small, small,
                  pl.BlockSpec((1, hd), lambda b, pt: (0, 0))]
                 + [page_spec(j) for j in range(n_pages)] * 2,
        out_specs=row_spec,
    )
    out = pl.pallas_call(
        kern,
        grid_spec=grid_spec,
        out_shape=jax.ShapeDtypeStruct((bs, 1, aw), F32),
        compiler_params=_params("arbitrary"),
        name="diff_attn_sample",
    )(page_rows.reshape(-1).astype(I32), q.reshape(bs, 1, aw), k_new.reshape(bs, 1, aw),
      v_new.reshape(bs, 1, aw), *lam_p, g, *([cache_k] * n_pages), *([cache_v] * n_pages))
    return out.reshape(bs, aw)


def _layer_norm(z, g, b):
    mu = jnp.mean(z, axis=-1, keepdims=True)
    zc = z - mu
    var = jnp.mean(zc * zc, axis=-1, keepdims=True)
    return zc * lax.rsqrt(var + LN_EPS) * g + b


def _out_kernel(xp_ref, sp_ref, ap_ref, xs_ref, ss_ref, as_ref, wo_ref, g_ref, b_ref, wrh_ref, wrl_ref,
                br_ref, h_ref, ht_ref, e_ref, gt_ref, *, n_prompt, sw, alpha, top_k):
    i = pl.program_id(0)
    is_s = i == n_prompt
    x = jnp.where(is_s, xs_ref[...], xp_ref[...])
    s = jnp.where(is_s, ss_ref[...], sp_ref[...]).astype(BF16)
    a = jnp.where(is_s, as_ref[...], ap_ref[...]).astype(BF16)
    mix = _dot(s, wo_ref[:sw, :]) + _dot(a, wo_ref[sw:, :])
    h = _layer_norm(alpha * x + mix, g_ref[...], b_ref[...])
    h_ref[...] = h
    rows = h.shape[0]
    for j in range(h.shape[1] // LANES):
        ht_ref[pl.ds(j, rows, stride=SUBLANES), :] = h[:, j * LANES:(j + 1) * LANES]
    h_hi = h.astype(BF16)
    h_lo = (h - h_hi.astype(F32)).astype(BF16)
    lg = (_dot_nt(wrh_ref[...], h_hi) + _dot_nt(wrl_ref[...], h_hi) + _dot_nt(wrh_ref[...], h_lo)
          + br_ref[...])
    eio = lax.broadcasted_iota(I32, lg.shape, 0)
    n_exp = lg.shape[0]
    vals, idxs = [], []
    for _ in range(top_k):
        m = jnp.max(lg, axis=0, keepdims=True)
        idx = jnp.min(jnp.where(lg == m, eio, n_exp), axis=0, keepdims=True)
        vals.append(m)
        idxs.append(idx)
        lg = jnp.where(eio == idx, NEG_BIG, lg)
    tv = jnp.concatenate(vals, axis=0)
    pe = jnp.exp(tv - tv[0:1, :])
    gt_ref[...] = pe / jnp.sum(pe, axis=0, keepdims=True)
    e_ref[...] = jnp.concatenate(idxs, axis=0)


def _out_route(xp, sp, ap, xs, ss, as_, w_out_bf, g, b, wr_hi, wr_lo, br, *, alpha):
    tp, d = xp.shape
    sw = sp.shape[1]
    rows = OUT_ROWS
    n_prompt = tp // rows
    total = tp + rows
    n_exp = wr_hi.shape[0]
    pidx = lambda i: (jnp.minimum(i, n_prompt - 1), 0)
    const = lambda i: (0, 0)
    kern = functools.partial(_out_kernel, n_prompt=n_prompt, sw=sw, alpha=alpha, top_k=TOP_K)
    return pl.pallas_call(
        kern,
        grid=(n_prompt + 1,),
        in_specs=[
            pl.BlockSpec((rows, d), pidx), pl.BlockSpec((rows, sw), pidx), pl.BlockSpec((rows, d - sw), pidx),
            pl.BlockSpec((rows, d), const), pl.BlockSpec((rows, sw), const), pl.BlockSpec((rows, d - sw), const),
            pl.BlockSpec(w_out_bf.shape, const), pl.BlockSpec((1, d), const), pl.BlockSpec((1, d), const),
            pl.BlockSpec((n_exp, d), const), pl.BlockSpec((n_exp, d), const), pl.BlockSpec((n_exp, 1), const),
        ],
        out_specs=[
            pl.BlockSpec((rows, d), lambda i: (i, 0)),
            pl.BlockSpec((rows * SUBLANES, LANES), lambda i: (i, 0)),
            pl.BlockSpec((TOP_K, rows), lambda i: (0, i)),
            pl.BlockSpec((TOP_K, rows), lambda i: (0, i)),
        ],
        out_shape=[
            jax.ShapeDtypeStruct((total, d), F32),
            jax.ShapeDtypeStruct((total * SUBLANES, LANES), F32),
            jax.ShapeDtypeStruct((TOP_K, total), I32),
            jax.ShapeDtypeStruct((TOP_K, total), F32),
        ],
        compiler_params=_params("arbitrary"),
        name="out_proj_route",
    )(xp, sp, ap, xs, ss, as_, w_out_bf, g, b, wr_hi, wr_lo, br)


def _moe_kernel(be_ref, tokc_ref, tokn_ref, slot_ref, gate_ref, h_hbm, wgu_ref, bgu_ref, wdn_ref, bdn_ref,
                y_hbm, xbuf, ybuf, wgu_bf, wdn_bf, gsem, ssem, *, rows, de, n_real):
    i = pl.program_id(0)
    last = pl.num_programs(0) - 1
    cur = i % 2
    nxt = 1 - cur
    tile = SUBLANES

    def buf_rows(buf, r):
        return pl.ds((buf * rows + r) * tile, tile)

    def gather(tok_ref, r, buf):
        return pltpu.make_async_copy(h_hbm.at[tok_ref[0, r]], xbuf.at[buf_rows(buf, r), :], gsem.at[buf])

    def scatter(dst_row, r, buf):
        return pltpu.make_async_copy(ybuf.at[buf_rows(buf, r), :], y_hbm.at[dst_row], ssem.at[buf])

    @pl.when(i == 0)
    def _():
        for r in range(rows):
            gather(tokc_ref, r, 0).start()
        ybuf[pl.ds(rows * tile, rows * tile), :] = jnp.zeros((rows * tile, LANES), F32)
        for r in range(rows):
            scatter(n_real + r, r, 1).start()

    @pl.when((i == 0) | (be_ref[i] != be_ref[jnp.maximum(i - 1, 0)]))
    def _():
        wgu_bf[...] = wgu_ref[...].astype(BF16)
        wdn_bf[...] = wdn_ref[...].astype(BF16)

    for r in range(rows):
        gather(tokn_ref, r, nxt).start()
    for r in range(rows):
        gather(tokc_ref, r, cur).wait()
    x = jnp.concatenate(
        [xbuf[pl.ds(cur * rows * tile + j, rows, stride=tile), :] for j in range(tile)], axis=1).astype(BF16)
    hh = _dot(x, wgu_bf[...]) + bgu_ref[...]
    glu = jnp.minimum(hh[:, :de], SWIGLU_LIMIT)
    lin = jnp.clip(hh[:, de:], -SWIGLU_LIMIT, SWIGLU_LIMIT)
    act = glu * jax.nn.sigmoid(SWIGLU_ALPHA * glu) * (lin + 1.0)
    y = (_dot(act.astype(BF16), wdn_bf[...]) + bdn_ref[...]) * gate_ref[...]
    for j in range(tile):
        ybuf[pl.ds(cur * rows * tile + j, rows, stride=tile), :] = y[:, j * LANES:(j + 1) * LANES]
    for r in range(rows):
        scatter(0, r, nxt).wait()
    for r in range(rows):
        scatter(slot_ref[0, r], r, cur).start()

    @pl.when(i == last)
    def _():
        for r in range(rows):
            gather(tokn_ref, r, nxt).wait()
        for r in range(rows):
            scatter(0, r, cur).wait()


def _moe(h_tiles, block_expert, tok_buf, slot_buf, gate_buf, w_gu, b_gu, w_dn, b_dn, *, n_real):
    n_blocks = block_expert.shape[0]
    rows = MOE_ROWS
    n_tok, tile, lanes = h_tiles.shape
    d = tile * lanes
    n_exp, _, de2 = w_gu.shape
    de = de2 // 2
    kern = functools.partial(_moe_kernel, rows=rows, de=de, n_real=n_real)
    smem_cur = pl.BlockSpec((None, 1, rows), lambda i, be: (i, 0, 0), memory_space=pltpu.SMEM)
    smem_nxt = pl.BlockSpec((None, 1, rows), lambda i, be: (jnp.minimum(i + 1, n_blocks - 1), 0, 0),
                            memory_space=pltpu.SMEM)
    grid_spec = pltpu.PrefetchScalarGridSpec(
        num_scalar_prefetch=1,
        grid=(n_blocks,),
        in_specs=[
            smem_cur, smem_nxt, smem_cur,
            pl.BlockSpec((rows, 1), lambda i, be: (i, 0)),
            pl.BlockSpec(memory_space=pl.ANY),
            pl.BlockSpec((None, d, de2), lambda i, be: (be[i], 0, 0)),
            pl.BlockSpec((None, 1, de2), lambda i, be: (be[i], 0, 0)),
            pl.BlockSpec((None, de, d), lambda i, be: (be[i], 0, 0)),
            pl.BlockSpec((None, 1, d), lambda i, be: (be[i], 0, 0)),
        ],
        out_specs=pl.BlockSpec(memory_space=pl.ANY),
        scratch_shapes=[
            pltpu.VMEM((2 * rows * tile, lanes), F32), pltpu.VMEM((2 * rows * tile, lanes), F32),
            pltpu.VMEM((d, de2), BF16), pltpu.VMEM((de, d), BF16),
            pltpu.SemaphoreType.DMA((2,)), pltpu.SemaphoreType.DMA((2,)),
        ],
    )
    tok3 = tok_buf.reshape(n_blocks, 1, rows)
    return pl.pallas_call(
        kern,
        grid_spec=grid_spec,
        out_shape=jax.ShapeDtypeStruct((n_real + rows, tile, lanes), F32),
        compiler_params=_params("arbitrary"),
        name="moe_experts",
    )(block_expert, tok3, tok3, slot_buf.reshape(n_blocks, 1, rows), gate_buf.reshape(n_blocks * rows, 1),
      h_tiles, w_gu, b_gu.reshape(n_exp, 1, de2), w_dn, b_dn.reshape(n_exp, 1, d))


def _route_plan(e_idx, gates, n_tokens, n_exp, rows):
    k_ = e_idx.shape[0]
    n_assign = k_ * n_tokens
    chunk = LANES
    n_chunks = n_assign // chunk
    e_flat = e_idx[:, :n_tokens].reshape(n_assign)
    g_flat = gates[:, :n_tokens].reshape(n_assign)
    onehot = (e_flat.reshape(n_chunks, chunk, 1) == jnp.arange(n_exp, dtype=I32)).astype(F32)
    before = jnp.tril(jnp.ones((chunk, chunk), F32), -1)
    rank_in = jnp.einsum("ji,cie->cje", before, onehot)
    per_chunk = jnp.sum(onehot, axis=1)
    chunk_base = jnp.cumsum(per_chunk, axis=0) - per_chunk
    rank = jnp.sum((rank_in + chunk_base[:, None, :]) * onehot, axis=2).reshape(n_assign).astype(I32)
    counts = jnp.sum(per_chunk, axis=0).astype(I32)
    padded = ((counts + rows - 1) // rows) * rows
    pend = jnp.cumsum(padded)
    pstart = pend - padded
    dest = pstart[e_flat] + rank
    n_blocks = -(-(n_assign + n_exp * (rows - 1)) // rows)
    n_rows = n_blocks * rows
    a_id = jnp.arange(n_assign, dtype=I32)
    sorted_a = jnp.full((n_rows,), -1, I32).at[dest].set(a_id)
    real = sorted_a >= 0
    sorted_a = jnp.maximum(sorted_a, 0)
    pos = jnp.arange(n_rows, dtype=I32)
    tok_buf = jnp.where(real, sorted_a % n_tokens, 0)
    slot_buf = jnp.where(real, sorted_a, n_assign + pos % rows)
    gate_buf = jnp.where(real, g_flat[sorted_a], 0.0)
    block_start = jnp.arange(n_blocks, dtype=I32) * rows
    block_expert = jnp.minimum(jnp.searchsorted(pend, block_start, side="right"), n_exp - 1).astype(I32)
    return tok_buf, slot_buf, gate_buf, block_expert


def _final_kernel(*refs, top_k, n_prompt, alpha):
    y_refs = refs[:top_k]
    h_ref, g_ref, b_ref, op_ref, os_ref, f_ref = refs[top_k:]
    i = pl.program_id(0)
    f = y_refs[0][...]
    for r in y_refs[1:]:
        f = f + r[...]
    f_ref[...] = f
    rows = h_ref.shape[0]
    f = jnp.concatenate([f_ref[pl.ds(j, rows, stride=SUBLANES), :] for j in range(SUBLANES)], axis=1)
    out = _layer_norm(alpha * h_ref[...] + f, g_ref[...], b_ref[...])

    @pl.when(i < n_prompt)
    def _():
        op_ref[...] = out

    @pl.when(i == n_prompt)
    def _():
        os_ref[...] = out


def _final(y_tiles, h_all, g, b, *, tp, ts, alpha):
    rows = FINAL_ROWS
    assert ts == rows and tp % rows == 0
    d = h_all.shape[1]
    n_prompt = tp // rows
    n_tok_blocks = n_prompt + 1
    const = lambda i: (0, 0)
    kern = functools.partial(_final_kernel, top_k=TOP_K, n_prompt=n_prompt, alpha=alpha)
    y_specs = [pl.BlockSpec((rows * SUBLANES, LANES), lambda i, k=k: (k * n_tok_blocks + i, 0))
               for k in range(TOP_K)]
    return pl.pallas_call(
        kern,
        grid=(n_tok_blocks,),
        in_specs=y_specs + [pl.BlockSpec((rows, d), lambda i: (i, 0)), pl.BlockSpec((1, d), const),
                            pl.BlockSpec((1, d), const)],
        out_specs=[pl.BlockSpec((rows, d), lambda i: (jnp.minimum(i, n_prompt - 1), 0)),
                   pl.BlockSpec((rows, d), const)],
        out_shape=[jax.ShapeDtypeStruct((tp, d), F32), jax.ShapeDtypeStruct((ts, d), F32)],
        scratch_shapes=[pltpu.VMEM((rows * SUBLANES, LANES), F32)],
        compiler_params=_params("arbitrary"),
        name="combine_ln",
    )(*([y_tiles] * TOP_K), h_all, g, b)


def _pad_rows(a, rows):
    return jnp.pad(a, ((0, rows - a.shape[0]), (0, 0)))


def kernel(x_prompt, x_sample, cache_k, cache_v, state_ssm_re, state_ssm_im, page_table, w_in, w_out, ssm_a_re, ssm_a_im, ssm_log_dt, ssm_b_re, ssm_b_im, ssm_c_re, ssm_c_im, ssm_d, w_glu, b_glu, lambda_q1, lambda_k1, lambda_q2, lambda_k2, subln_g, ln1_g, ln1_b, w_router, b_router, w_gate_up, b_gate_up, w_down, b_down, ln2_g, ln2_b):
    depth = w_in.shape[0]
    bp, lp, d = x_prompt.shape
    bs, ls, _ = x_sample.shape
    n_phys, page, heads, hd = cache_k.shape[1:]
    assert ls == 1 and hd == LANES and d == SUBLANES * LANES
    qk = hd // 2
    aw = heads * hd
    groups, n_state = ssm_a_re.shape[1:]
    sw = groups * ssm_b_re.shape[-1]
    gp = groups * n_state
    n_exp = w_router.shape[2]
    past_len = page_table.shape[1] * page
    alpha = (2 * depth) ** 0.25
    tp = bp * lp
    n_tok = tp + bs

    cos_p, sin_p = _rope_tables(jnp.arange(lp, dtype=I32), qk // 2)
    cos_s, sin_s = _rope_tables(jnp.full((bs,), past_len, I32), qk // 2)
    ck = cache_k.reshape(depth * n_phys, page, aw)
    cv = cache_v.reshape(depth * n_phys, page, aw)

    hp = x_prompt
    hs = x_sample.reshape(1, bs, d)
    outs = [[] for _ in range(8)]
    for l in range(depth):
        lam_init = 0.8 - 0.6 * math.exp(-0.3 * l)
        w_in_bf = w_in[l].astype(BF16)
        w_out_bf = w_out[l].astype(BF16)
        w_glu_bf = w_glu[l].astype(BF16)
        bg = b_glu[l].reshape(1, 2 * sw)
        lam_p = [p[l].reshape(1, qk).astype(F32) for p in (lambda_q1, lambda_k1, lambda_q2, lambda_k2)]
        prm = _ssm_params(ssm_a_re[l], ssm_a_im[l], ssm_log_dt[l], ssm_b_re[l], ssm_b_im[l],
                          ssm_c_re[l], ssm_c_im[l], ssm_d[l])

        tt = min(PROJ_ROWS // bp, lp)
        u_p, q_p, k_p, v_p = _project(hp, w_in_bf, cos_p, sin_p, tt=tt, sw=sw, aw=aw, half=qk // 2)
        zero = jnp.zeros((bp, gp), F32)
        ssm_p, sre_p, sim_p = _ssm(u_p, zero, zero, prm, nb=bp, steps=min(SSM_STEPS, lp), w_glu_bf=w_glu_bf,
                                   b_glu=bg)
        att_p = _attn_prompt(q_p, k_p, v_p, lam_p, subln_g[l].reshape(hd, 1), heads=heads, lam_init=lam_init)

        u_s, q_s, k_s, v_s = _project(hs, w_in_bf, cos_s, sin_s, tt=bs, sw=sw, aw=aw, half=qk // 2)
        ssm_s, sre_s, sim_s = _ssm(u_s, state_ssm_re[l].reshape(bs, gp), state_ssm_im[l].reshape(bs, gp), prm,
                                   nb=bs, steps=1, w_glu_bf=w_glu_bf, b_glu=bg)
        att_s = _attn_sample(q_s.reshape(bs, aw), k_s.reshape(bs, aw), v_s.reshape(bs, aw), ck, cv,
                             page_table + l * n_phys, lam_p, subln_g[l].reshape(1, hd), heads=heads,
                             lam_init=lam_init)

        wr = w_router[l].T.astype(F32)
        wr_hi = wr.astype(BF16)
        wr_lo = (wr - wr_hi.astype(F32)).astype(BF16)
        h_all, h_tiles, e_idx, gates = _out_route(
            hp.reshape(tp, d), ssm_p.reshape(tp, sw), att_p, _pad_rows(hs.reshape(bs, d), OUT_ROWS),
            _pad_rows(ssm_s, OUT_ROWS), _pad_rows(att_s, OUT_ROWS), w_out_bf, ln1_g[l].reshape(1, d),
            ln1_b[l].reshape(1, d), wr_hi, wr_lo, b_router[l].reshape(n_exp, 1), alpha=alpha)
        tok_buf, slot_buf, gate_buf, block_expert = _route_plan(e_idx, gates, n_tok, n_exp, MOE_ROWS)
        y_tiles = _moe(h_tiles.reshape(-1, SUBLANES, LANES), block_expert, tok_buf, slot_buf, gate_buf,
                       w_gate_up[l], b_gate_up[l], w_down[l], b_down[l], n_real=TOP_K * n_tok)
        y_p, y_s = _final(y_tiles.reshape(-1, LANES), h_all, ln2_g[l].reshape(1, d), ln2_b[l].reshape(1, d),
                          tp=tp, ts=bs, alpha=alpha)
        hp = y_p.reshape(bp, lp, d)
        hs = y_s.reshape(1, bs, d)

        for lst, val in zip(outs, (
                k_p.reshape(bp, lp, heads, hd), v_p.reshape(bp, lp, heads, hd),
                sre_p.reshape(bp, groups, n_state), sim_p.reshape(bp, groups, n_state),
                k_s.reshape(bs, ls, heads, hd), v_s.reshape(bs, ls, heads, hd),
                sre_s.reshape(bs, groups, n_state), sim_s.reshape(bs, groups, n_state))):
            lst.append(val)

    return (hp, hs.reshape(bs, ls, d)) + tuple(jnp.stack(o) for o in outs)
```

```python
import functools
import math

import jax
import jax.numpy as jnp
from jax import lax
from jax.experimental import pallas as pl
from jax.experimental.pallas import tpu as pltpu

F32 = jnp.float32
BF16 = jnp.bfloat16
I32 = jnp.int32

TOP_K = 4
ROPE_THETA = 10000.0
SWIGLU_ALPHA = 1.702
SWIGLU_LIMIT = 7.0
LN_EPS = 1e-5
RMS_EPS = 1e-5
SSM_A_RE_MAX = -1e-4
GELU_C0 = math.sqrt(2.0 / math.pi)
GELU_C1 = 0.044715
NEG_BIG = -1e30
LOG2E = math.log2(math.e)

LANES = 128
SUBLANES = 8
MXU_DIM = 256
VMEM_LIMIT_BYTES = 56 * 1024 * 1024

PROJ_ROWS = 512
SSM_STEPS = 64
ATT_ROWS = 256
ATT_LOOKAHEAD = 4
MOE_HIDDEN_CHUNK = 256
MOE_LOOKAHEAD = 2
MOE_ISSUE_POINTS = 4
OUT_ROWS = 512
MOE_ROWS = 256
FINAL_ROWS = 128
SCAN_LANES = 512


def _params(*sem):
    return pltpu.CompilerParams(dimension_semantics=sem, vmem_limit_bytes=VMEM_LIMIT_BYTES)


def _dot(a, b):
    return jnp.dot(a, b, preferred_element_type=F32)


def _dot_nt(a, b):
    return lax.dot_general(a, b, (((1,), (1,)), ((), ())), preferred_element_type=F32)


def _proj_kernel(x_ref, w_ref, cos_ref, sin_ref, u_ref, q_ref, k_ref, v_ref, *, nb, tt, sw, aw, half):
    d = x_ref.shape[-1]
    heads = aw // LANES
    acc = _dot(x_ref[...].reshape(nb * tt, d).astype(BF16), w_ref[...])
    c = cos_ref[...]
    s = sin_ref[...]
    lane = lax.broadcasted_iota(I32, c.shape, 1)
    first = (lane % (2 * half)) < half

    def rope(blk):
        swapped = jnp.where(first, pltpu.roll(blk, LANES - half, 1), pltpu.roll(blk, half, 1))
        return blk * c + swapped * s

    for b in range(nb):
        rows = slice(b * tt, (b + 1) * tt)
        for j in range(sw // LANES):
            u_ref[j, pl.ds(b, tt, stride=nb), :] = acc[rows, j * LANES:(j + 1) * LANES]
        q_ref[b] = jnp.concatenate(
            [rope(acc[rows, sw + j * LANES:sw + (j + 1) * LANES]) for j in range(heads)], axis=1)
        for j in range(heads):
            lo = sw + aw + j * LANES
            k_ref[b, pl.ds(j, tt, stride=heads), :] = rope(acc[rows, lo:lo + LANES])
            lo = sw + 2 * aw + j * LANES
            v_ref[b, pl.ds(j, tt, stride=heads), :] = acc[rows, lo:lo + LANES]


def _project(x3, w_bf, cos_t, sin_t, *, tt, sw, aw, half):
    nb, seq, d = x3.shape
    heads = aw // LANES
    kern = functools.partial(_proj_kernel, nb=nb, tt=tt, sw=sw, aw=aw, half=half)
    return pl.pallas_call(
        kern,
        grid=(seq // tt,),
        in_specs=[
            pl.BlockSpec((nb, tt, d), lambda i: (0, i, 0)),
            pl.BlockSpec(w_bf.shape, lambda i: (0, 0)),
            pl.BlockSpec((tt, LANES), lambda i: (i, 0)),
            pl.BlockSpec((tt, LANES), lambda i: (i, 0)),
        ],
        out_specs=[
            pl.BlockSpec((sw // LANES, tt * nb, LANES), lambda i: (0, i, 0)),
            pl.BlockSpec((nb, tt, aw), lambda i: (0, i, 0)),
            pl.BlockSpec((nb, tt * heads, LANES), lambda i: (0, i, 0)),
            pl.BlockSpec((nb, tt * heads, LANES), lambda i: (0, i, 0)),
        ],
        out_shape=[
            jax.ShapeDtypeStruct((sw // LANES, seq * nb, LANES), F32),
            jax.ShapeDtypeStruct((nb, seq, aw), F32),
            jax.ShapeDtypeStruct((nb, seq * heads, LANES), F32),
            jax.ShapeDtypeStruct((nb, seq * heads, LANES), F32),
        ],
        compiler_params=_params("arbitrary"),
        name="proj_rope",
    )(x3, w_bf, cos_t, sin_t)


def _rope_tables(pos, half):
    inv = ROPE_THETA ** (-jnp.arange(half, dtype=F32) / half)
    ang = pos.astype(F32)[:, None] * inv[None, :]
    reps = LANES // half
    cos = jnp.tile(jnp.cos(ang), (1, reps))
    sin = jnp.sin(ang)
    sin = jnp.tile(jnp.concatenate([-sin, sin], axis=1), (1, reps // 2))
    return cos, sin


def _ssm_kernel(u_ref, s0r_ref, s0i_ref, ar_ref, ai_ref, wb_ref, wc_ref, d_ref, wg_ref, bg_ref,
                o_ref, sr_ref, si_ref, bu_ref, st_ref, ot_ref, *, nb, steps, gp, sw):
    i = pl.program_id(0)
    rows = nb * steps

    @pl.when(i == 0)
    def _():
        st_ref[:, :gp] = s0r_ref[...]
        st_ref[:, gp:] = s0i_ref[...]

    u = jnp.concatenate([u_ref[j] for j in range(sw // LANES)], axis=1)
    ub = u.astype(BF16)
    n_tiles = 2 * gp // MXU_DIM
    per_half = gp // MXU_DIM
    in_tiles = sw // MXU_DIM
    for j in range(n_tiles):
        kt = ((j % per_half) * in_tiles) // per_half
        bu_ref[:, j * MXU_DIM:(j + 1) * MXU_DIM] = _dot(ub[:, kt * MXU_DIM:(kt + 1) * MXU_DIM], wb_ref[j])

    for c in range(gp // SCAN_LANES):
        lo_r = c * SCAN_LANES
        lo_i = gp + c * SCAN_LANES
        ar = jnp.broadcast_to(ar_ref[:, lo_r:lo_r + SCAN_LANES], (nb, SCAN_LANES))
        ai = jnp.broadcast_to(ai_ref[:, lo_r:lo_r + SCAN_LANES], (nb, SCAN_LANES))

        def step(t, carry, lo_r=lo_r, lo_i=lo_i, ar=ar, ai=ai):
            sr, si = carry
            r0 = pl.multiple_of(t * nb, nb)
            br = bu_ref[pl.ds(r0, nb), lo_r:lo_r + SCAN_LANES]
            bi = bu_ref[pl.ds(r0, nb), lo_i:lo_i + SCAN_LANES]
            nr = ar * sr - ai * si + br
            ni = ar * si + ai * sr + bi
            bu_ref[pl.ds(r0, nb), lo_r:lo_r + SCAN_LANES] = nr
            bu_ref[pl.ds(r0, nb), lo_i:lo_i + SCAN_LANES] = ni
            return nr, ni

        sr0 = st_ref[:, lo_r:lo_r + SCAN_LANES]
        si0 = st_ref[:, lo_i:lo_i + SCAN_LANES]
        if steps == 1:
            sr1, si1 = step(0, (sr0, si0))
        else:
            sr1, si1 = lax.fori_loop(0, steps, step, (sr0, si0))
        st_ref[:, lo_r:lo_r + SCAN_LANES] = sr1
        st_ref[:, lo_i:lo_i + SCAN_LANES] = si1

    sr_ref[...] = st_ref[:, :gp]
    si_ref[...] = st_ref[:, gp:]

    out_tiles = sw // MXU_DIM
    kw = gp // out_tiles
    ys = []
    for nt in range(out_tiles):
        s_re = bu_ref[:, nt * kw:(nt + 1) * kw].astype(BF16)
        s_im = bu_ref[:, gp + nt * kw:gp + (nt + 1) * kw].astype(BF16)
        ys.append(_dot(s_re, wc_ref[nt, 0]) + _dot(s_im, wc_ref[nt, 1]))
    y = jnp.concatenate(ys, axis=1) + d_ref[...] * u
    y = 0.5 * y * (1.0 + jnp.tanh(GELU_C0 * (y + GELU_C1 * (y * y * y))))
    z = _dot(y.astype(BF16), wg_ref[...]) + bg_ref[...]
    out = z[:, :sw] * jax.nn.sigmoid(z[:, sw:])
    if steps == 1:
        o_ref[...] = out
    else:
        for j in range(sw // LANES):
            ot_ref[j] = out[:, j * LANES:(j + 1) * LANES]
        for b in range(nb):
            for j in range(sw // LANES):
                o_ref[b, :, j * LANES:(j + 1) * LANES] = ot_ref[j, pl.ds(b, steps, stride=nb), :]


def _ssm(u_tb, s0r, s0i, prm, *, nb, steps, w_glu_bf, b_glu):
    n_slabs, n_rows, _ = u_tb.shape
    sw = n_slabs * LANES
    seq = n_rows // nb
    gp = prm["a_re"].shape[1]
    rows = nb * steps
    grid = (seq // steps,)
    const2 = lambda i: (0, 0)
    kern = functools.partial(_ssm_kernel, nb=nb, steps=steps, gp=gp, sw=sw)
    if steps == 1:
        o_spec = pl.BlockSpec((nb, sw), const2)
        o_shape = jax.ShapeDtypeStruct((nb, sw), F32)
    else:
        o_spec = pl.BlockSpec((nb, steps, sw), lambda i: (0, i, 0))
        o_shape = jax.ShapeDtypeStruct((nb, seq, sw), F32)
    return pl.pallas_call(
        kern,
        grid=grid,
        in_specs=[
            pl.BlockSpec((n_slabs, rows, LANES), lambda i: (0, i, 0)),
            pl.BlockSpec((nb, gp), const2),
            pl.BlockSpec((nb, gp), const2),
            pl.BlockSpec((1, gp), const2),
            pl.BlockSpec((1, gp), const2),
            pl.BlockSpec(prm["wb"].shape, lambda i: (0, 0, 0)),
            pl.BlockSpec(prm["wc"].shape, lambda i: (0, 0, 0, 0)),
            pl.BlockSpec((1, sw), const2),
            pl.BlockSpec(w_glu_bf.shape, const2),
            pl.BlockSpec((1, 2 * sw), const2),
        ],
        out_specs=[o_spec, pl.BlockSpec((nb, gp), const2), pl.BlockSpec((nb, gp), const2)],
        out_shape=[o_shape, jax.ShapeDtypeStruct((nb, gp), F32), jax.ShapeDtypeStruct((nb, gp), F32)],
        scratch_shapes=[
            pltpu.VMEM((rows, 2 * gp), F32),
            pltpu.VMEM((nb, 2 * gp), F32),
            pltpu.VMEM((sw // LANES, rows, LANES), F32),
        ],
        compiler_params=_params("arbitrary"),
        name="ssm_mixer",
    )(u_tb, s0r, s0i, prm["a_re"], prm["a_im"], prm["wb"], prm["wc"], prm["d"], w_glu_bf, b_glu)


def _ssm_params(a_re, a_im, log_dt, b_re, b_im, c_re, c_im, d):
    g_, p_ = a_re.shape
    h_ = b_re.shape[-1]
    gp = g_ * p_
    sw = g_ * h_
    a_re = jnp.minimum(a_re.astype(F32), SSM_A_RE_MAX)
    a_im = a_im.astype(F32)
    dt = jnp.exp(log_dt.astype(F32))[:, None]
    mag = jnp.exp(a_re * dt)
    lb_re = mag * jnp.cos(a_im * dt)
    lb_im = mag * jnp.sin(a_im * dt)
    den = jnp.square(a_re) + jnp.square(a_im)
    nr, ni = lb_re - 1.0, lb_im
    f_re = (nr * a_re + ni * a_im) / den
    f_im = (ni * a_re - nr * a_im) / den
    b_re = b_re.astype(F32)
    b_im = b_im.astype(F32)
    bb_re = f_re[..., None] * b_re - f_im[..., None] * b_im
    bb_im = f_re[..., None] * b_im + f_im[..., None] * b_re
    eye = jnp.eye(g_, dtype=F32)
    w_in_re = jnp.einsum("gph,gk->ghkp", bb_re, eye).reshape(sw, gp)
    w_in_im = jnp.einsum("gph,gk->ghkp", bb_im, eye).reshape(sw, gp)
    w_in = jnp.concatenate([w_in_re, w_in_im], axis=1)
    per_half = gp // MXU_DIM
    in_tiles = sw // MXU_DIM
    tiles = []
    for j in range(2 * per_half):
        kt = ((j % per_half) * in_tiles) // per_half
        tiles.append(w_in[kt * MXU_DIM:(kt + 1) * MXU_DIM, j * MXU_DIM:(j + 1) * MXU_DIM])
    wb = jnp.stack(tiles).astype(BF16)
    w_out_re = jnp.einsum("ghp,gk->gpkh", c_re.astype(F32), eye).reshape(gp, sw)
    w_out_im = -jnp.einsum("ghp,gk->gpkh", c_im.astype(F32), eye).reshape(gp, sw)
    out_tiles = sw // MXU_DIM
    kw = gp // out_tiles
    wc = jnp.stack([
        jnp.stack([w_out_re[nt * kw:(nt + 1) * kw, nt * MXU_DIM:(nt + 1) * MXU_DIM],
                   w_out_im[nt * kw:(nt + 1) * kw, nt * MXU_DIM:(nt + 1) * MXU_DIM]])
        for nt in range(out_tiles)]).astype(BF16)
    return dict(a_re=lb_re.reshape(1, gp), a_im=lb_im.reshape(1, gp), wb=wb, wc=wc,
                d=d.astype(F32).reshape(1, sw))


def _diff_lambda(lq1, lk1, lq2, lk2, lam_init):
    l1 = jnp.exp(jnp.sum(lq1 * lk1, axis=1, keepdims=True))
    l2 = jnp.exp(jnp.sum(lq2 * lk2, axis=1, keepdims=True))
    return l1 - l2 + lam_init


def _head_norm(o, g, lam_init):
    ms = jnp.mean(o * o, axis=-1, keepdims=True)
    return o * lax.rsqrt(ms + RMS_EPS) * g * (1.0 - lam_init)


def _attn_prompt_kernel(q_ref, k_ref, v_ref, lq1_ref, lk1_ref, lq2_ref, lk2_ref, g_ref, o_ref,
                        kb_ref, vt_ref, qs_ref, acc_ref, *, tq, qk, heads, lam_init, scale):
    h = pl.program_id(1)
    nq = vt_ref.shape[0]
    seq = nq * tq
    kb_ref[...] = k_ref[pl.ds(h, seq, stride=heads), :].astype(BF16)
    for jb in range(nq):
        vt_ref[jb] = v_ref[pl.ds(jb * tq * heads + h, tq, stride=heads), :].T.astype(BF16)
    q = q_ref[...] * (scale * LOG2E)
    lane = lax.broadcasted_iota(I32, q.shape, 1)
    qs_ref[0] = jnp.where(lane < qk, q, 0.0).astype(BF16)
    qs_ref[1] = jnp.where(lane >= qk, q, 0.0).astype(BF16)
    acc_ref[...] = jnp.zeros(acc_ref.shape, F32)
    key = lax.broadcasted_iota(I32, (tq, tq), 0)
    qry = lax.broadcasted_iota(I32, (tq, tq), 1)
    keep = key <= qry
    m = [[jnp.full((1, tq), NEG_BIG, F32)] * 2 for _ in range(nq)]
    l = [[jnp.zeros((1, tq), F32)] * 2 for _ in range(nq)]
    tasks = [(j, i, c) for j in range(nq) for i in range(j, nq) for c in range(2)]

    def scores(t):
        j, i, c = tasks[t]
        st = _dot_nt(kb_ref[j * tq:(j + 1) * tq, :], qs_ref[c, i * tq:(i + 1) * tq, :])
        return jnp.where(keep, st, NEG_BIG) if i == j else st

    pending = {t: scores(t) for t in range(min(ATT_LOOKAHEAD, len(tasks)))}
    for t, (j, i, c) in enumerate(tasks):
        if t + ATT_LOOKAHEAD < len(tasks):
            pending[t + ATT_LOOKAHEAD] = scores(t + ATT_LOOKAHEAD)
        st = pending.pop(t)
        m_new = jnp.maximum(m[i][c], jnp.max(st, axis=0, keepdims=True))
        alpha = jnp.exp2(m[i][c] - m_new)
        p = jnp.exp2(st - m_new)
        acc_ref[i, c] = alpha * acc_ref[i, c] + _dot(vt_ref[j], p.astype(BF16))
        l[i][c] = alpha * l[i][c] + jnp.sum(p, axis=0, keepdims=True)
        m[i][c] = m_new
    lam = _diff_lambda(lq1_ref[...], lk1_ref[...], lq2_ref[...], lk2_ref[...], lam_init)
    for i in range(nq):
        ot = acc_ref[i, 0] * (1.0 / l[i][0]) - lam * (acc_ref[i, 1] * (1.0 / l[i][1]))
        ms = jnp.mean(ot * ot, axis=0, keepdims=True)
        ot = ot * lax.rsqrt(ms + RMS_EPS) * g_ref[...] * (1.0 - lam_init)
        o_ref[i * tq:(i + 1) * tq, :] = ot.T


def _attn_prompt(q, k, v, lam_p, g_col, *, heads, lam_init):
    nb, seq, aw = q.shape
    hd = aw // heads
    qk = hd // 2
    tq = min(ATT_ROWS, seq)
    nq = seq // tq
    kern = functools.partial(_attn_prompt_kernel, tq=tq, qk=qk, heads=heads, lam_init=lam_init, scale=qk ** -0.5)
    small = pl.BlockSpec((1, qk), lambda b, h: (0, 0))
    return pl.pallas_call(
        kern,
        grid=(nb, heads),
        in_specs=[
            pl.BlockSpec((None, seq, hd), lambda b, h: (b, 0, h)),
            pl.BlockSpec((None, seq * heads, hd), lambda b, h: (b, 0, 0)),
            pl.BlockSpec((None, seq * heads, hd), lambda b, h: (b, 0, 0)),
            small, small, small, small,
            pl.BlockSpec((hd, 1), lambda b, h: (0, 0)),
        ],
        out_specs=pl.BlockSpec((seq, hd), lambda b, h: (b, h)),
        out_shape=jax.ShapeDtypeStruct((nb * seq, aw), F32),
        scratch_shapes=[pltpu.VMEM((seq, hd), BF16), pltpu.VMEM((nq, hd, tq), BF16),
                        pltpu.VMEM((2, seq, hd), BF16), pltpu.VMEM((nq, 2, hd, tq), F32)],
        compiler_params=_params("arbitrary", "arbitrary"),
        name="diff_attn_prompt",
    )(q, k, v, *lam_p, g_col)


def _attn_sample_kernel(pt_ref, q_ref, ks_ref, vs_ref, lq1_ref, lk1_ref, lq2_ref, lk2_ref, g_ref, *rest,
                        n_pages, heads, qk, lam_init, scale):
    k_pages = rest[:n_pages]
    v_pages = rest[n_pages:2 * n_pages]
    o_ref = rest[2 * n_pages]
    del pt_ref
    hd = 2 * qk
    aw = heads * hd
    n_maps = 2 * heads
    q = q_ref[...] * scale
    row = lax.broadcasted_iota(I32, (n_maps, aw), 0)
    lane = lax.broadcasted_iota(I32, (n_maps, aw), 1)
    own = (lane // hd == row % heads) & ((lane // qk) % 2 == row // heads)
    qm = jnp.where(own, jnp.broadcast_to(q, (n_maps, aw)), 0.0)
    qmb = qm.astype(BF16)
    page = k_pages[0].shape[0] // heads

    def page_rows(ref):
        return jnp.concatenate([ref[pl.ds(h, page, stride=heads), :] for h in range(heads)], axis=1).astype(BF16)

    s_pages = [_dot_nt(qmb, page_rows(kp)) for kp in k_pages]
    s_self = jnp.sum(qm * ks_ref[...], axis=1, keepdims=True)
    m = s_self
    for s in s_pages:
        m = jnp.maximum(m, jnp.max(s, axis=1, keepdims=True))
    p_self = jnp.exp(s_self - m)
    p_pages = [jnp.exp(s - m) for s in s_pages]
    l = p_self
    for p in p_pages:
        l = l + jnp.sum(p, axis=1, keepdims=True)
    inv = 1.0 / l
    lam = _diff_lambda(lq1_ref[...], lk1_ref[...], lq2_ref[...], lk2_ref[...], lam_init)

    def mix(p):
        pn = p * inv
        return pn - lam * pltpu.roll(pn, heads, 0)

    acc = jnp.broadcast_to(mix(p_self), (n_maps, aw)) * vs_ref[...]
    for p, vp in zip(p_pages, v_pages):
        acc = acc + _dot(mix(p).astype(BF16), page_rows(vp))
    pick = (row < heads) & (lane // hd == row)
    o = jnp.sum(jnp.where(pick, acc, 0.0), axis=0, keepdims=True)
    g = g_ref[...]
    for h in range(heads):
        o_ref[:, h * hd:(h + 1) * hd] = _head_norm(o[:, h * hd:(h + 1) * hd], g, lam_init)


def _attn_sample(q, k_new, v_new, cache_k, cache_v, page_ids, lam_p, g, *, page, heads, lam_init):
    bs, aw = q.shape
    n_pages = page_ids.shape[1]
    hd = aw // heads
    qk = hd // 2
    kern = functools.partial(_attn_sample_kernel, n_pages=n_pages, heads=heads, qk=qk, lam_init=lam_init,
                             scale=qk ** -0.5)
    row_spec = pl.BlockSpec((None, 1, aw), lambda b, pt: (b, 0, 0))
    small = pl.BlockSpec((1, qk), lambda b, pt: (0, 0))

    def page_spec(j):
        return pl.BlockSpec((page * heads, hd), lambda b, pt, j=j: (pt[b * n_pages + j], 0))

    grid_spec = pltpu.PrefetchScalarGridSpec(
        num_scalar_prefetch=1,
        grid=(bs,),
        in_specs=[row_spec, row_spec, row_spec, small, small, small, small,
                  pl.BlockSpec((1, hd), lambda b, pt: (0, 0))]
                 + [page_spec(j) for j in range(n_pages)] * 2,
        out_specs=row_spec,
    )
    out = pl.pallas_call(
        kern,
        grid_spec=grid_spec,
        out_shape=jax.ShapeDtypeStruct((bs, 1, aw), F32),
        compiler_params=_params("arbitrary"),
        name="diff_attn_sample",
    )(page_ids.reshape(-1).astype(I32), q.reshape(bs, 1, aw), k_new.reshape(bs, 1, aw),
      v_new.reshape(bs, 1, aw), *lam_p, g, *([cache_k] * n_pages), *([cache_v] * n_pages))
    return out.reshape(bs, aw)


def _layer_norm(z, g, b):
    mu = jnp.mean(z, axis=-1, keepdims=True)
    zc = z - mu
    var = jnp.mean(zc * zc, axis=-1, keepdims=True)
    return zc * lax.rsqrt(var + LN_EPS) * g + b


def _out_kernel(xp_ref, sp_ref, ap_ref, xs_ref, ss_ref, as_ref, wo_ref, g_ref, b_ref, wrh_ref, wrl_ref,
                br_ref, h_ref, ht_ref, e_ref, gt_ref, *, n_prompt, sw, alpha, top_k):
    i = pl.program_id(0)
    is_s = i == n_prompt
    x = jnp.where(is_s, xs_ref[...], xp_ref[...])
    s = jnp.where(is_s, ss_ref[...], sp_ref[...]).astype(BF16)
    a = jnp.where(is_s, as_ref[...], ap_ref[...]).astype(BF16)
    mix = _dot(s, wo_ref[:sw, :]) + _dot(a, wo_ref[sw:, :])
    h = _layer_norm(alpha * x + mix, g_ref[...], b_ref[...])
    h_ref[...] = h
    rows = h.shape[0]
    for j in range(h.shape[1] // LANES):
        ht_ref[pl.ds(j, rows, stride=SUBLANES), :] = h[:, j * LANES:(j + 1) * LANES]
    h_hi = h.astype(BF16)
    h_lo = (h - h_hi.astype(F32)).astype(BF16)
    lg = (_dot_nt(wrh_ref[...], h_hi) + _dot_nt(wrl_ref[...], h_hi) + _dot_nt(wrh_ref[...], h_lo)
          + br_ref[...])
    eio = lax.broadcasted_iota(I32, lg.shape, 0)
    n_exp = lg.shape[0]
    vals, idxs = [], []
    for _ in range(top_k):
        m = jnp.max(lg, axis=0, keepdims=True)
        idx = jnp.min(jnp.where(lg == m, eio, n_exp), axis=0, keepdims=True)
        vals.append(m)
        idxs.append(idx)
        lg = jnp.where(eio == idx, NEG_BIG, lg)
    tv = jnp.concatenate(vals, axis=0)
    pe = jnp.exp(tv - tv[0:1, :])
    gt_ref[...] = pe / jnp.sum(pe, axis=0, keepdims=True)
    e_ref[...] = jnp.concatenate(idxs, axis=0)


def _out_route(xp, sp, ap, xs, ss, as_, w_out_bf, g, b, wr_hi, wr_lo, br, *, alpha):
    tp, d = xp.shape
    sw = sp.shape[1]
    rows = OUT_ROWS
    n_prompt = tp // rows
    total = tp + rows
    n_exp = wr_hi.shape[0]
    pidx = lambda i: (jnp.minimum(i, n_prompt - 1), 0)
    const = lambda i: (0, 0)
    kern = functools.partial(_out_kernel, n_prompt=n_prompt, sw=sw, alpha=alpha, top_k=TOP_K)
    return pl.pallas_call(
        kern,
        grid=(n_prompt + 1,),
        in_specs=[
            pl.BlockSpec((rows, d), pidx), pl.BlockSpec((rows, sw), pidx), pl.BlockSpec((rows, d - sw), pidx),
            pl.BlockSpec((rows, d), const), pl.BlockSpec((rows, sw), const), pl.BlockSpec((rows, d - sw), const),
            pl.BlockSpec(w_out_bf.shape, const), pl.BlockSpec((1, d), const), pl.BlockSpec((1, d), const),
            pl.BlockSpec((n_exp, d), const), pl.BlockSpec((n_exp, d), const), pl.BlockSpec((n_exp, 1), const),
        ],
        out_specs=[
            pl.BlockSpec((rows, d), lambda i: (i, 0)),
            pl.BlockSpec((rows * SUBLANES, LANES), lambda i: (i, 0)),
            pl.BlockSpec((TOP_K, rows), lambda i: (0, i)),
            pl.BlockSpec((TOP_K, rows), lambda i: (0, i)),
        ],
        out_shape=[
            jax.ShapeDtypeStruct((total, d), F32),
            jax.ShapeDtypeStruct((total * SUBLANES, LANES), F32),
            jax.ShapeDtypeStruct((TOP_K, total), I32),
            jax.ShapeDtypeStruct((TOP_K, total), F32),
        ],
        compiler_params=_params("arbitrary"),
        name="out_proj_route",
    )(xp, sp, ap, xs, ss, as_, w_out_bf, g, b, wr_hi, wr_lo, br)


def _moe_kernel(be_ref, nu_ref, tokc_ref, tokn_ref, slotp_ref, h_hbm, wgu_ref, bgu_ref, wdn_ref, bdn_ref,
                y_hbm, xbuf, ybuf, wgu_bf, wdn_bf, gsem, ssem, *, rows, de):
    i = pl.program_id(0)
    n_used = nu_ref[0]
    cur = i % 2
    nxt = 1 - cur
    tile = SUBLANES

    def buf_rows(buf, r):
        return pl.ds((buf * rows + r) * tile, tile)

    def gather(tok_ref, r, buf):
        return pltpu.make_async_copy(h_hbm.at[tok_ref[0, r]], xbuf.at[buf_rows(buf, r), :], gsem.at[buf])

    def scatter_prev(r):
        return pltpu.make_async_copy(ybuf.at[buf_rows(nxt, r), :], y_hbm.at[slotp_ref[0, r]], ssem.at[nxt])

    @pl.when(i == 0)
    def _():
        for r in range(rows):
            gather(tokc_ref, r, 0).start(priority=r % 2)
        ybuf[pl.ds(rows * tile, rows * tile), :] = jnp.zeros((rows * tile, LANES), F32)

    @pl.when(i < n_used)
    def _():
        @pl.when((i == 0) | (be_ref[i] != be_ref[jnp.maximum(i - 1, 0)]))
        def _():
            wgu_bf[...] = wgu_ref[...].astype(BF16)
            wdn_bf[...] = wdn_ref[...].astype(BF16)

        for r in range(rows):
            gather(tokc_ref, r, cur).wait()
        x = jnp.concatenate(
            [xbuf[pl.ds(cur * rows * tile + j, rows, stride=tile), :] for j in range(tile)], axis=1).astype(BF16)
        cw = MOE_HIDDEN_CHUNK
        n_chunks = de // cw
        group = rows // (MOE_ISSUE_POINTS * n_chunks)

        def issue(g):
            for r in range(g * group, (g + 1) * group):
                gather(tokn_ref, r, nxt).start(priority=r % 2)
                scatter_prev(r).start(priority=r % 2)

        def up(c):
            lo, hi = c * cw, (c + 1) * cw
            return (_dot(x, wgu_bf[:, lo:hi]) + bgu_ref[:, lo:hi],
                    _dot(x, wgu_bf[:, de + lo:de + hi]) + bgu_ref[:, de + lo:de + hi])

        pending = {c: up(c) for c in range(min(MOE_LOOKAHEAD, n_chunks))}
        y = bdn_ref[...]
        for c in range(n_chunks):
            g0 = c * MOE_ISSUE_POINTS
            issue(g0)
            if c + MOE_LOOKAHEAD < n_chunks:
                pending[c + MOE_LOOKAHEAD] = up(c + MOE_LOOKAHEAD)
            issue(g0 + 1)
            glu, lin = pending.pop(c)
            glu = jnp.minimum(glu, SWIGLU_LIMIT)
            lin = jnp.clip(lin, -SWIGLU_LIMIT, SWIGLU_LIMIT)
            act = glu * jax.nn.sigmoid(SWIGLU_ALPHA * glu) * (lin + 1.0)
            issue(g0 + 2)
            y = y + _dot(act.astype(BF16), wdn_bf[c * cw:(c + 1) * cw, :])
            issue(g0 + 3)
        for j in range(tile):
            ybuf[pl.ds(cur * rows * tile + j, rows, stride=tile), :] = y[:, j * LANES:(j + 1) * LANES]
        for r in range(rows):
            scatter_prev(r).wait()

    @pl.when(i == n_used)
    def _():
        for r in range(rows):
            gather(tokn_ref, r, cur).wait()
        for r in range(rows):
            scatter_prev(r).start(priority=r % 2)
        for r in range(rows):
            scatter_prev(r).wait()


def _moe(h_tiles, block_expert, n_used, tok_buf, slot_buf, w_gu, b_gu, w_dn, b_dn, *, n_real):
    n_blocks = block_expert.shape[0]
    rows = MOE_ROWS
    n_tok, tile, lanes = h_tiles.shape
    d = tile * lanes
    n_exp, _, de2 = w_gu.shape
    de = de2 // 2
    kern = functools.partial(_moe_kernel, rows=rows, de=de)

    def used(i, nu):
        return jnp.minimum(i, nu[0] - 1)

    smem_cur = pl.BlockSpec((None, 1, rows), lambda i, be, nu: (used(i, nu), 0, 0), memory_space=pltpu.SMEM)
    smem_nxt = pl.BlockSpec((None, 1, rows), lambda i, be, nu: (used(i + 1, nu), 0, 0), memory_space=pltpu.SMEM)
    smem_prev = pl.BlockSpec((None, 1, rows), lambda i, be, nu: (jnp.minimum(i, nu[0]), 0, 0),
                             memory_space=pltpu.SMEM)

    def expert(i, be, nu):
        return (be[used(i, nu)], 0, 0)

    spare = n_real + jnp.arange(rows, dtype=I32)
    slot_prev = jnp.concatenate([spare, slot_buf]).reshape(n_blocks + 1, 1, rows)
    grid_spec = pltpu.PrefetchScalarGridSpec(
        num_scalar_prefetch=2,
        grid=(n_blocks + 1,),
        in_specs=[
            smem_cur, smem_nxt, smem_prev,
            pl.BlockSpec(memory_space=pl.ANY),
            pl.BlockSpec((None, d, de2), expert),
            pl.BlockSpec((None, 1, de2), expert),
            pl.BlockSpec((None, de, d), expert),
            pl.BlockSpec((None, 1, d), expert),
        ],
        out_specs=pl.BlockSpec(memory_space=pl.ANY),
        scratch_shapes=[
            pltpu.VMEM((2 * rows * tile, lanes), F32), pltpu.VMEM((2 * rows * tile, lanes), F32),
            pltpu.VMEM((d, de2), BF16), pltpu.VMEM((de, d), BF16),
            pltpu.SemaphoreType.DMA((2,)), pltpu.SemaphoreType.DMA((2,)),
        ],
    )
    tok3 = tok_buf.reshape(n_blocks, 1, rows)
    return pl.pallas_call(
        kern,
        grid_spec=grid_spec,
        out_shape=jax.ShapeDtypeStruct((n_real + rows, tile, lanes), F32),
        compiler_params=_params("arbitrary"),
        name="moe_experts",
    )(block_expert, n_used, tok3, tok3, slot_prev, h_tiles, w_gu, b_gu.reshape(n_exp, 1, de2), w_dn,
      b_dn.reshape(n_exp, 1, d))


def _route_plan(e_idx, n_tokens, n_exp, rows):
    k_ = e_idx.shape[0]
    n_assign = k_ * n_tokens
    chunk = LANES
    n_chunks = n_assign // chunk
    e_flat = e_idx[:, :n_tokens].reshape(n_assign)
    onehot = (e_flat.reshape(n_chunks, chunk, 1) == jnp.arange(n_exp, dtype=I32)).astype(F32)
    before = jnp.tril(jnp.ones((chunk, chunk), F32), -1)
    rank_in = jnp.einsum("ji,cie->cje", before, onehot)
    per_chunk = jnp.sum(onehot, axis=1)
    chunk_base = jnp.cumsum(per_chunk, axis=0) - per_chunk
    rank = jnp.sum((rank_in + chunk_base[:, None, :]) * onehot, axis=2).reshape(n_assign).astype(I32)
    counts = jnp.sum(per_chunk, axis=0).astype(I32)
    padded = ((counts + rows - 1) // rows) * rows
    pend = jnp.cumsum(padded)
    pstart = pend - padded
    dest = pstart[e_flat] + rank
    n_blocks = -(-(n_assign + n_exp * (rows - 1)) // rows)
    n_rows = n_blocks * rows
    a_id = jnp.arange(n_assign, dtype=I32)
    sorted_a = jnp.full((n_rows,), -1, I32).at[dest].set(a_id)
    real = sorted_a >= 0
    pos = jnp.arange(n_rows, dtype=I32)
    tok_buf = jnp.where(real, sorted_a % n_tokens, 0)
    slot_buf = jnp.where(real, sorted_a, n_assign + pos % rows)
    block_start = jnp.arange(n_blocks, dtype=I32) * rows
    block_expert = jnp.minimum(jnp.sum((pend[None, :] <= block_start[:, None]).astype(I32), axis=1), n_exp - 1)
    n_used = (pend[-1] // rows).astype(I32).reshape(1)
    return tok_buf, slot_buf, block_expert, n_used


def _final_kernel(*refs, top_k, n_prompt, alpha):
    y_refs = refs[:top_k]
    gt_ref, h_ref, g_ref, b_ref, op_ref, os_ref = refs[top_k:]
    i = pl.program_id(0)
    rows = h_ref.shape[0]
    eye = lax.broadcasted_iota(I32, (rows, rows), 0) == lax.broadcasted_iota(I32, (rows, rows), 1)
    f = None
    for k, y_ref in enumerate(y_refs):
        yk = jnp.concatenate([y_ref[pl.ds(j, rows, stride=SUBLANES), :] for j in range(SUBLANES)], axis=1)
        gk = jnp.sum(jnp.where(eye, jnp.broadcast_to(gt_ref[k:k + 1, :], (rows, rows)), 0.0), axis=1, keepdims=True)
        f = gk * yk if f is None else f + gk * yk
    out = _layer_norm(alpha * h_ref[...] + f, g_ref[...], b_ref[...])

    @pl.when(i < n_prompt)
    def _():
        op_ref[...] = out

    @pl.when(i == n_prompt)
    def _():
        os_ref[...] = out


def _final(y_tiles, gates, h_all, g, b, *, tp, ts, alpha):
    rows = FINAL_ROWS
    assert ts == rows and tp % rows == 0
    d = h_all.shape[1]
    n_prompt = tp // rows
    n_tok_blocks = n_prompt + 1
    const = lambda i: (0, 0)
    kern = functools.partial(_final_kernel, top_k=TOP_K, n_prompt=n_prompt, alpha=alpha)
    y_specs = [pl.BlockSpec((rows * SUBLANES, LANES), lambda i, k=k: (k * n_tok_blocks + i, 0))
               for k in range(TOP_K)]
    return pl.pallas_call(
        kern,
        grid=(n_tok_blocks,),
        in_specs=y_specs + [pl.BlockSpec((TOP_K, rows), lambda i: (0, i)),
                            pl.BlockSpec((rows, d), lambda i: (i, 0)), pl.BlockSpec((1, d), const),
                            pl.BlockSpec((1, d), const)],
        out_specs=[pl.BlockSpec((rows, d), lambda i: (jnp.minimum(i, n_prompt - 1), 0)),
                   pl.BlockSpec((rows, d), const)],
        out_shape=[jax.ShapeDtypeStruct((tp, d), F32), jax.ShapeDtypeStruct((ts, d), F32)],
        compiler_params=_params("arbitrary"),
        name="combine_ln",
    )(*([y_tiles] * TOP_K), gates, h_all, g, b)


def _pad_rows(a, rows):
    return jnp.pad(a, ((0, rows - a.shape[0]), (0, 0)))


def kernel(x_prompt, x_sample, cache_k, cache_v, state_ssm_re, state_ssm_im, page_table, w_in, w_out, ssm_a_re, ssm_a_im, ssm_log_dt, ssm_b_re, ssm_b_im, ssm_c_re, ssm_c_im, ssm_d, w_glu, b_glu, lambda_q1, lambda_k1, lambda_q2, lambda_k2, subln_g, ln1_g, ln1_b, w_router, b_router, w_gate_up, b_gate_up, w_down, b_down, ln2_g, ln2_b):
    depth = w_in.shape[0]
    bp, lp, d = x_prompt.shape
    bs, ls, _ = x_sample.shape
    n_phys, page, heads, hd = cache_k.shape[1:]
    assert ls == 1 and hd == LANES and d == SUBLANES * LANES
    qk = hd // 2
    aw = heads * hd
    groups, n_state = ssm_a_re.shape[1:]
    sw = groups * ssm_b_re.shape[-1]
    gp = groups * n_state
    n_exp = w_router.shape[2]
    past_len = page_table.shape[1] * page
    alpha = (2 * depth) ** 0.25
    tp = bp * lp
    n_tok = tp + bs

    cos_p, sin_p = _rope_tables(jnp.arange(lp, dtype=I32), qk // 2)
    cos_s, sin_s = _rope_tables(jnp.full((bs,), past_len, I32), qk // 2)
    ck = cache_k.reshape(depth * n_phys * page * heads, hd)
    cv = cache_v.reshape(depth * n_phys * page * heads, hd)

    hp = x_prompt
    hs = x_sample.reshape(1, bs, d)
    outs = [[] for _ in range(8)]
    for l in range(depth):
        lam_init = 0.8 - 0.6 * math.exp(-0.3 * l)
        w_in_bf = w_in[l].astype(BF16)
        w_out_bf = w_out[l].astype(BF16)
        w_glu_bf = w_glu[l].astype(BF16)
        bg = b_glu[l].reshape(1, 2 * sw)
        lam_p = [p[l].reshape(1, qk).astype(F32) for p in (lambda_q1, lambda_k1, lambda_q2, lambda_k2)]
        prm = _ssm_params(ssm_a_re[l], ssm_a_im[l], ssm_log_dt[l], ssm_b_re[l], ssm_b_im[l],
                          ssm_c_re[l], ssm_c_im[l], ssm_d[l])

        tt = min(PROJ_ROWS // bp, lp)
        u_p, q_p, k_p, v_p = _project(hp, w_in_bf, cos_p, sin_p, tt=tt, sw=sw, aw=aw, half=qk // 2)
        zero = jnp.zeros((bp, gp), F32)
        ssm_p, sre_p, sim_p = _ssm(u_p, zero, zero, prm, nb=bp, steps=min(SSM_STEPS, lp), w_glu_bf=w_glu_bf,
                                   b_glu=bg)
        att_p = _attn_prompt(q_p, k_p, v_p, lam_p, subln_g[l].reshape(hd, 1), heads=heads, lam_init=lam_init)

        u_s, q_s, k_s, v_s = _project(hs, w_in_bf, cos_s, sin_s, tt=bs, sw=sw, aw=aw, half=qk // 2)
        ssm_s, sre_s, sim_s = _ssm(u_s, state_ssm_re[l].reshape(bs, gp), state_ssm_im[l].reshape(bs, gp), prm,
                                   nb=bs, steps=1, w_glu_bf=w_glu_bf, b_glu=bg)
        att_s = _attn_sample(q_s.reshape(bs, aw), k_s.reshape(bs, aw), v_s.reshape(bs, aw), ck, cv,
                             page_table + l * n_phys, lam_p, subln_g[l].reshape(1, hd), page=page, heads=heads,
                             lam_init=lam_init)

        wr = w_router[l].T.astype(F32)
        wr_hi = wr.astype(BF16)
        wr_lo = (wr - wr_hi.astype(F32)).astype(BF16)
        h_all, h_tiles, e_idx, gates = _out_route(
            hp.reshape(tp, d), ssm_p.reshape(tp, sw), att_p, _pad_rows(hs.reshape(bs, d), OUT_ROWS),
            _pad_rows(ssm_s, OUT_ROWS), _pad_rows(att_s, OUT_ROWS), w_out_bf, ln1_g[l].reshape(1, d),
            ln1_b[l].reshape(1, d), wr_hi, wr_lo, b_router[l].reshape(n_exp, 1), alpha=alpha)
        tok_buf, slot_buf, block_expert, n_used = _route_plan(e_idx, n_tok, n_exp, MOE_ROWS)
        y_tiles = _moe(h_tiles.reshape(-1, SUBLANES, LANES), block_expert, n_used, tok_buf, slot_buf,
                       w_gate_up[l], b_gate_up[l], w_down[l], b_down[l], n_real=TOP_K * n_tok)
        y_p, y_s = _final(y_tiles.reshape(-1, LANES), gates, h_all, ln2_g[l].reshape(1, d),
                          ln2_b[l].reshape(1, d), tp=tp, ts=bs, alpha=alpha)
        hp = y_p.reshape(bp, lp, d)
        hs = y_s.reshape(1, bs, d)

        for lst, val in zip(outs, (
                k_p.reshape(bp, lp, heads, hd), v_p.reshape(bp, lp, heads, hd),
                sre_p.reshape(bp, groups, n_state), sim_p.reshape(bp, groups, n_state),
                k_s.reshape(bs, ls, heads, hd), v_s.reshape(bs, ls, heads, hd),
                sre_s.reshape(bs, groups, n_state), sim_s.reshape(bs, groups, n_state))):
            lst.append(val)

    return (hp, hs.reshape(bs, ls, d)) + tuple(jnp.stack(o) for o in outs)
```

```python
import functools
import math

import jax
import jax.numpy as jnp
from jax import lax
from jax.experimental import pallas as pl
from jax.experimental.pallas import tpu as pltpu

F32 = jnp.float32
BF16 = jnp.bfloat16
I32 = jnp.int32

TOP_K = 4
ROPE_THETA = 10000.0
SWIGLU_ALPHA = 1.702
SWIGLU_LIMIT = 7.0
LN_EPS = 1e-5
RMS_EPS = 1e-5
SSM_A_RE_MAX = -1e-4
GELU_C0 = math.sqrt(2.0 / math.pi)
GELU_C1 = 0.044715
NEG_BIG = -1e30
LOG2E = math.log2(math.e)

LANES = 128
SUBLANES = 8
MXU_DIM = 256
VMEM_LIMIT_BYTES = 56 * 1024 * 1024

PROJ_ROWS = 512
SSM_STEPS = 64
ATT_ROWS = 256
ATT_LOOKAHEAD = 4
MOE_HIDDEN_CHUNK = 256
MOE_LOOKAHEAD = 2
ROUTE_CHUNK = 8
OUT_ROWS = 512
MOE_ROWS = 256
SCAN_LANES = 512


def _params(*sem):
    return pltpu.CompilerParams(dimension_semantics=sem, vmem_limit_bytes=VMEM_LIMIT_BYTES)


def _dot(a, b):
    return jnp.dot(a, b, preferred_element_type=F32)


def _dot_nt(a, b):
    return lax.dot_general(a, b, (((1,), (1,)), ((), ())), preferred_element_type=F32)


def _proj_kernel(x_ref, w_ref, cos_ref, sin_ref, u_ref, q_ref, k_ref, v_ref, *, nb, tt, sw, aw, half):
    d = x_ref.shape[-1]
    heads = aw // LANES
    acc = _dot(x_ref[...].reshape(nb * tt, d).astype(BF16), w_ref[...])
    c = cos_ref[...]
    s = sin_ref[...]
    lane = lax.broadcasted_iota(I32, c.shape, 1)
    first = (lane % (2 * half)) < half

    def rope(blk):
        swapped = jnp.where(first, pltpu.roll(blk, LANES - half, 1), pltpu.roll(blk, half, 1))
        return blk * c + swapped * s

    for b in range(nb):
        rows = slice(b * tt, (b + 1) * tt)
        for j in range(sw // LANES):
            u_ref[j, pl.ds(b, tt, stride=nb), :] = acc[rows, j * LANES:(j + 1) * LANES]
        q_ref[b] = jnp.concatenate(
            [rope(acc[rows, sw + j * LANES:sw + (j + 1) * LANES]) for j in range(heads)], axis=1)
        for j in range(heads):
            lo = sw + aw + j * LANES
            k_ref[b, pl.ds(j, tt, stride=heads), :] = rope(acc[rows, lo:lo + LANES])
            lo = sw + 2 * aw + j * LANES
            v_ref[b, pl.ds(j, tt, stride=heads), :] = acc[rows, lo:lo + LANES]


def _project(x3, w_bf, cos_t, sin_t, *, tt, sw, aw, half):
    nb, seq, d = x3.shape
    heads = aw // LANES
    kern = functools.partial(_proj_kernel, nb=nb, tt=tt, sw=sw, aw=aw, half=half)
    return pl.pallas_call(
        kern,
        grid=(seq // tt,),
        in_specs=[
            pl.BlockSpec((nb, tt, d), lambda i: (0, i, 0)),
            pl.BlockSpec(w_bf.shape, lambda i: (0, 0)),
            pl.BlockSpec((tt, LANES), lambda i: (i, 0)),
            pl.BlockSpec((tt, LANES), lambda i: (i, 0)),
        ],
        out_specs=[
            pl.BlockSpec((sw // LANES, tt * nb, LANES), lambda i: (0, i, 0)),
            pl.BlockSpec((nb, tt, aw), lambda i: (0, i, 0)),
            pl.BlockSpec((nb, tt * heads, LANES), lambda i: (0, i, 0)),
            pl.BlockSpec((nb, tt * heads, LANES), lambda i: (0, i, 0)),
        ],
        out_shape=[
            jax.ShapeDtypeStruct((sw // LANES, seq * nb, LANES), F32),
            jax.ShapeDtypeStruct((nb, seq, aw), F32),
            jax.ShapeDtypeStruct((nb, seq * heads, LANES), F32),
            jax.ShapeDtypeStruct((nb, seq * heads, LANES), F32),
        ],
        compiler_params=_params("arbitrary"),
        name="proj_rope",
    )(x3, w_bf, cos_t, sin_t)


def _rope_tables(pos, half):
    inv = ROPE_THETA ** (-jnp.arange(half, dtype=F32) / half)
    ang = pos.astype(F32)[:, None] * inv[None, :]
    reps = LANES // half
    cos = jnp.tile(jnp.cos(ang), (1, reps))
    sin = jnp.sin(ang)
    sin = jnp.tile(jnp.concatenate([-sin, sin], axis=1), (1, reps // 2))
    return cos, sin


def _ssm_kernel(u_ref, s0r_ref, s0i_ref, ar_ref, ai_ref, wb_ref, wc_ref, d_ref, wg_ref, bg_ref,
                o_ref, sr_ref, si_ref, bu_ref, st_ref, ot_ref, *, nb, steps, gp, sw):
    i = pl.program_id(0)
    rows = nb * steps

    @pl.when(i == 0)
    def _():
        st_ref[:, :gp] = s0r_ref[...]
        st_ref[:, gp:] = s0i_ref[...]

    u = jnp.concatenate([u_ref[j] for j in range(sw // LANES)], axis=1)
    ub = u.astype(BF16)
    n_tiles = 2 * gp // MXU_DIM
    per_half = gp // MXU_DIM
    in_tiles = sw // MXU_DIM
    for j in range(n_tiles):
        kt = ((j % per_half) * in_tiles) // per_half
        bu_ref[:, j * MXU_DIM:(j + 1) * MXU_DIM] = _dot(ub[:, kt * MXU_DIM:(kt + 1) * MXU_DIM], wb_ref[j])

    for c in range(gp // SCAN_LANES):
        lo_r = c * SCAN_LANES
        lo_i = gp + c * SCAN_LANES
        ar = jnp.broadcast_to(ar_ref[:, lo_r:lo_r + SCAN_LANES], (nb, SCAN_LANES))
        ai = jnp.broadcast_to(ai_ref[:, lo_r:lo_r + SCAN_LANES], (nb, SCAN_LANES))

        def step(t, carry, lo_r=lo_r, lo_i=lo_i, ar=ar, ai=ai):
            sr, si = carry
            r0 = pl.multiple_of(t * nb, nb)
            br = bu_ref[pl.ds(r0, nb), lo_r:lo_r + SCAN_LANES]
            bi = bu_ref[pl.ds(r0, nb), lo_i:lo_i + SCAN_LANES]
            nr = ar * sr - ai * si + br
            ni = ar * si + ai * sr + bi
            bu_ref[pl.ds(r0, nb), lo_r:lo_r + SCAN_LANES] = nr
            bu_ref[pl.ds(r0, nb), lo_i:lo_i + SCAN_LANES] = ni
            return nr, ni

        sr0 = st_ref[:, lo_r:lo_r + SCAN_LANES]
        si0 = st_ref[:, lo_i:lo_i + SCAN_LANES]
        if steps == 1:
            sr1, si1 = step(0, (sr0, si0))
        else:
            sr1, si1 = lax.fori_loop(0, steps, step, (sr0, si0))
        st_ref[:, lo_r:lo_r + SCAN_LANES] = sr1
        st_ref[:, lo_i:lo_i + SCAN_LANES] = si1

    sr_ref[...] = st_ref[:, :gp]
    si_ref[...] = st_ref[:, gp:]

    out_tiles = sw // MXU_DIM
    kw = gp // out_tiles
    ys = []
    for nt in range(out_tiles):
        s_re = bu_ref[:, nt * kw:(nt + 1) * kw].astype(BF16)
        s_im = bu_ref[:, gp + nt * kw:gp + (nt + 1) * kw].astype(BF16)
        ys.append(_dot(s_re, wc_ref[nt, 0]) + _dot(s_im, wc_ref[nt, 1]))
    y = jnp.concatenate(ys, axis=1) + d_ref[...] * u
    y = 0.5 * y * (1.0 + jnp.tanh(GELU_C0 * (y + GELU_C1 * (y * y * y))))
    z = _dot(y.astype(BF16), wg_ref[...]) + bg_ref[...]
    out = z[:, :sw] * jax.nn.sigmoid(z[:, sw:])
    if steps == 1:
        o_ref[...] = out
    else:
        for j in range(sw // LANES):
            ot_ref[j] = out[:, j * LANES:(j + 1) * LANES]
        for b in range(nb):
            for j in range(sw // LANES):
                o_ref[b, :, j * LANES:(j + 1) * LANES] = ot_ref[j, pl.ds(b, steps, stride=nb), :]


def _ssm(u_tb, s0r, s0i, prm, *, nb, steps, w_glu_bf, b_glu):
    n_slabs, n_rows, _ = u_tb.shape
    sw = n_slabs * LANES
    seq = n_rows // nb
    gp = prm["a_re"].shape[1]
    rows = nb * steps
    grid = (seq // steps,)
    const2 = lambda i: (0, 0)
    kern = functools.partial(_ssm_kernel, nb=nb, steps=steps, gp=gp, sw=sw)
    if steps == 1:
        o_spec = pl.BlockSpec((nb, sw), const2)
        o_shape = jax.ShapeDtypeStruct((nb, sw), F32)
    else:
        o_spec = pl.BlockSpec((nb, steps, sw), lambda i: (0, i, 0))
        o_shape = jax.ShapeDtypeStruct((nb, seq, sw), F32)
    return pl.pallas_call(
        kern,
        grid=grid,
        in_specs=[
            pl.BlockSpec((n_slabs, rows, LANES), lambda i: (0, i, 0)),
            pl.BlockSpec((nb, gp), const2),
            pl.BlockSpec((nb, gp), const2),
            pl.BlockSpec((1, gp), const2),
            pl.BlockSpec((1, gp), const2),
            pl.BlockSpec(prm["wb"].shape, lambda i: (0, 0, 0)),
            pl.BlockSpec(prm["wc"].shape, lambda i: (0, 0, 0, 0)),
            pl.BlockSpec((1, sw), const2),
            pl.BlockSpec(w_glu_bf.shape, const2),
            pl.BlockSpec((1, 2 * sw), const2),
        ],
        out_specs=[o_spec, pl.BlockSpec((nb, gp), const2), pl.BlockSpec((nb, gp), const2)],
        out_shape=[o_shape, jax.ShapeDtypeStruct((nb, gp), F32), jax.ShapeDtypeStruct((nb, gp), F32)],
        scratch_shapes=[
            pltpu.VMEM((rows, 2 * gp), F32),
            pltpu.VMEM((nb, 2 * gp), F32),
            pltpu.VMEM((sw // LANES, rows, LANES), F32),
        ],
        compiler_params=_params("arbitrary"),
        name="ssm_mixer",
    )(u_tb, s0r, s0i, prm["a_re"], prm["a_im"], prm["wb"], prm["wc"], prm["d"], w_glu_bf, b_glu)


def _ssm_params(a_re, a_im, log_dt, b_re, b_im, c_re, c_im, d):
    g_, p_ = a_re.shape
    h_ = b_re.shape[-1]
    gp = g_ * p_
    sw = g_ * h_
    a_re = jnp.minimum(a_re.astype(F32), SSM_A_RE_MAX)
    a_im = a_im.astype(F32)
    dt = jnp.exp(log_dt.astype(F32))[:, None]
    mag = jnp.exp(a_re * dt)
    lb_re = mag * jnp.cos(a_im * dt)
    lb_im = mag * jnp.sin(a_im * dt)
    den = jnp.square(a_re) + jnp.square(a_im)
    nr, ni = lb_re - 1.0, lb_im
    f_re = (nr * a_re + ni * a_im) / den
    f_im = (ni * a_re - nr * a_im) / den
    b_re = b_re.astype(F32)
    b_im = b_im.astype(F32)
    bb_re = f_re[..., None] * b_re - f_im[..., None] * b_im
    bb_im = f_re[..., None] * b_im + f_im[..., None] * b_re
    eye = jnp.eye(g_, dtype=F32)
    w_in_re = jnp.einsum("gph,gk->ghkp", bb_re, eye).reshape(sw, gp)
    w_in_im = jnp.einsum("gph,gk->ghkp", bb_im, eye).reshape(sw, gp)
    w_in = jnp.concatenate([w_in_re, w_in_im], axis=1)
    per_half = gp // MXU_DIM
    in_tiles = sw // MXU_DIM
    tiles = []
    for j in range(2 * per_half):
        kt = ((j % per_half) * in_tiles) // per_half
        tiles.append(w_in[kt * MXU_DIM:(kt + 1) * MXU_DIM, j * MXU_DIM:(j + 1) * MXU_DIM])
    wb = jnp.stack(tiles).astype(BF16)
    w_out_re = jnp.einsum("ghp,gk->gpkh", c_re.astype(F32), eye).reshape(gp, sw)
    w_out_im = -jnp.einsum("ghp,gk->gpkh", c_im.astype(F32), eye).reshape(gp, sw)
    out_tiles = sw // MXU_DIM
    kw = gp // out_tiles
    wc = jnp.stack([
        jnp.stack([w_out_re[nt * kw:(nt + 1) * kw, nt * MXU_DIM:(nt + 1) * MXU_DIM],
                   w_out_im[nt * kw:(nt + 1) * kw, nt * MXU_DIM:(nt + 1) * MXU_DIM]])
        for nt in range(out_tiles)]).astype(BF16)
    return dict(a_re=lb_re.reshape(1, gp), a_im=lb_im.reshape(1, gp), wb=wb, wc=wc,
                d=d.astype(F32).reshape(1, sw))


def _diff_lambda(lq1, lk1, lq2, lk2, lam_init):
    l1 = jnp.exp(jnp.sum(lq1 * lk1, axis=1, keepdims=True))
    l2 = jnp.exp(jnp.sum(lq2 * lk2, axis=1, keepdims=True))
    return l1 - l2 + lam_init


def _head_norm(o, g, lam_init):
    ms = jnp.mean(o * o, axis=-1, keepdims=True)
    return o * lax.rsqrt(ms + RMS_EPS) * g * (1.0 - lam_init)


def _attn_prompt_kernel(q_ref, k_ref, v_ref, lq1_ref, lk1_ref, lq2_ref, lk2_ref, g_ref, o_ref,
                        kb_ref, vt_ref, qs_ref, acc_ref, *, tq, qk, heads, lam_init, scale):
    h = pl.program_id(1)
    nq = vt_ref.shape[0]
    seq = nq * tq
    kb_ref[...] = k_ref[pl.ds(h, seq, stride=heads), :].astype(BF16)
    for jb in range(nq):
        vt_ref[jb] = v_ref[pl.ds(jb * tq * heads + h, tq, stride=heads), :].T.astype(BF16)
    q = q_ref[...] * (scale * LOG2E)
    lane = lax.broadcasted_iota(I32, q.shape, 1)
    qs_ref[0] = jnp.where(lane < qk, q, 0.0).astype(BF16)
    qs_ref[1] = jnp.where(lane >= qk, q, 0.0).astype(BF16)
    acc_ref[...] = jnp.zeros(acc_ref.shape, F32)
    key = lax.broadcasted_iota(I32, (tq, tq), 0)
    qry = lax.broadcasted_iota(I32, (tq, tq), 1)
    keep = key <= qry
    m = [[jnp.full((1, tq), NEG_BIG, F32)] * 2 for _ in range(nq)]
    l = [[jnp.zeros((1, tq), F32)] * 2 for _ in range(nq)]
    tasks = [(j, i, c) for j in range(nq) for i in range(j, nq) for c in range(2)]

    def scores(t):
        j, i, c = tasks[t]
        st = _dot_nt(kb_ref[j * tq:(j + 1) * tq, :], qs_ref[c, i * tq:(i + 1) * tq, :])
        return jnp.where(keep, st, NEG_BIG) if i == j else st

    pending = {t: scores(t) for t in range(min(ATT_LOOKAHEAD, len(tasks)))}
    for t, (j, i, c) in enumerate(tasks):
        if t + ATT_LOOKAHEAD < len(tasks):
            pending[t + ATT_LOOKAHEAD] = scores(t + ATT_LOOKAHEAD)
        st = pending.pop(t)
        m_new = jnp.maximum(m[i][c], jnp.max(st, axis=0, keepdims=True))
        alpha = jnp.exp2(m[i][c] - m_new)
        p = jnp.exp2(st - m_new)
        acc_ref[i, c] = alpha * acc_ref[i, c] + _dot(vt_ref[j], p.astype(BF16))
        l[i][c] = alpha * l[i][c] + jnp.sum(p, axis=0, keepdims=True)
        m[i][c] = m_new
    lam = _diff_lambda(lq1_ref[...], lk1_ref[...], lq2_ref[...], lk2_ref[...], lam_init)
    for i in range(nq):
        ot = acc_ref[i, 0] * (1.0 / l[i][0]) - lam * (acc_ref[i, 1] * (1.0 / l[i][1]))
        ms = jnp.mean(ot * ot, axis=0, keepdims=True)
        ot = ot * lax.rsqrt(ms + RMS_EPS) * g_ref[...] * (1.0 - lam_init)
        o_ref[i * tq:(i + 1) * tq, :] = ot.T


def _attn_prompt(q, k, v, lam_p, g_col, *, heads, lam_init):
    nb, seq, aw = q.shape
    hd = aw // heads
    qk = hd // 2
    tq = min(ATT_ROWS, seq)
    nq = seq // tq
    kern = functools.partial(_attn_prompt_kernel, tq=tq, qk=qk, heads=heads, lam_init=lam_init, scale=qk ** -0.5)
    small = pl.BlockSpec((1, qk), lambda b, h: (0, 0))
    return pl.pallas_call(
        kern,
        grid=(nb, heads),
        in_specs=[
            pl.BlockSpec((None, seq, hd), lambda b, h: (b, 0, h)),
            pl.BlockSpec((None, seq * heads, hd), lambda b, h: (b, 0, 0)),
            pl.BlockSpec((None, seq * heads, hd), lambda b, h: (b, 0, 0)),
            small, small, small, small,
            pl.BlockSpec((hd, 1), lambda b, h: (0, 0)),
        ],
        out_specs=pl.BlockSpec((seq, hd), lambda b, h: (b, h)),
        out_shape=jax.ShapeDtypeStruct((nb * seq, aw), F32),
        scratch_shapes=[pltpu.VMEM((seq, hd), BF16), pltpu.VMEM((nq, hd, tq), BF16),
                        pltpu.VMEM((2, seq, hd), BF16), pltpu.VMEM((nq, 2, hd, tq), F32)],
        compiler_params=_params("arbitrary", "arbitrary"),
        name="diff_attn_prompt",
    )(q, k, v, *lam_p, g_col)


def _attn_sample_kernel(pt_ref, q_ref, ks_ref, vs_ref, lq1_ref, lk1_ref, lq2_ref, lk2_ref, g_ref, *rest,
                        n_pages, heads, qk, lam_init, scale):
    k_pages = rest[:n_pages]
    v_pages = rest[n_pages:2 * n_pages]
    o_ref = rest[2 * n_pages]
    del pt_ref
    hd = 2 * qk
    aw = heads * hd
    n_maps = 2 * heads
    q = q_ref[...] * scale
    row = lax.broadcasted_iota(I32, (n_maps, aw), 0)
    lane = lax.broadcasted_iota(I32, (n_maps, aw), 1)
    own = (lane // hd == row % heads) & ((lane // qk) % 2 == row // heads)
    qm = jnp.where(own, jnp.broadcast_to(q, (n_maps, aw)), 0.0)
    qmb = qm.astype(BF16)
    page = k_pages[0].shape[0] // heads

    def page_rows(ref):
        return jnp.concatenate([ref[pl.ds(h, page, stride=heads), :] for h in range(heads)], axis=1).astype(BF16)

    s_pages = [_dot_nt(qmb, page_rows(kp)) for kp in k_pages]
    s_self = jnp.sum(qm * ks_ref[...], axis=1, keepdims=True)
    m = s_self
    for s in s_pages:
        m = jnp.maximum(m, jnp.max(s, axis=1, keepdims=True))
    p_self = jnp.exp(s_self - m)
    p_pages = [jnp.exp(s - m) for s in s_pages]
    l = p_self
    for p in p_pages:
        l = l + jnp.sum(p, axis=1, keepdims=True)
    inv = 1.0 / l
    lam = _diff_lambda(lq1_ref[...], lk1_ref[...], lq2_ref[...], lk2_ref[...], lam_init)

    def mix(p):
        pn = p * inv
        return pn - lam * pltpu.roll(pn, heads, 0)

    acc = jnp.broadcast_to(mix(p_self), (n_maps, aw)) * vs_ref[...]
    for p, vp in zip(p_pages, v_pages):
        acc = acc + _dot(mix(p).astype(BF16), page_rows(vp))
    pick = (row < heads) & (lane // hd == row)
    o = jnp.sum(jnp.where(pick, acc, 0.0), axis=0, keepdims=True)
    g = g_ref[...]
    for h in range(heads):
        o_ref[:, h * hd:(h + 1) * hd] = _head_norm(o[:, h * hd:(h + 1) * hd], g, lam_init)


def _attn_sample(q, k_new, v_new, cache_k, cache_v, page_ids, lam_p, g, *, page, heads, lam_init):
    bs, aw = q.shape
    n_pages = page_ids.shape[1]
    hd = aw // heads
    qk = hd // 2
    kern = functools.partial(_attn_sample_kernel, n_pages=n_pages, heads=heads, qk=qk, lam_init=lam_init,
                             scale=qk ** -0.5)
    row_spec = pl.BlockSpec((None, 1, aw), lambda b, pt: (b, 0, 0))
    small = pl.BlockSpec((1, qk), lambda b, pt: (0, 0))

    def page_spec(j):
        return pl.BlockSpec((page * heads, hd), lambda b, pt, j=j: (pt[b * n_pages + j], 0))

    grid_spec = pltpu.PrefetchScalarGridSpec(
        num_scalar_prefetch=1,
        grid=(bs,),
        in_specs=[row_spec, row_spec, row_spec, small, small, small, small,
                  pl.BlockSpec((1, hd), lambda b, pt: (0, 0))]
                 + [page_spec(j) for j in range(n_pages)] * 2,
        out_specs=row_spec,
    )
    out = pl.pallas_call(
        kern,
        grid_spec=grid_spec,
        out_shape=jax.ShapeDtypeStruct((bs, 1, aw), F32),
        compiler_params=_params("arbitrary"),
        name="diff_attn_sample",
    )(page_ids.reshape(-1).astype(I32), q.reshape(bs, 1, aw), k_new.reshape(bs, 1, aw),
      v_new.reshape(bs, 1, aw), *lam_p, g, *([cache_k] * n_pages), *([cache_v] * n_pages))
    return out.reshape(bs, aw)


def _layer_norm(z, g, b):
    mu = jnp.mean(z, axis=-1, keepdims=True)
    zc = z - mu
    var = jnp.mean(zc * zc, axis=-1, keepdims=True)
    return zc * lax.rsqrt(var + LN_EPS) * g + b


def _out_kernel(xp_ref, sp_ref, ap_ref, xs_ref, ss_ref, as_ref, wo_ref, g_ref, b_ref, wrh_ref, wrl_ref,
                br_ref, h_ref, e_ref, gt_ref, *, n_prompt, sw, alpha, top_k):
    i = pl.program_id(0)
    is_s = i == n_prompt
    x = jnp.where(is_s, xs_ref[...], xp_ref[...])
    s = jnp.where(is_s, ss_ref[...], sp_ref[...]).astype(BF16)
    a = jnp.where(is_s, as_ref[...], ap_ref[...]).astype(BF16)
    mix = _dot(s, wo_ref[:sw, :]) + _dot(a, wo_ref[sw:, :])
    h = _layer_norm(alpha * x + mix, g_ref[...], b_ref[...])
    h_ref[...] = h
    h_hi = h.astype(BF16)
    h_lo = (h - h_hi.astype(F32)).astype(BF16)
    lg = (_dot_nt(wrh_ref[...], h_hi) + _dot_nt(wrl_ref[...], h_hi) + _dot_nt(wrh_ref[...], h_lo)
          + br_ref[...])
    eio = lax.broadcasted_iota(I32, lg.shape, 0)
    n_exp = lg.shape[0]
    vals, idxs = [], []
    for _ in range(top_k):
        m = jnp.max(lg, axis=0, keepdims=True)
        idx = jnp.min(jnp.where(lg == m, eio, n_exp), axis=0, keepdims=True)
        vals.append(m)
        idxs.append(idx)
        lg = jnp.where(eio == idx, NEG_BIG, lg)
    tv = jnp.concatenate(vals, axis=0)
    pe = jnp.exp(tv - tv[0:1, :])
    gt_ref[...] = pe / jnp.sum(pe, axis=0, keepdims=True)
    e_ref[...] = jnp.concatenate(idxs, axis=0)


def _out_route(xp, sp, ap, xs, ss, as_, w_out_bf, g, b, wr_hi, wr_lo, br, *, alpha):
    tp, d = xp.shape
    sw = sp.shape[1]
    rows = OUT_ROWS
    n_prompt = tp // rows
    total = tp + rows
    n_exp = wr_hi.shape[0]
    pidx = lambda i: (jnp.minimum(i, n_prompt - 1), 0)
    const = lambda i: (0, 0)
    kern = functools.partial(_out_kernel, n_prompt=n_prompt, sw=sw, alpha=alpha, top_k=TOP_K)
    return pl.pallas_call(
        kern,
        grid=(n_prompt + 1,),
        in_specs=[
            pl.BlockSpec((rows, d), pidx), pl.BlockSpec((rows, sw), pidx), pl.BlockSpec((rows, d - sw), pidx),
            pl.BlockSpec((rows, d), const), pl.BlockSpec((rows, sw), const), pl.BlockSpec((rows, d - sw), const),
            pl.BlockSpec(w_out_bf.shape, const), pl.BlockSpec((1, d), const), pl.BlockSpec((1, d), const),
            pl.BlockSpec((n_exp, d), const), pl.BlockSpec((n_exp, d), const), pl.BlockSpec((n_exp, 1), const),
        ],
        out_specs=[
            pl.BlockSpec((rows, d), lambda i: (i, 0)),
            pl.BlockSpec((TOP_K, rows), lambda i: (0, i)),
            pl.BlockSpec((TOP_K, rows), lambda i: (0, i)),
        ],
        out_shape=[
            jax.ShapeDtypeStruct((total, d), F32),
            jax.ShapeDtypeStruct((TOP_K, total), I32),
            jax.ShapeDtypeStruct((TOP_K, total), F32),
        ],
        compiler_params=_params("arbitrary"),
        name="out_proj_route",
    )(xp, sp, ap, xs, ss, as_, w_out_bf, g, b, wr_hi, wr_lo, br)


def _tile_plan(e_tile, tile_base, n_tok, n_exp):
    k_, tt = e_tile.shape
    valid = (tile_base + lax.broadcasted_iota(I32, (1, tt), 1)) < n_tok
    eio = lax.broadcasted_iota(I32, (n_exp, tt), 0)
    onehot = [jnp.where((eio == e_tile[k:k + 1, :]) & valid, 1.0, 0.0) for k in range(k_)]
    count_k = [jnp.sum(o, axis=1, keepdims=True) for o in onehot]
    count = functools.reduce(lambda a, b: a + b, count_k)
    padded = jnp.floor((count + (ROUTE_CHUNK - 1)) * (1.0 / ROUTE_CHUNK)) * ROUTE_CHUNK
    r = lax.broadcasted_iota(I32, (n_exp, n_exp), 0)
    c = lax.broadcasted_iota(I32, (n_exp, n_exp), 1)
    padded_row = jnp.sum(jnp.where(r == c, jnp.broadcast_to(padded, (n_exp, n_exp)), 0.0), axis=0, keepdims=True)
    offset = jnp.sum(jnp.where(c < r, jnp.broadcast_to(padded_row, (n_exp, n_exp)), 0.0), axis=1, keepdims=True)
    earlier = (lax.broadcasted_iota(I32, (tt, tt), 0) < lax.broadcasted_iota(I32, (tt, tt), 1)).astype(BF16)
    base = offset
    pos = []
    for k in range(k_):
        before = _dot(onehot[k].astype(BF16), earlier) + base
        pos.append(jnp.where(valid, jnp.sum(onehot[k] * before, axis=0, keepdims=True), -1.0))
        base = base + count_k[k]
    return pos


def _segment_copies(nch_ref, off_ref, gbase_ref, tile, n_exp, make_copy):
    total = 0
    for e in range(n_exp):
        n = nch_ref[tile * n_exp + e]
        lo = off_ref[tile * n_exp + e]
        gb = gbase_ref[tile * n_exp + e]

        def body(c, carry, lo=lo, gb=gb):
            make_copy(pl.multiple_of((lo + c * ROUTE_CHUNK) * SUBLANES, ROUTE_CHUNK * SUBLANES),
                      pl.multiple_of((gb + c * ROUTE_CHUNK) * SUBLANES, SUBLANES)).start()
            return carry

        lax.fori_loop(0, n, body, 0)
        total = total + n
    return total


def _dispatch_kernel(nch_ref, off_ref, gbase_ref, zlo_ref, zhi_ref, h_ref, e_ref, x_hbm, xt_ref, zero_ref, sem,
                     *, n_tok, n_exp, n_blocks, block_rows):
    i = pl.program_id(0)
    tt = h_ref.shape[0]
    npos = xt_ref.shape[0] // SUBLANES
    chunk = ROUTE_CHUNK * SUBLANES
    blk = block_rows * SUBLANES

    @pl.when(i == 0)
    def _():
        zero_ref[...] = jnp.zeros(zero_ref.shape, F32)

        def zero_block(b):
            return pltpu.make_async_copy(zero_ref, x_hbm.at[pl.ds(pl.multiple_of(b * blk, blk), blk), :], sem)

        def start(b, c):
            zero_block(b).start()
            return c

        def wait(b, c):
            zero_block(b).wait()
            return c

        for e in range(n_exp + 1):
            lax.fori_loop(zlo_ref[e], zhi_ref[e], start, 0)
        for e in range(n_exp + 1):
            lax.fori_loop(zlo_ref[e], zhi_ref[e], wait, 0)

    pos = _tile_plan(e_ref[...], i * tt, n_tok, n_exp)
    pio = lax.broadcasted_iota(I32, (npos, tt), 0).astype(F32)
    sel = functools.reduce(lambda a, b: a | b, [pio == p for p in pos])
    rows = _dot(jnp.where(sel, 1.0, 0.0).astype(BF16), h_ref[...].astype(BF16))
    for j in range(SUBLANES):
        xt_ref[pl.ds(j, npos, stride=SUBLANES), :] = rows[:, j * LANES:(j + 1) * LANES]

    def copy(src_row, dst_row):
        return pltpu.make_async_copy(xt_ref.at[pl.ds(src_row, chunk), :], x_hbm.at[pl.ds(dst_row, chunk), :], sem)

    total = _segment_copies(nch_ref, off_ref, gbase_ref, i, n_exp, copy)

    def wait_one(c, carry):
        copy(0, 0).wait()
        return carry

    lax.fori_loop(0, total, wait_one, 0)


def _dispatch(h_all, e_idx, tables, *, n_tok, n_exp, n_blocks):
    t_pad, d = h_all.shape
    tt = OUT_ROWS
    npos = TOP_K * tt + n_exp * ROUTE_CHUNK
    kern = functools.partial(_dispatch_kernel, n_tok=n_tok, n_exp=n_exp, n_blocks=n_blocks, block_rows=MOE_ROWS)
    grid_spec = pltpu.PrefetchScalarGridSpec(
        num_scalar_prefetch=5,
        grid=(t_pad // tt,),
        in_specs=[pl.BlockSpec((tt, d), lambda i, *_: (i, 0)),
                  pl.BlockSpec((TOP_K, tt), lambda i, *_: (0, i))],
        out_specs=pl.BlockSpec(memory_space=pl.ANY),
        scratch_shapes=[pltpu.VMEM((npos * SUBLANES, LANES), F32), pltpu.VMEM((MOE_ROWS * SUBLANES, LANES), F32),
                        pltpu.SemaphoreType.DMA],
    )
    return pl.pallas_call(
        kern,
        grid_spec=grid_spec,
        out_shape=jax.ShapeDtypeStruct((n_blocks * MOE_ROWS * SUBLANES, LANES), F32),
        compiler_params=_params("arbitrary"),
        name="moe_dispatch",
    )(tables["nch"], tables["off"], tables["gbase"], tables["zlo"], tables["zhi"], h_all, e_idx)


def _moe_kernel(be_ref, nu_ref, x_ref, wgu_ref, bgu_ref, wdn_ref, bdn_ref, y_ref, wgu_bf, wdn_bf, *, rows, de):
    i = pl.program_id(0)
    tile = SUBLANES

    @pl.when(i >= nu_ref[0])
    def _():
        y_ref[...] = jnp.zeros(y_ref.shape, F32)

    @pl.when(i < nu_ref[0])
    def _():
        @pl.when((i == 0) | (be_ref[i] != be_ref[jnp.maximum(i - 1, 0)]))
        def _():
            wgu_bf[...] = wgu_ref[...].astype(BF16)
            wdn_bf[...] = wdn_ref[...].astype(BF16)

        x = jnp.concatenate([x_ref[pl.ds(j, rows, stride=tile), :] for j in range(tile)], axis=1).astype(BF16)
        cw = MOE_HIDDEN_CHUNK
        n_chunks = de // cw

        def up(c):
            lo, hi = c * cw, (c + 1) * cw
            return (_dot(x, wgu_bf[:, lo:hi]) + bgu_ref[:, lo:hi],
                    _dot(x, wgu_bf[:, de + lo:de + hi]) + bgu_ref[:, de + lo:de + hi])

        pending = {c: up(c) for c in range(min(MOE_LOOKAHEAD, n_chunks))}
        y = bdn_ref[...]
        for c in range(n_chunks):
            if c + MOE_LOOKAHEAD < n_chunks:
                pending[c + MOE_LOOKAHEAD] = up(c + MOE_LOOKAHEAD)
            glu, lin = pending.pop(c)
            glu = jnp.minimum(glu, SWIGLU_LIMIT)
            lin = jnp.clip(lin, -SWIGLU_LIMIT, SWIGLU_LIMIT)
            act = glu * jax.nn.sigmoid(SWIGLU_ALPHA * glu) * (lin + 1.0)
            y = y + _dot(act.astype(BF16), wdn_bf[c * cw:(c + 1) * cw, :])
        for j in range(tile):
            y_ref[pl.ds(j, rows, stride=tile), :] = y[:, j * LANES:(j + 1) * LANES]


def _moe(x_tiles, block_expert, n_used, w_gu, b_gu, w_dn, b_dn):
    rows = MOE_ROWS
    n_blocks = block_expert.shape[0]
    n_exp, d, de2 = w_gu.shape
    de = de2 // 2
    kern = functools.partial(_moe_kernel, rows=rows, de=de)

    def used(i, nu):
        return jnp.minimum(i, nu[0] - 1)

    def expert(i, be, nu):
        return (be[used(i, nu)], 0, 0)

    grid_spec = pltpu.PrefetchScalarGridSpec(
        num_scalar_prefetch=2,
        grid=(n_blocks,),
        in_specs=[
            pl.BlockSpec((rows * SUBLANES, LANES), lambda i, be, nu: (used(i, nu), 0)),
            pl.BlockSpec((None, d, de2), expert),
            pl.BlockSpec((None, 1, de2), expert),
            pl.BlockSpec((None, de, d), expert),
            pl.BlockSpec((None, 1, d), expert),
        ],
        out_specs=pl.BlockSpec((rows * SUBLANES, LANES), lambda i, be, nu: (i, 0)),
        scratch_shapes=[pltpu.VMEM((d, de2), BF16), pltpu.VMEM((de, d), BF16)],
    )
    return pl.pallas_call(
        kern,
        grid_spec=grid_spec,
        out_shape=jax.ShapeDtypeStruct(x_tiles.shape, F32),
        compiler_params=_params("arbitrary"),
        name="moe_experts",
    )(block_expert, n_used, x_tiles, w_gu, b_gu.reshape(n_exp, 1, de2), w_dn, b_dn.reshape(n_exp, 1, d))


def _route_tables(e_idx, n_tok, n_exp):
    k_, t_pad = e_idx.shape
    tt = OUT_ROWS
    rows = MOE_ROWS
    tiles = t_pad // tt
    valid = jnp.arange(t_pad, dtype=I32) < n_tok
    e_m = jnp.where(valid[None, :], e_idx, -1).reshape(k_, tiles, tt, 1)
    cnt = jnp.sum((e_m == jnp.arange(n_exp, dtype=I32)).astype(I32), axis=(0, 2))
    cpad = ((cnt + ROUTE_CHUNK - 1) // ROUTE_CHUNK) * ROUTE_CHUNK
    off = jnp.cumsum(cpad, axis=1) - cpad
    total = jnp.sum(cnt, axis=0)
    region = ((total + ROUTE_CHUNK + rows - 1) // rows) * rows
    pend = jnp.cumsum(region)
    pstart = pend - region
    gbase = pstart[None, :] + jnp.cumsum(cnt, axis=0) - cnt
    n_blocks = -(-(k_ * n_tok + n_exp * (ROUTE_CHUNK + rows - 1)) // rows)
    block_start = jnp.arange(n_blocks, dtype=I32) * rows
    block_expert = jnp.minimum(jnp.sum((pend[None, :] <= block_start[:, None]).astype(I32), axis=1), n_exp - 1)
    n_used = (pend[-1] // rows).astype(I32)
    zlo = jnp.concatenate([(pstart + total) // rows, n_used[None]]).astype(I32)
    zhi = jnp.concatenate([pend // rows, jnp.full((1,), n_blocks, I32)]).astype(I32)
    tables = dict(nch=(cpad // ROUTE_CHUNK).reshape(-1).astype(I32), off=off.reshape(-1).astype(I32),
                  gbase=gbase.reshape(-1).astype(I32), zlo=zlo, zhi=zhi)
    return tables, block_expert.astype(I32), n_used.reshape(1), n_blocks


def _final_kernel(nch_ref, off_ref, gbase_ref, e_ref, gt_ref, h_ref, g_ref, b_ref, y_hbm, op_ref, os_ref,
                  yt_ref, sem, *, n_tok, n_exp, n_prompt, alpha):
    i = pl.program_id(0)
    tt = h_ref.shape[0]
    npos = yt_ref.shape[0] // SUBLANES
    chunk = ROUTE_CHUNK * SUBLANES

    @pl.when(i == 0)
    def _():
        yt_ref[...] = jnp.zeros(yt_ref.shape, F32)

    def copy(dst_row, src_row):
        return pltpu.make_async_copy(y_hbm.at[pl.ds(src_row, chunk), :], yt_ref.at[pl.ds(dst_row, chunk), :], sem)

    total = _segment_copies(nch_ref, off_ref, gbase_ref, i, n_exp, copy)

    pos = _tile_plan(e_ref[...], i * tt, n_tok, n_exp)
    eye = lax.broadcasted_iota(I32, (tt, tt), 0) == lax.broadcasted_iota(I32, (tt, tt), 1)

    def column(row_vec):
        return jnp.sum(jnp.where(eye, jnp.broadcast_to(row_vec, (tt, tt)), 0.0), axis=1, keepdims=True)

    pio = lax.broadcasted_iota(I32, (tt, npos), 1).astype(F32)
    w = None
    for k, p in enumerate(pos):
        wk = jnp.where(pio == column(p), column(gt_ref[k:k + 1, :]), 0.0)
        w = wk if w is None else w + wk

    def wait_one(c, carry):
        copy(0, 0).wait()
        return carry

    lax.fori_loop(0, total, wait_one, 0)
    ys = jnp.concatenate([yt_ref[pl.ds(j, npos, stride=SUBLANES), :] for j in range(SUBLANES)], axis=1)
    f = _dot(w.astype(BF16), ys.astype(BF16))
    out = _layer_norm(alpha * h_ref[...] + f, g_ref[...], b_ref[...])

    @pl.when(i < n_prompt)
    def _():
        op_ref[...] = out

    @pl.when(i == n_prompt)
    def _():
        os_ref[...] = out[:os_ref.shape[0], :]


def _final(y_tiles, e_idx, gates, h_all, g, b, tables, *, tp, ts, n_exp, alpha):
    tt = OUT_ROWS
    t_pad, d = h_all.shape
    n_prompt = tp // tt
    npos = TOP_K * tt + n_exp * ROUTE_CHUNK
    const = lambda i, *_: (0, 0)
    kern = functools.partial(_final_kernel, n_tok=tp + ts, n_exp=n_exp, n_prompt=n_prompt, alpha=alpha)
    grid_spec = pltpu.PrefetchScalarGridSpec(
        num_scalar_prefetch=3,
        grid=(t_pad // tt,),
        in_specs=[pl.BlockSpec((TOP_K, tt), lambda i, *_: (0, i)),
                  pl.BlockSpec((TOP_K, tt), lambda i, *_: (0, i)),
                  pl.BlockSpec((tt, d), lambda i, *_: (i, 0)),
                  pl.BlockSpec((1, d), const), pl.BlockSpec((1, d), const),
                  pl.BlockSpec(memory_space=pl.ANY)],
        out_specs=[pl.BlockSpec((tt, d), lambda i, *_: (jnp.minimum(i, n_prompt - 1), 0)),
                   pl.BlockSpec((ts, d), const)],
        scratch_shapes=[pltpu.VMEM((npos * SUBLANES, LANES), F32), pltpu.SemaphoreType.DMA],
    )
    return pl.pallas_call(
        kern,
        grid_spec=grid_spec,
        out_shape=[jax.ShapeDtypeStruct((tp, d), F32), jax.ShapeDtypeStruct((ts, d), F32)],
        compiler_params=_params("arbitrary"),
        name="combine_ln",
    )(tables["nch"], tables["off"], tables["gbase"], e_idx, gates, h_all, g, b, y_tiles)


def _pad_rows(a, rows):
    return jnp.pad(a, ((0, rows - a.shape[0]), (0, 0)))


def kernel(x_prompt, x_sample, cache_k, cache_v, state_ssm_re, state_ssm_im, page_table, w_in, w_out, ssm_a_re, ssm_a_im, ssm_log_dt, ssm_b_re, ssm_b_im, ssm_c_re, ssm_c_im, ssm_d, w_glu, b_glu, lambda_q1, lambda_k1, lambda_q2, lambda_k2, subln_g, ln1_g, ln1_b, w_router, b_router, w_gate_up, b_gate_up, w_down, b_down, ln2_g, ln2_b):
    depth = w_in.shape[0]
    bp, lp, d = x_prompt.shape
    bs, ls, _ = x_sample.shape
    n_phys, page, heads, hd = cache_k.shape[1:]
    assert ls == 1 and hd == LANES and d == SUBLANES * LANES
    qk = hd // 2
    aw = heads * hd
    groups, n_state = ssm_a_re.shape[1:]
    sw = groups * ssm_b_re.shape[-1]
    gp = groups * n_state
    n_exp = w_router.shape[2]
    past_len = page_table.shape[1] * page
    alpha = (2 * depth) ** 0.25
    tp = bp * lp
    n_tok = tp + bs

    cos_p, sin_p = _rope_tables(jnp.arange(lp, dtype=I32), qk // 2)
    cos_s, sin_s = _rope_tables(jnp.full((bs,), past_len, I32), qk // 2)
    ck = cache_k.reshape(depth * n_phys * page * heads, hd)
    cv = cache_v.reshape(depth * n_phys * page * heads, hd)

    hp = x_prompt
    hs = x_sample.reshape(1, bs, d)
    outs = [[] for _ in range(8)]
    for l in range(depth):
        lam_init = 0.8 - 0.6 * math.exp(-0.3 * l)
        w_in_bf = w_in[l].astype(BF16)
        w_out_bf = w_out[l].astype(BF16)
        w_glu_bf = w_glu[l].astype(BF16)
        bg = b_glu[l].reshape(1, 2 * sw)
        lam_p = [p[l].reshape(1, qk).astype(F32) for p in (lambda_q1, lambda_k1, lambda_q2, lambda_k2)]
        prm = _ssm_params(ssm_a_re[l], ssm_a_im[l], ssm_log_dt[l], ssm_b_re[l], ssm_b_im[l],
                          ssm_c_re[l], ssm_c_im[l], ssm_d[l])

        tt = min(PROJ_ROWS // bp, lp)
        u_p, q_p, k_p, v_p = _project(hp, w_in_bf, cos_p, sin_p, tt=tt, sw=sw, aw=aw, half=qk // 2)
        zero = jnp.zeros((bp, gp), F32)
        ssm_p, sre_p, sim_p = _ssm(u_p, zero, zero, prm, nb=bp, steps=min(SSM_STEPS, lp), w_glu_bf=w_glu_bf,
                                   b_glu=bg)
        att_p = _attn_prompt(q_p, k_p, v_p, lam_p, subln_g[l].reshape(hd, 1), heads=heads, lam_init=lam_init)

        u_s, q_s, k_s, v_s = _project(hs, w_in_bf, cos_s, sin_s, tt=bs, sw=sw, aw=aw, half=qk // 2)
        ssm_s, sre_s, sim_s = _ssm(u_s, state_ssm_re[l].reshape(bs, gp), state_ssm_im[l].reshape(bs, gp), prm,
                                   nb=bs, steps=1, w_glu_bf=w_glu_bf, b_glu=bg)
        att_s = _attn_sample(q_s.reshape(bs, aw), k_s.reshape(bs, aw), v_s.reshape(bs, aw), ck, cv,
                             page_table + l * n_phys, lam_p, subln_g[l].reshape(1, hd), page=page, heads=heads,
                             lam_init=lam_init)

        wr = w_router[l].T.astype(F32)
        wr_hi = wr.astype(BF16)
        wr_lo = (wr - wr_hi.astype(F32)).astype(BF16)
        h_all, e_idx, gates = _out_route(
            hp.reshape(tp, d), ssm_p.reshape(tp, sw), att_p, _pad_rows(hs.reshape(bs, d), OUT_ROWS),
            _pad_rows(ssm_s, OUT_ROWS), _pad_rows(att_s, OUT_ROWS), w_out_bf, ln1_g[l].reshape(1, d),
            ln1_b[l].reshape(1, d), wr_hi, wr_lo, b_router[l].reshape(n_exp, 1), alpha=alpha)
        tables, block_expert, n_used, n_blocks = _route_tables(e_idx, n_tok, n_exp)
        x_tiles = _dispatch(h_all, e_idx, tables, n_tok=n_tok, n_exp=n_exp, n_blocks=n_blocks)
        y_tiles = _moe(x_tiles, block_expert, n_used, w_gate_up[l], b_gate_up[l], w_down[l], b_down[l])
        y_p, y_s = _final(y_tiles, e_idx, gates, h_all, ln2_g[l].reshape(1, d), ln2_b[l].reshape(1, d), tables,
                          tp=tp, ts=bs, n_exp=n_exp, alpha=alpha)
        hp = y_p.reshape(bp, lp, d)
        hs = y_s.reshape(1, bs, d)

        for lst, val in zip(outs, (
                k_p.reshape(bp, lp, heads, hd), v_p.reshape(bp, lp, heads, hd),
                sre_p.reshape(bp, groups, n_state), sim_p.reshape(bp, groups, n_state),
                k_s.reshape(bs, ls, heads, hd), v_s.reshape(bs, ls, heads, hd),
                sre_s.reshape(bs, groups, n_state), sim_s.reshape(bs, groups, n_state))):
            lst.append(val)

    return (hp, hs.reshape(bs, ls, d)) + tuple(jnp.stack(o) for o in outs)
```

```python
import functools
import math

import jax
import jax.numpy as jnp
from jax import lax
from jax.experimental import pallas as pl
from jax.experimental.pallas import tpu as pltpu

F32 = jnp.float32
BF16 = jnp.bfloat16
I32 = jnp.int32

TOP_K = 4
ROPE_THETA = 10000.0
SWIGLU_ALPHA = 1.702
SWIGLU_LIMIT = 7.0
LN_EPS = 1e-5
RMS_EPS = 1e-5
SSM_A_RE_MAX = -1e-4
GELU_C0 = math.sqrt(2.0 / math.pi)
GELU_C1 = 0.044715
NEG_BIG = -1e30
LOG2E = math.log2(math.e)

LANES = 128
SUBLANES = 8
MXU_DIM = 256
VMEM_LIMIT_BYTES = 56 * 1024 * 1024

PROJ_ROWS = 512
SSM_STEPS = 64
ATT_ROWS = 256
ATT_LOOKAHEAD = 4
MOE_HIDDEN_CHUNK = 256
MOE_LOOKAHEAD = 2
ROUTE_CHUNK = 8
ROUTE_ROWS = 256
OUT_ROWS = 512
MOE_ROWS = 256
SCAN_LANES = 512


def _params(*sem):
    return pltpu.CompilerParams(dimension_semantics=sem, vmem_limit_bytes=VMEM_LIMIT_BYTES)


def _dot(a, b):
    return jnp.dot(a, b, preferred_element_type=F32)


def _dot_nt(a, b):
    return lax.dot_general(a, b, (((1,), (1,)), ((), ())), preferred_element_type=F32)


def _proj_kernel(x_ref, w_ref, cos_ref, sin_ref, u_ref, q_ref, k_ref, v_ref, *, nb, tt, sw, aw, half):
    d = x_ref.shape[-1]
    heads = aw // LANES
    acc = _dot(x_ref[...].reshape(nb * tt, d).astype(BF16), w_ref[...])
    c = cos_ref[...]
    s = sin_ref[...]
    lane = lax.broadcasted_iota(I32, c.shape, 1)
    first = (lane % (2 * half)) < half

    def rope(blk):
        swapped = jnp.where(first, pltpu.roll(blk, LANES - half, 1), pltpu.roll(blk, half, 1))
        return blk * c + swapped * s

    for b in range(nb):
        rows = slice(b * tt, (b + 1) * tt)
        for j in range(sw // LANES):
            u_ref[j, pl.ds(b, tt, stride=nb), :] = acc[rows, j * LANES:(j + 1) * LANES]
        q_ref[b] = jnp.concatenate(
            [rope(acc[rows, sw + j * LANES:sw + (j + 1) * LANES]) for j in range(heads)], axis=1)
        for j in range(heads):
            lo = sw + aw + j * LANES
            k_ref[b, pl.ds(j, tt, stride=heads), :] = rope(acc[rows, lo:lo + LANES])
            lo = sw + 2 * aw + j * LANES
            v_ref[b, pl.ds(j, tt, stride=heads), :] = acc[rows, lo:lo + LANES]


def _project(x3, w_bf, cos_t, sin_t, *, tt, sw, aw, half):
    nb, seq, d = x3.shape
    heads = aw // LANES
    kern = functools.partial(_proj_kernel, nb=nb, tt=tt, sw=sw, aw=aw, half=half)
    return pl.pallas_call(
        kern,
        grid=(seq // tt,),
        in_specs=[
            pl.BlockSpec((nb, tt, d), lambda i: (0, i, 0)),
            pl.BlockSpec(w_bf.shape, lambda i: (0, 0)),
            pl.BlockSpec((tt, LANES), lambda i: (i, 0)),
            pl.BlockSpec((tt, LANES), lambda i: (i, 0)),
        ],
        out_specs=[
            pl.BlockSpec((sw // LANES, tt * nb, LANES), lambda i: (0, i, 0)),
            pl.BlockSpec((nb, tt, aw), lambda i: (0, i, 0)),
            pl.BlockSpec((nb, tt * heads, LANES), lambda i: (0, i, 0)),
            pl.BlockSpec((nb, tt * heads, LANES), lambda i: (0, i, 0)),
        ],
        out_shape=[
            jax.ShapeDtypeStruct((sw // LANES, seq * nb, LANES), F32),
            jax.ShapeDtypeStruct((nb, seq, aw), F32),
            jax.ShapeDtypeStruct((nb, seq * heads, LANES), F32),
            jax.ShapeDtypeStruct((nb, seq * heads, LANES), F32),
        ],
        compiler_params=_params("arbitrary"),
        name="proj_rope",
    )(x3, w_bf, cos_t, sin_t)


def _rope_tables(pos, half):
    inv = ROPE_THETA ** (-jnp.arange(half, dtype=F32) / half)
    ang = pos.astype(F32)[:, None] * inv[None, :]
    reps = LANES // half
    cos = jnp.tile(jnp.cos(ang), (1, reps))
    sin = jnp.sin(ang)
    sin = jnp.tile(jnp.concatenate([-sin, sin], axis=1), (1, reps // 2))
    return cos, sin


def _ssm_kernel(u_ref, s0r_ref, s0i_ref, ar_ref, ai_ref, wb_ref, wc_ref, d_ref, wg_ref, bg_ref,
                o_ref, sr_ref, si_ref, bu_ref, st_ref, ot_ref, *, nb, steps, gp, sw):
    i = pl.program_id(0)
    rows = nb * steps

    @pl.when(i == 0)
    def _():
        st_ref[:, :gp] = s0r_ref[...]
        st_ref[:, gp:] = s0i_ref[...]

    u = jnp.concatenate([u_ref[j] for j in range(sw // LANES)], axis=1)
    ub = u.astype(BF16)
    n_tiles = 2 * gp // MXU_DIM
    per_half = gp // MXU_DIM
    in_tiles = sw // MXU_DIM
    for j in range(n_tiles):
        kt = ((j % per_half) * in_tiles) // per_half
        bu_ref[:, j * MXU_DIM:(j + 1) * MXU_DIM] = _dot(ub[:, kt * MXU_DIM:(kt + 1) * MXU_DIM], wb_ref[j])

    for c in range(gp // SCAN_LANES):
        lo_r = c * SCAN_LANES
        lo_i = gp + c * SCAN_LANES
        ar = jnp.broadcast_to(ar_ref[:, lo_r:lo_r + SCAN_LANES], (nb, SCAN_LANES))
        ai = jnp.broadcast_to(ai_ref[:, lo_r:lo_r + SCAN_LANES], (nb, SCAN_LANES))

        def step(t, carry, lo_r=lo_r, lo_i=lo_i, ar=ar, ai=ai):
            sr, si = carry
            r0 = pl.multiple_of(t * nb, nb)
            br = bu_ref[pl.ds(r0, nb), lo_r:lo_r + SCAN_LANES]
            bi = bu_ref[pl.ds(r0, nb), lo_i:lo_i + SCAN_LANES]
            nr = ar * sr - ai * si + br
            ni = ar * si + ai * sr + bi
            bu_ref[pl.ds(r0, nb), lo_r:lo_r + SCAN_LANES] = nr
            bu_ref[pl.ds(r0, nb), lo_i:lo_i + SCAN_LANES] = ni
            return nr, ni

        sr0 = st_ref[:, lo_r:lo_r + SCAN_LANES]
        si0 = st_ref[:, lo_i:lo_i + SCAN_LANES]
        if steps == 1:
            sr1, si1 = step(0, (sr0, si0))
        else:
            sr1, si1 = lax.fori_loop(0, steps, step, (sr0, si0))
        st_ref[:, lo_r:lo_r + SCAN_LANES] = sr1
        st_ref[:, lo_i:lo_i + SCAN_LANES] = si1

    sr_ref[...] = st_ref[:, :gp]
    si_ref[...] = st_ref[:, gp:]

    out_tiles = sw // MXU_DIM
    kw = gp // out_tiles
    ys = []
    for nt in range(out_tiles):
        s_re = bu_ref[:, nt * kw:(nt + 1) * kw].astype(BF16)
        s_im = bu_ref[:, gp + nt * kw:gp + (nt + 1) * kw].astype(BF16)
        ys.append(_dot(s_re, wc_ref[nt, 0]) + _dot(s_im, wc_ref[nt, 1]))
    y = jnp.concatenate(ys, axis=1) + d_ref[...] * u
    y = 0.5 * y * (1.0 + jnp.tanh(GELU_C0 * (y + GELU_C1 * (y * y * y))))
    z = _dot(y.astype(BF16), wg_ref[...]) + bg_ref[...]
    out = z[:, :sw] * jax.nn.sigmoid(z[:, sw:])
    if steps == 1:
        o_ref[...] = out
    else:
        for j in range(sw // LANES):
            ot_ref[j] = out[:, j * LANES:(j + 1) * LANES]
        for b in range(nb):
            for j in range(sw // LANES):
                o_ref[b, :, j * LANES:(j + 1) * LANES] = ot_ref[j, pl.ds(b, steps, stride=nb), :]


def _ssm(u_tb, s0r, s0i, prm, *, nb, steps, w_glu_bf, b_glu):
    n_slabs, n_rows, _ = u_tb.shape
    sw = n_slabs * LANES
    seq = n_rows // nb
    gp = prm["a_re"].shape[1]
    rows = nb * steps
    grid = (seq // steps,)
    const2 = lambda i: (0, 0)
    kern = functools.partial(_ssm_kernel, nb=nb, steps=steps, gp=gp, sw=sw)
    if steps == 1:
        o_spec = pl.BlockSpec((nb, sw), const2)
        o_shape = jax.ShapeDtypeStruct((nb, sw), F32)
    else:
        o_spec = pl.BlockSpec((nb, steps, sw), lambda i: (0, i, 0))
        o_shape = jax.ShapeDtypeStruct((nb, seq, sw), F32)
    return pl.pallas_call(
        kern,
        grid=grid,
        in_specs=[
            pl.BlockSpec((n_slabs, rows, LANES), lambda i: (0, i, 0)),
            pl.BlockSpec((nb, gp), const2),
            pl.BlockSpec((nb, gp), const2),
            pl.BlockSpec((1, gp), const2),
            pl.BlockSpec((1, gp), const2),
            pl.BlockSpec(prm["wb"].shape, lambda i: (0, 0, 0)),
            pl.BlockSpec(prm["wc"].shape, lambda i: (0, 0, 0, 0)),
            pl.BlockSpec((1, sw), const2),
            pl.BlockSpec(w_glu_bf.shape, const2),
            pl.BlockSpec((1, 2 * sw), const2),
        ],
        out_specs=[o_spec, pl.BlockSpec((nb, gp), const2), pl.BlockSpec((nb, gp), const2)],
        out_shape=[o_shape, jax.ShapeDtypeStruct((nb, gp), F32), jax.ShapeDtypeStruct((nb, gp), F32)],
        scratch_shapes=[
            pltpu.VMEM((rows, 2 * gp), F32),
            pltpu.VMEM((nb, 2 * gp), F32),
            pltpu.VMEM((sw // LANES, rows, LANES), F32),
        ],
        compiler_params=_params("arbitrary"),
        name="ssm_mixer",
    )(u_tb, s0r, s0i, prm["a_re"], prm["a_im"], prm["wb"], prm["wc"], prm["d"], w_glu_bf, b_glu)


def _ssm_params(a_re, a_im, log_dt, b_re, b_im, c_re, c_im, d):
    g_, p_ = a_re.shape
    h_ = b_re.shape[-1]
    gp = g_ * p_
    sw = g_ * h_
    a_re = jnp.minimum(a_re.astype(F32), SSM_A_RE_MAX)
    a_im = a_im.astype(F32)
    dt = jnp.exp(log_dt.astype(F32))[:, None]
    mag = jnp.exp(a_re * dt)
    lb_re = mag * jnp.cos(a_im * dt)
    lb_im = mag * jnp.sin(a_im * dt)
    den = jnp.square(a_re) + jnp.square(a_im)
    nr, ni = lb_re - 1.0, lb_im
    f_re = (nr * a_re + ni * a_im) / den
    f_im = (ni * a_re - nr * a_im) / den
    b_re = b_re.astype(F32)
    b_im = b_im.astype(F32)
    bb_re = f_re[..., None] * b_re - f_im[..., None] * b_im
    bb_im = f_re[..., None] * b_im + f_im[..., None] * b_re
    eye = jnp.eye(g_, dtype=F32)
    w_in_re = jnp.einsum("gph,gk->ghkp", bb_re, eye).reshape(sw, gp)
    w_in_im = jnp.einsum("gph,gk->ghkp", bb_im, eye).reshape(sw, gp)
    w_in = jnp.concatenate([w_in_re, w_in_im], axis=1)
    per_half = gp // MXU_DIM
    in_tiles = sw // MXU_DIM
    tiles = []
    for j in range(2 * per_half):
        kt = ((j % per_half) * in_tiles) // per_half
        tiles.append(w_in[kt * MXU_DIM:(kt + 1) * MXU_DIM, j * MXU_DIM:(j + 1) * MXU_DIM])
    wb = jnp.stack(tiles).astype(BF16)
    w_out_re = jnp.einsum("ghp,gk->gpkh", c_re.astype(F32), eye).reshape(gp, sw)
    w_out_im = -jnp.einsum("ghp,gk->gpkh", c_im.astype(F32), eye).reshape(gp, sw)
    out_tiles = sw // MXU_DIM
    kw = gp // out_tiles
    wc = jnp.stack([
        jnp.stack([w_out_re[nt * kw:(nt + 1) * kw, nt * MXU_DIM:(nt + 1) * MXU_DIM],
                   w_out_im[nt * kw:(nt + 1) * kw, nt * MXU_DIM:(nt + 1) * MXU_DIM]])
        for nt in range(out_tiles)]).astype(BF16)
    return dict(a_re=lb_re.reshape(1, gp), a_im=lb_im.reshape(1, gp), wb=wb, wc=wc,
                d=d.astype(F32).reshape(1, sw))


def _diff_lambda(lq1, lk1, lq2, lk2, lam_init):
    l1 = jnp.exp(jnp.sum(lq1 * lk1, axis=1, keepdims=True))
    l2 = jnp.exp(jnp.sum(lq2 * lk2, axis=1, keepdims=True))
    return l1 - l2 + lam_init


def _head_norm(o, g, lam_init):
    ms = jnp.mean(o * o, axis=-1, keepdims=True)
    return o * lax.rsqrt(ms + RMS_EPS) * g * (1.0 - lam_init)


def _attn_prompt_kernel(q_ref, k_ref, v_ref, lq1_ref, lk1_ref, lq2_ref, lk2_ref, g_ref, o_ref,
                        kb_ref, vt_ref, qs_ref, acc_ref, *, tq, qk, heads, lam_init, scale):
    h = pl.program_id(1)
    nq = vt_ref.shape[0]
    seq = nq * tq
    kb_ref[...] = k_ref[pl.ds(h, seq, stride=heads), :].astype(BF16)
    for jb in range(nq):
        vt_ref[jb] = v_ref[pl.ds(jb * tq * heads + h, tq, stride=heads), :].T.astype(BF16)
    q = q_ref[...] * (scale * LOG2E)
    lane = lax.broadcasted_iota(I32, q.shape, 1)
    qs_ref[0] = jnp.where(lane < qk, q, 0.0).astype(BF16)
    qs_ref[1] = jnp.where(lane >= qk, q, 0.0).astype(BF16)
    acc_ref[...] = jnp.zeros(acc_ref.shape, F32)
    key = lax.broadcasted_iota(I32, (tq, tq), 0)
    qry = lax.broadcasted_iota(I32, (tq, tq), 1)
    keep = key <= qry
    m = [[jnp.full((1, tq), NEG_BIG, F32)] * 2 for _ in range(nq)]
    l = [[jnp.zeros((1, tq), F32)] * 2 for _ in range(nq)]
    tasks = [(j, i, c) for j in range(nq) for i in range(j, nq) for c in range(2)]

    def scores(t):
        j, i, c = tasks[t]
        st = _dot_nt(kb_ref[j * tq:(j + 1) * tq, :], qs_ref[c, i * tq:(i + 1) * tq, :])
        return jnp.where(keep, st, NEG_BIG) if i == j else st

    pending = {t: scores(t) for t in range(min(ATT_LOOKAHEAD, len(tasks)))}
    for t, (j, i, c) in enumerate(tasks):
        if t + ATT_LOOKAHEAD < len(tasks):
            pending[t + ATT_LOOKAHEAD] = scores(t + ATT_LOOKAHEAD)
        st = pending.pop(t)
        m_new = jnp.maximum(m[i][c], jnp.max(st, axis=0, keepdims=True))
        alpha = jnp.exp2(m[i][c] - m_new)
        p = jnp.exp2(st - m_new)
        acc_ref[i, c] = alpha * acc_ref[i, c] + _dot(vt_ref[j], p.astype(BF16))
        l[i][c] = alpha * l[i][c] + jnp.sum(p, axis=0, keepdims=True)
        m[i][c] = m_new
    lam = _diff_lambda(lq1_ref[...], lk1_ref[...], lq2_ref[...], lk2_ref[...], lam_init)
    for i in range(nq):
        ot = acc_ref[i, 0] * (1.0 / l[i][0]) - lam * (acc_ref[i, 1] * (1.0 / l[i][1]))
        ms = jnp.mean(ot * ot, axis=0, keepdims=True)
        ot = ot * lax.rsqrt(ms + RMS_EPS) * g_ref[...] * (1.0 - lam_init)
        o_ref[i * tq:(i + 1) * tq, :] = ot.T


def _attn_prompt(q, k, v, lam_p, g_col, *, heads, lam_init):
    nb, seq, aw = q.shape
    hd = aw // heads
    qk = hd // 2
    tq = min(ATT_ROWS, seq)
    nq = seq // tq
    kern = functools.partial(_attn_prompt_kernel, tq=tq, qk=qk, heads=heads, lam_init=lam_init, scale=qk ** -0.5)
    small = pl.BlockSpec((1, qk), lambda b, h: (0, 0))
    return pl.pallas_call(
        kern,
        grid=(nb, heads),
        in_specs=[
            pl.BlockSpec((None, seq, hd), lambda b, h: (b, 0, h)),
            pl.BlockSpec((None, seq * heads, hd), lambda b, h: (b, 0, 0)),
            pl.BlockSpec((None, seq * heads, hd), lambda b, h: (b, 0, 0)),
            small, small, small, small,
            pl.BlockSpec((hd, 1), lambda b, h: (0, 0)),
        ],
        out_specs=pl.BlockSpec((seq, hd), lambda b, h: (b, h)),
        out_shape=jax.ShapeDtypeStruct((nb * seq, aw), F32),
        scratch_shapes=[pltpu.VMEM((seq, hd), BF16), pltpu.VMEM((nq, hd, tq), BF16),
                        pltpu.VMEM((2, seq, hd), BF16), pltpu.VMEM((nq, 2, hd, tq), F32)],
        compiler_params=_params("arbitrary", "arbitrary"),
        name="diff_attn_prompt",
    )(q, k, v, *lam_p, g_col)


def _attn_sample_kernel(pt_ref, q_ref, ks_ref, vs_ref, lq1_ref, lk1_ref, lq2_ref, lk2_ref, g_ref, *rest,
                        n_pages, heads, qk, lam_init, scale):
    k_pages = rest[:n_pages]
    v_pages = rest[n_pages:2 * n_pages]
    o_ref = rest[2 * n_pages]
    del pt_ref
    hd = 2 * qk
    aw = heads * hd
    n_maps = 2 * heads
    q = q_ref[...] * scale
    row = lax.broadcasted_iota(I32, (n_maps, aw), 0)
    lane = lax.broadcasted_iota(I32, (n_maps, aw), 1)
    own = (lane // hd == row % heads) & ((lane // qk) % 2 == row // heads)
    qm = jnp.where(own, jnp.broadcast_to(q, (n_maps, aw)), 0.0)
    qmb = qm.astype(BF16)
    page = k_pages[0].shape[0] // heads

    def page_rows(ref):
        return jnp.concatenate([ref[pl.ds(h, page, stride=heads), :] for h in range(heads)], axis=1).astype(BF16)

    s_pages = [_dot_nt(qmb, page_rows(kp)) for kp in k_pages]
    s_self = jnp.sum(qm * ks_ref[...], axis=1, keepdims=True)
    m = s_self
    for s in s_pages:
        m = jnp.maximum(m, jnp.max(s, axis=1, keepdims=True))
    p_self = jnp.exp(s_self - m)
    p_pages = [jnp.exp(s - m) for s in s_pages]
    l = p_self
    for p in p_pages:
        l = l + jnp.sum(p, axis=1, keepdims=True)
    inv = 1.0 / l
    lam = _diff_lambda(lq1_ref[...], lk1_ref[...], lq2_ref[...], lk2_ref[...], lam_init)

    def mix(p):
        pn = p * inv
        return pn - lam * pltpu.roll(pn, heads, 0)

    acc = jnp.broadcast_to(mix(p_self), (n_maps, aw)) * vs_ref[...]
    for p, vp in zip(p_pages, v_pages):
        acc = acc + _dot(mix(p).astype(BF16), page_rows(vp))
    pick = (row < heads) & (lane // hd == row)
    o = jnp.sum(jnp.where(pick, acc, 0.0), axis=0, keepdims=True)
    g = g_ref[...]
    for h in range(heads):
        o_ref[:, h * hd:(h + 1) * hd] = _head_norm(o[:, h * hd:(h + 1) * hd], g, lam_init)


def _attn_sample(q, k_new, v_new, cache_k, cache_v, page_ids, lam_p, g, *, page, heads, lam_init):
    bs, aw = q.shape
    n_pages = page_ids.shape[1]
    hd = aw // heads
    qk = hd // 2
    kern = functools.partial(_attn_sample_kernel, n_pages=n_pages, heads=heads, qk=qk, lam_init=lam_init,
                             scale=qk ** -0.5)
    row_spec = pl.BlockSpec((None, 1, aw), lambda b, pt: (b, 0, 0))
    small = pl.BlockSpec((1, qk), lambda b, pt: (0, 0))

    def page_spec(j):
        return pl.BlockSpec((page * heads, hd), lambda b, pt, j=j: (pt[b * n_pages + j], 0))

    grid_spec = pltpu.PrefetchScalarGridSpec(
        num_scalar_prefetch=1,
        grid=(bs,),
        in_specs=[row_spec, row_spec, row_spec, small, small, small, small,
                  pl.BlockSpec((1, hd), lambda b, pt: (0, 0))]
                 + [page_spec(j) for j in range(n_pages)] * 2,
        out_specs=row_spec,
    )
    out = pl.pallas_call(
        kern,
        grid_spec=grid_spec,
        out_shape=jax.ShapeDtypeStruct((bs, 1, aw), F32),
        compiler_params=_params("arbitrary"),
        name="diff_attn_sample",
    )(page_ids.reshape(-1).astype(I32), q.reshape(bs, 1, aw), k_new.reshape(bs, 1, aw),
      v_new.reshape(bs, 1, aw), *lam_p, g, *([cache_k] * n_pages), *([cache_v] * n_pages))
    return out.reshape(bs, aw)


def _layer_norm(z, g, b):
    mu = jnp.mean(z, axis=-1, keepdims=True)
    zc = z - mu
    var = jnp.mean(zc * zc, axis=-1, keepdims=True)
    return zc * lax.rsqrt(var + LN_EPS) * g + b


def _out_kernel(xp_ref, sp_ref, ap_ref, xs_ref, ss_ref, as_ref, wo_ref, g_ref, b_ref, wrh_ref, wrl_ref,
                br_ref, h_ref, e_ref, gt_ref, *, n_prompt, sw, alpha, top_k):
    i = pl.program_id(0)
    is_s = i == n_prompt
    x = jnp.where(is_s, xs_ref[...], xp_ref[...])
    s = jnp.where(is_s, ss_ref[...], sp_ref[...]).astype(BF16)
    a = jnp.where(is_s, as_ref[...], ap_ref[...]).astype(BF16)
    mix = _dot(s, wo_ref[:sw, :]) + _dot(a, wo_ref[sw:, :])
    h = _layer_norm(alpha * x + mix, g_ref[...], b_ref[...])
    h_ref[...] = h
    h_hi = h.astype(BF16)
    h_lo = (h - h_hi.astype(F32)).astype(BF16)
    lg = (_dot_nt(wrh_ref[...], h_hi) + _dot_nt(wrl_ref[...], h_hi) + _dot_nt(wrh_ref[...], h_lo)
          + br_ref[...])
    eio = lax.broadcasted_iota(I32, lg.shape, 0)
    n_exp = lg.shape[0]
    vals, idxs = [], []
    for _ in range(top_k):
        m = jnp.max(lg, axis=0, keepdims=True)
        idx = jnp.min(jnp.where(lg == m, eio, n_exp), axis=0, keepdims=True)
        vals.append(m)
        idxs.append(idx)
        lg = jnp.where(eio == idx, NEG_BIG, lg)
    tv = jnp.concatenate(vals, axis=0)
    pe = jnp.exp(tv - tv[0:1, :])
    gt_ref[...] = pe / jnp.sum(pe, axis=0, keepdims=True)
    e_ref[...] = jnp.concatenate(idxs, axis=0)


def _out_route(xp, sp, ap, xs, ss, as_, w_out_bf, g, b, wr_hi, wr_lo, br, *, alpha):
    tp, d = xp.shape
    sw = sp.shape[1]
    rows = OUT_ROWS
    n_prompt = tp // rows
    total = tp + rows
    n_exp = wr_hi.shape[0]
    pidx = lambda i: (jnp.minimum(i, n_prompt - 1), 0)
    const = lambda i: (0, 0)
    kern = functools.partial(_out_kernel, n_prompt=n_prompt, sw=sw, alpha=alpha, top_k=TOP_K)
    return pl.pallas_call(
        kern,
        grid=(n_prompt + 1,),
        in_specs=[
            pl.BlockSpec((rows, d), pidx), pl.BlockSpec((rows, sw), pidx), pl.BlockSpec((rows, d - sw), pidx),
            pl.BlockSpec((rows, d), const), pl.BlockSpec((rows, sw), const), pl.BlockSpec((rows, d - sw), const),
            pl.BlockSpec(w_out_bf.shape, const), pl.BlockSpec((1, d), const), pl.BlockSpec((1, d), const),
            pl.BlockSpec((n_exp, d), const), pl.BlockSpec((n_exp, d), const), pl.BlockSpec((n_exp, 1), const),
        ],
        out_specs=[
            pl.BlockSpec((rows, d), lambda i: (i, 0)),
            pl.BlockSpec((TOP_K, rows), lambda i: (0, i)),
            pl.BlockSpec((TOP_K, rows), lambda i: (0, i)),
        ],
        out_shape=[
            jax.ShapeDtypeStruct((total, d), F32),
            jax.ShapeDtypeStruct((TOP_K, total), I32),
            jax.ShapeDtypeStruct((TOP_K, total), F32),
        ],
        compiler_params=_params("arbitrary"),
        name="out_proj_route",
    )(xp, sp, ap, xs, ss, as_, w_out_bf, g, b, wr_hi, wr_lo, br)


def _tile_plan(e_tile, tile_base, n_tok, n_exp):
    k_, tt = e_tile.shape
    valid = (tile_base + lax.broadcasted_iota(I32, (1, tt), 1)) < n_tok
    eio = lax.broadcasted_iota(I32, (n_exp, tt), 0)
    onehot = [jnp.where((eio == e_tile[k:k + 1, :]) & valid, 1.0, 0.0) for k in range(k_)]
    count_k = [jnp.sum(o, axis=1, keepdims=True) for o in onehot]
    count = functools.reduce(lambda a, b: a + b, count_k)
    padded = jnp.floor((count + (ROUTE_CHUNK - 1)) * (1.0 / ROUTE_CHUNK)) * ROUTE_CHUNK
    r = lax.broadcasted_iota(I32, (n_exp, n_exp), 0)
    c = lax.broadcasted_iota(I32, (n_exp, n_exp), 1)
    padded_row = jnp.sum(jnp.where(r == c, jnp.broadcast_to(padded, (n_exp, n_exp)), 0.0), axis=0, keepdims=True)
    offset = jnp.sum(jnp.where(c < r, jnp.broadcast_to(padded_row, (n_exp, n_exp)), 0.0), axis=1, keepdims=True)
    earlier = (lax.broadcasted_iota(I32, (tt, tt), 0) < lax.broadcasted_iota(I32, (tt, tt), 1)).astype(BF16)
    base = offset
    pos = []
    for k in range(k_):
        before = _dot(onehot[k].astype(BF16), earlier) + base
        pos.append(jnp.where(valid, jnp.sum(onehot[k] * before, axis=0, keepdims=True), -1.0))
        base = base + count_k[k]
    return pos


def _segment_copies(nch_ref, off_ref, gbase_ref, tile, n_exp, make_copy):
    total = 0
    for e in range(n_exp):
        n = nch_ref[tile * n_exp + e]
        lo = off_ref[tile * n_exp + e]
        gb = gbase_ref[tile * n_exp + e]

        def body(c, carry, lo=lo, gb=gb):
            make_copy(pl.multiple_of(lo + c * ROUTE_CHUNK, ROUTE_CHUNK),
                      pl.multiple_of(gb + c * ROUTE_CHUNK, ROUTE_CHUNK)).start()
            return carry

        lax.fori_loop(0, n, body, 0)
        total = total + n
    return total


def _tile_chunks(nch_ref, tile, n_exp):
    total = 0
    for e in range(n_exp):
        total = total + nch_ref[tile * n_exp + e]
    return total


def _dispatch_kernel(nch_ref, off_ref, gbase_ref, zlo_ref, zhi_ref, h_ref, e_ref, x_hbm, xt_ref, zero_ref, sem,
                     *, n_tok, n_exp, n_blocks, block_rows):
    i = pl.program_id(0)
    tt = h_ref.shape[0]
    npos = xt_ref.shape[1]

    @pl.when(i == 0)
    def _():
        zero_ref[...] = jnp.zeros(zero_ref.shape, F32)

        def zero_block(b):
            rows0 = pl.multiple_of(b * block_rows, block_rows)
            return pltpu.make_async_copy(zero_ref, x_hbm.at[pl.ds(rows0, block_rows), :], sem)

        def start(b, c):
            zero_block(b).start()
            return c

        def wait(b, c):
            zero_block(b).wait()
            return c

        for e in range(n_exp + 1):
            lax.fori_loop(zlo_ref[e], zhi_ref[e], start, 0)
        for e in range(n_exp + 1):
            lax.fori_loop(zlo_ref[e], zhi_ref[e], wait, 0)

    slot = i % 2
    pos = _tile_plan(e_ref[...], i * tt, n_tok, n_exp)
    hb = h_ref[...].astype(BF16)
    for c in range(npos // ROUTE_ROWS):
        pio = (lax.broadcasted_iota(I32, (ROUTE_ROWS, tt), 0) + c * ROUTE_ROWS).astype(F32)
        sel = functools.reduce(lambda a, b: a | b, [pio == p for p in pos])
        xt_ref[slot, c * ROUTE_ROWS:(c + 1) * ROUTE_ROWS, :] = _dot(jnp.where(sel, 1.0, 0.0).astype(BF16), hb)

    def copy(src_row, dst_row):
        return pltpu.make_async_copy(xt_ref.at[slot, pl.ds(src_row, ROUTE_CHUNK), :],
                                     x_hbm.at[pl.ds(dst_row, ROUTE_CHUNK), :], sem)

    def wait_copies(n):
        def wait_one(c, carry):
            copy(0, 0).wait()
            return carry
        lax.fori_loop(0, n, wait_one, 0)

    @pl.when(i > 0)
    def _():
        wait_copies(_tile_chunks(nch_ref, i - 1, n_exp))

    total = _segment_copies(nch_ref, off_ref, gbase_ref, i, n_exp, copy)

    @pl.when(i == pl.num_programs(0) - 1)
    def _():
        wait_copies(total)


def _dispatch(h_all, e_idx, tables, *, n_tok, n_exp, n_blocks):
    t_pad, d = h_all.shape
    tt = OUT_ROWS
    npos = TOP_K * tt + n_exp * ROUTE_CHUNK
    kern = functools.partial(_dispatch_kernel, n_tok=n_tok, n_exp=n_exp, n_blocks=n_blocks, block_rows=MOE_ROWS)
    grid_spec = pltpu.PrefetchScalarGridSpec(
        num_scalar_prefetch=5,
        grid=(t_pad // tt,),
        in_specs=[pl.BlockSpec((tt, d), lambda i, *_: (i, 0)),
                  pl.BlockSpec((TOP_K, tt), lambda i, *_: (0, i))],
        out_specs=pl.BlockSpec(memory_space=pl.ANY),
        scratch_shapes=[pltpu.VMEM((2, npos, d), F32), pltpu.VMEM((MOE_ROWS, d), F32), pltpu.SemaphoreType.DMA],
    )
    return pl.pallas_call(
        kern,
        grid_spec=grid_spec,
        out_shape=jax.ShapeDtypeStruct((n_blocks * MOE_ROWS, d), F32),
        compiler_params=_params("arbitrary"),
        name="moe_dispatch",
    )(tables["nch"], tables["off"], tables["gbase"], tables["zlo"], tables["zhi"], h_all, e_idx)


def _moe_kernel(be_ref, nu_ref, x_ref, wgu_ref, bgu_ref, wdn_ref, bdn_ref, y_ref, wgu_bf, wdn_bf, *, de):
    i = pl.program_id(0)

    @pl.when(i >= nu_ref[0])
    def _():
        y_ref[...] = jnp.zeros(y_ref.shape, F32)

    @pl.when(i < nu_ref[0])
    def _():
        @pl.when((i == 0) | (be_ref[i] != be_ref[jnp.maximum(i - 1, 0)]))
        def _():
            wgu_bf[...] = wgu_ref[...].astype(BF16)
            wdn_bf[...] = wdn_ref[...].astype(BF16)

        x = x_ref[...].astype(BF16)
        cw = MOE_HIDDEN_CHUNK
        n_chunks = de // cw

        def up(c):
            lo, hi = c * cw, (c + 1) * cw
            return (_dot(x, wgu_bf[:, lo:hi]) + bgu_ref[:, lo:hi],
                    _dot(x, wgu_bf[:, de + lo:de + hi]) + bgu_ref[:, de + lo:de + hi])

        pending = {c: up(c) for c in range(min(MOE_LOOKAHEAD, n_chunks))}
        y = bdn_ref[...]
        for c in range(n_chunks):
            if c + MOE_LOOKAHEAD < n_chunks:
                pending[c + MOE_LOOKAHEAD] = up(c + MOE_LOOKAHEAD)
            glu, lin = pending.pop(c)
            glu = jnp.minimum(glu, SWIGLU_LIMIT)
            lin = jnp.clip(lin, -SWIGLU_LIMIT, SWIGLU_LIMIT)
            act = glu * jax.nn.sigmoid(SWIGLU_ALPHA * glu) * (lin + 1.0)
            y = y + _dot(act.astype(BF16), wdn_bf[c * cw:(c + 1) * cw, :])
        y_ref[...] = y


def _moe(x_rows, block_expert, n_used, w_gu, b_gu, w_dn, b_dn):
    rows = MOE_ROWS
    n_blocks = block_expert.shape[0]
    n_exp, d, de2 = w_gu.shape
    de = de2 // 2
    kern = functools.partial(_moe_kernel, de=de)

    def used(i, nu):
        return jnp.minimum(i, nu[0] - 1)

    def expert(i, be, nu):
        return (be[used(i, nu)], 0, 0)

    grid_spec = pltpu.PrefetchScalarGridSpec(
        num_scalar_prefetch=2,
        grid=(n_blocks,),
        in_specs=[
            pl.BlockSpec((rows, d), lambda i, be, nu: (used(i, nu), 0)),
            pl.BlockSpec((None, d, de2), expert),
            pl.BlockSpec((None, 1, de2), expert),
            pl.BlockSpec((None, de, d), expert),
            pl.BlockSpec((None, 1, d), expert),
        ],
        out_specs=pl.BlockSpec((rows, d), lambda i, be, nu: (i, 0)),
        scratch_shapes=[pltpu.VMEM((d, de2), BF16), pltpu.VMEM((de, d), BF16)],
    )
    return pl.pallas_call(
        kern,
        grid_spec=grid_spec,
        out_shape=jax.ShapeDtypeStruct(x_rows.shape, F32),
        compiler_params=_params("arbitrary"),
        name="moe_experts",
    )(block_expert, n_used, x_rows, w_gu, b_gu.reshape(n_exp, 1, de2), w_dn, b_dn.reshape(n_exp, 1, d))


def _route_tables(e_idx, n_tok, n_exp):
    k_, t_pad = e_idx.shape
    tt = OUT_ROWS
    rows = MOE_ROWS
    tiles = t_pad // tt
    valid = jnp.arange(t_pad, dtype=I32) < n_tok
    e_m = jnp.where(valid[None, :], e_idx, -1).reshape(k_, tiles, tt, 1)
    cnt = jnp.sum((e_m == jnp.arange(n_exp, dtype=I32)).astype(I32), axis=(0, 2))
    cpad = ((cnt + ROUTE_CHUNK - 1) // ROUTE_CHUNK) * ROUTE_CHUNK
    off = jnp.cumsum(cpad, axis=1) - cpad
    total = jnp.sum(cpad, axis=0)
    region = ((total + rows - 1) // rows) * rows
    pend = jnp.cumsum(region)
    pstart = pend - region
    gbase = pstart[None, :] + jnp.cumsum(cpad, axis=0) - cpad
    n_blocks = -(-(k_ * n_tok + n_exp * (tiles * (ROUTE_CHUNK - 1) + rows - 1)) // rows)
    block_start = jnp.arange(n_blocks, dtype=I32) * rows
    block_expert = jnp.minimum(jnp.sum((pend[None, :] <= block_start[:, None]).astype(I32), axis=1), n_exp - 1)
    n_used = (pend[-1] // rows).astype(I32)
    zlo = jnp.concatenate([(pstart + total) // rows, n_used[None]]).astype(I32)
    zhi = jnp.concatenate([pend // rows, jnp.full((1,), n_blocks, I32)]).astype(I32)
    tables = dict(nch=(cpad // ROUTE_CHUNK).reshape(-1).astype(I32), off=off.reshape(-1).astype(I32),
                  gbase=gbase.reshape(-1).astype(I32), zlo=zlo, zhi=zhi)
    return tables, block_expert.astype(I32), n_used.reshape(1), n_blocks


def _final_kernel(nch_ref, off_ref, gbase_ref, e_ref, gt_ref, h_ref, g_ref, b_ref, y_hbm, op_ref, os_ref,
                  yt_ref, sem, *, n_tok, n_exp, n_prompt, alpha):
    i = pl.program_id(0)
    n_tiles = pl.num_programs(0)
    tt = h_ref.shape[0]
    npos = yt_ref.shape[1]
    slot = i % 2

    def fetch(tile, buf):
        def copy(dst_row, src_row):
            return pltpu.make_async_copy(y_hbm.at[pl.ds(src_row, ROUTE_CHUNK), :],
                                         yt_ref.at[buf, pl.ds(dst_row, ROUTE_CHUNK), :], sem.at[buf])
        return copy

    @pl.when(i == 0)
    def _():
        yt_ref[...] = jnp.zeros(yt_ref.shape, F32)
        _segment_copies(nch_ref, off_ref, gbase_ref, 0, n_exp, fetch(0, 0))

    @pl.when(i + 1 < n_tiles)
    def _():
        _segment_copies(nch_ref, off_ref, gbase_ref, i + 1, n_exp, fetch(i + 1, 1 - slot))

    pos = _tile_plan(e_ref[...], i * tt, n_tok, n_exp)
    top_k = len(pos)
    rows = jnp.concatenate(pos + [gt_ref[...], jnp.zeros((LANES - 2 * top_k, tt), F32)], axis=0)
    cols = rows.T

    def wait_one(c, carry):
        fetch(i, slot)(0, 0).wait()
        return carry

    lax.fori_loop(0, _tile_chunks(nch_ref, i, n_exp), wait_one, 0)
    f = jnp.zeros((tt, h_ref.shape[1]), F32)
    for c in range(npos // ROUTE_ROWS):
        pio = (lax.broadcasted_iota(I32, (tt, ROUTE_ROWS), 1) + c * ROUTE_ROWS).astype(F32)
        w = functools.reduce(lambda a, b: a + b, [
            jnp.where(pio == cols[:, k:k + 1], cols[:, top_k + k:top_k + k + 1], 0.0) for k in range(top_k)])
        f = f + _dot(w.astype(BF16), yt_ref[slot, c * ROUTE_ROWS:(c + 1) * ROUTE_ROWS, :].astype(BF16))
    out = _layer_norm(alpha * h_ref[...] + f, g_ref[...], b_ref[...])

    @pl.when(i < n_prompt)
    def _():
        op_ref[...] = out

    @pl.when(i == n_prompt)
    def _():
        os_ref[...] = out[:os_ref.shape[0], :]


def _final(y_rows, e_idx, gates, h_all, g, b, tables, *, tp, ts, n_exp, alpha):
    tt = OUT_ROWS
    t_pad, d = h_all.shape
    n_prompt = tp // tt
    npos = TOP_K * tt + n_exp * ROUTE_CHUNK
    const = lambda i, *_: (0, 0)
    kern = functools.partial(_final_kernel, n_tok=tp + ts, n_exp=n_exp, n_prompt=n_prompt, alpha=alpha)
    grid_spec = pltpu.PrefetchScalarGridSpec(
        num_scalar_prefetch=3,
        grid=(t_pad // tt,),
        in_specs=[pl.BlockSpec((TOP_K, tt), lambda i, *_: (0, i)),
                  pl.BlockSpec((TOP_K, tt), lambda i, *_: (0, i)),
                  pl.BlockSpec((tt, d), lambda i, *_: (i, 0)),
                  pl.BlockSpec((1, d), const), pl.BlockSpec((1, d), const),
                  pl.BlockSpec(memory_space=pl.ANY)],
        out_specs=[pl.BlockSpec((tt, d), lambda i, *_: (jnp.minimum(i, n_prompt - 1), 0)),
                   pl.BlockSpec((ts, d), const)],
        scratch_shapes=[pltpu.VMEM((2, npos, d), F32), pltpu.SemaphoreType.DMA((2,))],
    )
    return pl.pallas_call(
        kern,
        grid_spec=grid_spec,
        out_shape=[jax.ShapeDtypeStruct((tp, d), F32), jax.ShapeDtypeStruct((ts, d), F32)],
        compiler_params=_params("arbitrary"),
        name="combine_ln",
    )(tables["nch"], tables["off"], tables["gbase"], e_idx, gates, h_all, g, b, y_rows)


def _pad_rows(a, rows):
    return jnp.pad(a, ((0, rows - a.shape[0]), (0, 0)))


def kernel(x_prompt, x_sample, cache_k, cache_v, state_ssm_re, state_ssm_im, page_table, w_in, w_out, ssm_a_re, ssm_a_im, ssm_log_dt, ssm_b_re, ssm_b_im, ssm_c_re, ssm_c_im, ssm_d, w_glu, b_glu, lambda_q1, lambda_k1, lambda_q2, lambda_k2, subln_g, ln1_g, ln1_b, w_router, b_router, w_gate_up, b_gate_up, w_down, b_down, ln2_g, ln2_b):
    depth = w_in.shape[0]
    bp, lp, d = x_prompt.shape
    bs, ls, _ = x_sample.shape
    n_phys, page, heads, hd = cache_k.shape[1:]
    assert ls == 1 and hd == LANES
    qk = hd // 2
    aw = heads * hd
    groups, n_state = ssm_a_re.shape[1:]
    sw = groups * ssm_b_re.shape[-1]
    gp = groups * n_state
    n_exp = w_router.shape[2]
    past_len = page_table.shape[1] * page
    alpha = (2 * depth) ** 0.25
    tp = bp * lp
    n_tok = tp + bs

    cos_p, sin_p = _rope_tables(jnp.arange(lp, dtype=I32), qk // 2)
    cos_s, sin_s = _rope_tables(jnp.full((bs,), past_len, I32), qk // 2)
    ck = cache_k.reshape(depth * n_phys * page * heads, hd)
    cv = cache_v.reshape(depth * n_phys * page * heads, hd)

    hp = x_prompt
    hs = x_sample.reshape(1, bs, d)
    outs = [[] for _ in range(8)]
    for l in range(depth):
        lam_init = 0.8 - 0.6 * math.exp(-0.3 * l)
        w_in_bf = w_in[l].astype(BF16)
        w_out_bf = w_out[l].astype(BF16)
        w_glu_bf = w_glu[l].astype(BF16)
        bg = b_glu[l].reshape(1, 2 * sw)
        lam_p = [p[l].reshape(1, qk).astype(F32) for p in (lambda_q1, lambda_k1, lambda_q2, lambda_k2)]
        prm = _ssm_params(ssm_a_re[l], ssm_a_im[l], ssm_log_dt[l], ssm_b_re[l], ssm_b_im[l],
                          ssm_c_re[l], ssm_c_im[l], ssm_d[l])

        tt = min(PROJ_ROWS // bp, lp)
        u_p, q_p, k_p, v_p = _project(hp, w_in_bf, cos_p, sin_p, tt=tt, sw=sw, aw=aw, half=qk // 2)
        zero = jnp.zeros((bp, gp), F32)
        ssm_p, sre_p, sim_p = _ssm(u_p, zero, zero, prm, nb=bp, steps=min(SSM_STEPS, lp), w_glu_bf=w_glu_bf,
                                   b_glu=bg)
        att_p = _attn_prompt(q_p, k_p, v_p, lam_p, subln_g[l].reshape(hd, 1), heads=heads, lam_init=lam_init)

        u_s, q_s, k_s, v_s = _project(hs, w_in_bf, cos_s, sin_s, tt=bs, sw=sw, aw=aw, half=qk // 2)
        ssm_s, sre_s, sim_s = _ssm(u_s, state_ssm_re[l].reshape(bs, gp), state_ssm_im[l].reshape(bs, gp), prm,
                                   nb=bs, steps=1, w_glu_bf=w_glu_bf, b_glu=bg)
        att_s = _attn_sample(q_s.reshape(bs, aw), k_s.reshape(bs, aw), v_s.reshape(bs, aw), ck, cv,
                             page_table + l * n_phys, lam_p, subln_g[l].reshape(1, hd), page=page, heads=heads,
                             lam_init=lam_init)

        wr = w_router[l].T.astype(F32)
        wr_hi = wr.astype(BF16)
        wr_lo = (wr - wr_hi.astype(F32)).astype(BF16)
        h_all, e_idx, gates = _out_route(
            hp.reshape(tp, d), ssm_p.reshape(tp, sw), att_p, _pad_rows(hs.reshape(bs, d), OUT_ROWS),
            _pad_rows(ssm_s, OUT_ROWS), _pad_rows(att_s, OUT_ROWS), w_out_bf, ln1_g[l].reshape(1, d),
            ln1_b[l].reshape(1, d), wr_hi, wr_lo, b_router[l].reshape(n_exp, 1), alpha=alpha)
        tables, block_expert, n_used, n_blocks = _route_tables(e_idx, n_tok, n_exp)
        x_rows = _dispatch(h_all, e_idx, tables, n_tok=n_tok, n_exp=n_exp, n_blocks=n_blocks)
        y_rows = _moe(x_rows, block_expert, n_used, w_gate_up[l], b_gate_up[l], w_down[l], b_down[l])
        y_p, y_s = _final(y_rows, e_idx, gates, h_all, ln2_g[l].reshape(1, d), ln2_b[l].reshape(1, d), tables,
                          tp=tp, ts=bs, n_exp=n_exp, alpha=alpha)
        hp = y_p.reshape(bp, lp, d)
        hs = y_s.reshape(1, bs, d)

        for lst, val in zip(outs, (
                k_p.reshape(bp, lp, heads, hd), v_p.reshape(bp, lp, heads, hd),
                sre_p.reshape(bp, groups, n_state), sim_p.reshape(bp, groups, n_state),
                k_s.reshape(bs, ls, heads, hd), v_s.reshape(bs, ls, heads, hd),
                sre_s.reshape(bs, groups, n_state), sim_s.reshape(bs, groups, n_state))):
            lst.append(val)

    return (hp, hs.reshape(bs, ls, d)) + tuple(jnp.stack(o) for o in outs)
```

```python
import functools
import math

import jax
import jax.numpy as jnp
from jax import lax
from jax.experimental import pallas as pl
from jax.experimental.pallas import tpu as pltpu

F32 = jnp.float32
BF16 = jnp.bfloat16
I32 = jnp.int32

TOP_K = 4
ROPE_THETA = 10000.0
SWIGLU_ALPHA = 1.702
SWIGLU_LIMIT = 7.0
LN_EPS = 1e-5
RMS_EPS = 1e-5
SSM_A_RE_MAX = -1e-4
GELU_C0 = math.sqrt(2.0 / math.pi)
GELU_C1 = 0.044715
NEG_BIG = -1e30
LOG2E = math.log2(math.e)

LANES = 128
SUBLANES = 8
MXU_DIM = 256
VMEM_LIMIT_BYTES = 56 * 1024 * 1024

PROJ_ROWS = 512
SSM_STEPS = 64
ATT_ROWS = 256
ATT_LOOKAHEAD = 4
MOE_HIDDEN_CHUNK = 256
MOE_LOOKAHEAD = 2
ROUTE_CHUNK = 8
ROUTE_ROWS = 256
ROUTE_WAIT = 32
OUT_ROWS = 512
MOE_ROWS = 256
SCAN_LANES = 512


def _params(*sem):
    return pltpu.CompilerParams(dimension_semantics=sem, vmem_limit_bytes=VMEM_LIMIT_BYTES)


def _dot(a, b):
    return jnp.dot(a, b, preferred_element_type=F32)


def _dot_nt(a, b):
    return lax.dot_general(a, b, (((1,), (1,)), ((), ())), preferred_element_type=F32)


def _proj_kernel(x_ref, w_ref, cos_ref, sin_ref, u_ref, q_ref, k_ref, v_ref, *, nb, tt, sw, aw, half):
    d = x_ref.shape[-1]
    heads = aw // LANES
    acc = _dot(x_ref[...].reshape(nb * tt, d).astype(BF16), w_ref[...])
    c = cos_ref[...]
    s = sin_ref[...]
    lane = lax.broadcasted_iota(I32, c.shape, 1)
    first = (lane % (2 * half)) < half

    def rope(blk):
        swapped = jnp.where(first, pltpu.roll(blk, LANES - half, 1), pltpu.roll(blk, half, 1))
        return blk * c + swapped * s

    for b in range(nb):
        rows = slice(b * tt, (b + 1) * tt)
        for j in range(sw // LANES):
            u_ref[j, pl.ds(b, tt, stride=nb), :] = acc[rows, j * LANES:(j + 1) * LANES]
        q_ref[b] = jnp.concatenate(
            [rope(acc[rows, sw + j * LANES:sw + (j + 1) * LANES]) for j in range(heads)], axis=1)
        for j in range(heads):
            lo = sw + aw + j * LANES
            k_ref[b, pl.ds(j, tt, stride=heads), :] = rope(acc[rows, lo:lo + LANES])
            lo = sw + 2 * aw + j * LANES
            v_ref[b, pl.ds(j, tt, stride=heads), :] = acc[rows, lo:lo + LANES]


def _project(x3, w_bf, cos_t, sin_t, *, tt, sw, aw, half):
    nb, seq, d = x3.shape
    heads = aw // LANES
    kern = functools.partial(_proj_kernel, nb=nb, tt=tt, sw=sw, aw=aw, half=half)
    return pl.pallas_call(
        kern,
        grid=(seq // tt,),
        in_specs=[
            pl.BlockSpec((nb, tt, d), lambda i: (0, i, 0)),
            pl.BlockSpec(w_bf.shape, lambda i: (0, 0)),
            pl.BlockSpec((tt, LANES), lambda i: (i, 0)),
            pl.BlockSpec((tt, LANES), lambda i: (i, 0)),
        ],
        out_specs=[
            pl.BlockSpec((sw // LANES, tt * nb, LANES), lambda i: (0, i, 0)),
            pl.BlockSpec((nb, tt, aw), lambda i: (0, i, 0)),
            pl.BlockSpec((nb, tt * heads, LANES), lambda i: (0, i, 0)),
            pl.BlockSpec((nb, tt * heads, LANES), lambda i: (0, i, 0)),
        ],
        out_shape=[
            jax.ShapeDtypeStruct((sw // LANES, seq * nb, LANES), F32),
            jax.ShapeDtypeStruct((nb, seq, aw), F32),
            jax.ShapeDtypeStruct((nb, seq * heads, LANES), F32),
            jax.ShapeDtypeStruct((nb, seq * heads, LANES), F32),
        ],
        compiler_params=_params("arbitrary"),
        name="proj_rope",
    )(x3, w_bf, cos_t, sin_t)


def _rope_tables(pos, half):
    inv = ROPE_THETA ** (-jnp.arange(half, dtype=F32) / half)
    ang = pos.astype(F32)[:, None] * inv[None, :]
    reps = LANES // half
    cos = jnp.tile(jnp.cos(ang), (1, reps))
    sin = jnp.sin(ang)
    sin = jnp.tile(jnp.concatenate([-sin, sin], axis=1), (1, reps // 2))
    return cos, sin


def _ssm_kernel(u_ref, s0r_ref, s0i_ref, ar_ref, ai_ref, wb_ref, wc_ref, d_ref, wg_ref, bg_ref,
                o_ref, sr_ref, si_ref, bu_ref, st_ref, ot_ref, *, nb, steps, gp, sw):
    i = pl.program_id(0)
    rows = nb * steps

    @pl.when(i == 0)
    def _():
        st_ref[:, :gp] = s0r_ref[...]
        st_ref[:, gp:] = s0i_ref[...]

    u = jnp.concatenate([u_ref[j] for j in range(sw // LANES)], axis=1)
    ub = u.astype(BF16)
    n_tiles = 2 * gp // MXU_DIM
    per_half = gp // MXU_DIM
    in_tiles = sw // MXU_DIM
    for j in range(n_tiles):
        kt = ((j % per_half) * in_tiles) // per_half
        bu_ref[:, j * MXU_DIM:(j + 1) * MXU_DIM] = _dot(ub[:, kt * MXU_DIM:(kt + 1) * MXU_DIM], wb_ref[j])

    for c in range(gp // SCAN_LANES):
        lo_r = c * SCAN_LANES
        lo_i = gp + c * SCAN_LANES
        ar = jnp.broadcast_to(ar_ref[:, lo_r:lo_r + SCAN_LANES], (nb, SCAN_LANES))
        ai = jnp.broadcast_to(ai_ref[:, lo_r:lo_r + SCAN_LANES], (nb, SCAN_LANES))

        def step(t, carry, lo_r=lo_r, lo_i=lo_i, ar=ar, ai=ai):
            sr, si = carry
            r0 = pl.multiple_of(t * nb, nb)
            br = bu_ref[pl.ds(r0, nb), lo_r:lo_r + SCAN_LANES]
            bi = bu_ref[pl.ds(r0, nb), lo_i:lo_i + SCAN_LANES]
            nr = ar * sr - ai * si + br
            ni = ar * si + ai * sr + bi
            bu_ref[pl.ds(r0, nb), lo_r:lo_r + SCAN_LANES] = nr
            bu_ref[pl.ds(r0, nb), lo_i:lo_i + SCAN_LANES] = ni
            return nr, ni

        sr0 = st_ref[:, lo_r:lo_r + SCAN_LANES]
        si0 = st_ref[:, lo_i:lo_i + SCAN_LANES]
        if steps == 1:
            sr1, si1 = step(0, (sr0, si0))
        else:
            sr1, si1 = lax.fori_loop(0, steps, step, (sr0, si0))
        st_ref[:, lo_r:lo_r + SCAN_LANES] = sr1
        st_ref[:, lo_i:lo_i + SCAN_LANES] = si1

    sr_ref[...] = st_ref[:, :gp]
    si_ref[...] = st_ref[:, gp:]

    out_tiles = sw // MXU_DIM
    kw = gp // out_tiles
    ys = []
    for nt in range(out_tiles):
        s_re = bu_ref[:, nt * kw:(nt + 1) * kw].astype(BF16)
        s_im = bu_ref[:, gp + nt * kw:gp + (nt + 1) * kw].astype(BF16)
        ys.append(_dot(s_re, wc_ref[nt, 0]) + _dot(s_im, wc_ref[nt, 1]))
    y = jnp.concatenate(ys, axis=1) + d_ref[...] * u
    y = 0.5 * y * (1.0 + jnp.tanh(GELU_C0 * (y + GELU_C1 * (y * y * y))))
    z = _dot(y.astype(BF16), wg_ref[...]) + bg_ref[...]
    out = z[:, :sw] * jax.nn.sigmoid(z[:, sw:])
    if steps == 1:
        o_ref[...] = out
    else:
        for j in range(sw // LANES):
            ot_ref[j] = out[:, j * LANES:(j + 1) * LANES]
        for b in range(nb):
            for j in range(sw // LANES):
                o_ref[b, :, j * LANES:(j + 1) * LANES] = ot_ref[j, pl.ds(b, steps, stride=nb), :]


def _ssm(u_tb, s0r, s0i, prm, *, nb, steps, w_glu_bf, b_glu):
    n_slabs, n_rows, _ = u_tb.shape
    sw = n_slabs * LANES
    seq = n_rows // nb
    gp = prm["a_re"].shape[1]
    rows = nb * steps
    grid = (seq // steps,)
    const2 = lambda i: (0, 0)
    kern = functools.partial(_ssm_kernel, nb=nb, steps=steps, gp=gp, sw=sw)
    if steps == 1:
        o_spec = pl.BlockSpec((nb, sw), const2)
        o_shape = jax.ShapeDtypeStruct((nb, sw), F32)
    else:
        o_spec = pl.BlockSpec((nb, steps, sw), lambda i: (0, i, 0))
        o_shape = jax.ShapeDtypeStruct((nb, seq, sw), F32)
    return pl.pallas_call(
        kern,
        grid=grid,
        in_specs=[
            pl.BlockSpec((n_slabs, rows, LANES), lambda i: (0, i, 0)),
            pl.BlockSpec((nb, gp), const2),
            pl.BlockSpec((nb, gp), const2),
            pl.BlockSpec((1, gp), const2),
            pl.BlockSpec((1, gp), const2),
            pl.BlockSpec(prm["wb"].shape, lambda i: (0, 0, 0)),
            pl.BlockSpec(prm["wc"].shape, lambda i: (0, 0, 0, 0)),
            pl.BlockSpec((1, sw), const2),
            pl.BlockSpec(w_glu_bf.shape, const2),
            pl.BlockSpec((1, 2 * sw), const2),
        ],
        out_specs=[o_spec, pl.BlockSpec((nb, gp), const2), pl.BlockSpec((nb, gp), const2)],
        out_shape=[o_shape, jax.ShapeDtypeStruct((nb, gp), F32), jax.ShapeDtypeStruct((nb, gp), F32)],
        scratch_shapes=[
            pltpu.VMEM((rows, 2 * gp), F32),
            pltpu.VMEM((nb, 2 * gp), F32),
            pltpu.VMEM((sw // LANES, rows, LANES), F32),
        ],
        compiler_params=_params("arbitrary"),
        name="ssm_mixer",
    )(u_tb, s0r, s0i, prm["a_re"], prm["a_im"], prm["wb"], prm["wc"], prm["d"], w_glu_bf, b_glu)


def _ssm_params(a_re, a_im, log_dt, b_re, b_im, c_re, c_im, d):
    g_, p_ = a_re.shape
    h_ = b_re.shape[-1]
    gp = g_ * p_
    sw = g_ * h_
    a_re = jnp.minimum(a_re.astype(F32), SSM_A_RE_MAX)
    a_im = a_im.astype(F32)
    dt = jnp.exp(log_dt.astype(F32))[:, None]
    mag = jnp.exp(a_re * dt)
    lb_re = mag * jnp.cos(a_im * dt)
    lb_im = mag * jnp.sin(a_im * dt)
    den = jnp.square(a_re) + jnp.square(a_im)
    nr, ni = lb_re - 1.0, lb_im
    f_re = (nr * a_re + ni * a_im) / den
    f_im = (ni * a_re - nr * a_im) / den
    b_re = b_re.astype(F32)
    b_im = b_im.astype(F32)
    bb_re = f_re[..., None] * b_re - f_im[..., None] * b_im
    bb_im = f_re[..., None] * b_im + f_im[..., None] * b_re
    eye = jnp.eye(g_, dtype=F32)
    w_in_re = jnp.einsum("gph,gk->ghkp", bb_re, eye).reshape(sw, gp)
    w_in_im = jnp.einsum("gph,gk->ghkp", bb_im, eye).reshape(sw, gp)
    w_in = jnp.concatenate([w_in_re, w_in_im], axis=1)
    per_half = gp // MXU_DIM
    in_tiles = sw // MXU_DIM
    tiles = []
    for j in range(2 * per_half):
        kt = ((j % per_half) * in_tiles) // per_half
        tiles.append(w_in[kt * MXU_DIM:(kt + 1) * MXU_DIM, j * MXU_DIM:(j + 1) * MXU_DIM])
    wb = jnp.stack(tiles).astype(BF16)
    w_out_re = jnp.einsum("ghp,gk->gpkh", c_re.astype(F32), eye).reshape(gp, sw)
    w_out_im = -jnp.einsum("ghp,gk->gpkh", c_im.astype(F32), eye).reshape(gp, sw)
    out_tiles = sw // MXU_DIM
    kw = gp // out_tiles
    wc = jnp.stack([
        jnp.stack([w_out_re[nt * kw:(nt + 1) * kw, nt * MXU_DIM:(nt + 1) * MXU_DIM],
                   w_out_im[nt * kw:(nt + 1) * kw, nt * MXU_DIM:(nt + 1) * MXU_DIM]])
        for nt in range(out_tiles)]).astype(BF16)
    return dict(a_re=lb_re.reshape(1, gp), a_im=lb_im.reshape(1, gp), wb=wb, wc=wc,
                d=d.astype(F32).reshape(1, sw))


def _diff_lambda(lq1, lk1, lq2, lk2, lam_init):
    l1 = jnp.exp(jnp.sum(lq1 * lk1, axis=1, keepdims=True))
    l2 = jnp.exp(jnp.sum(lq2 * lk2, axis=1, keepdims=True))
    return l1 - l2 + lam_init


def _head_norm(o, g, lam_init):
    ms = jnp.mean(o * o, axis=-1, keepdims=True)
    return o * lax.rsqrt(ms + RMS_EPS) * g * (1.0 - lam_init)


def _attn_prompt_kernel(q_ref, k_ref, v_ref, lq1_ref, lk1_ref, lq2_ref, lk2_ref, g_ref, o_ref,
                        kb_ref, vt_ref, qs_ref, acc_ref, *, tq, qk, heads, lam_init, scale):
    h = pl.program_id(1)
    nq = vt_ref.shape[0]
    seq = nq * tq
    kb_ref[...] = k_ref[pl.ds(h, seq, stride=heads), :].astype(BF16)
    for jb in range(nq):
        vt_ref[jb] = v_ref[pl.ds(jb * tq * heads + h, tq, stride=heads), :].T.astype(BF16)
    q = q_ref[...] * (scale * LOG2E)
    lane = lax.broadcasted_iota(I32, q.shape, 1)
    qs_ref[0] = jnp.where(lane < qk, q, 0.0).astype(BF16)
    qs_ref[1] = jnp.where(lane >= qk, q, 0.0).astype(BF16)
    acc_ref[...] = jnp.zeros(acc_ref.shape, F32)
    key = lax.broadcasted_iota(I32, (tq, tq), 0)
    qry = lax.broadcasted_iota(I32, (tq, tq), 1)
    keep = key <= qry
    m = [[jnp.full((1, tq), NEG_BIG, F32)] * 2 for _ in range(nq)]
    l = [[jnp.zeros((1, tq), F32)] * 2 for _ in range(nq)]
    tasks = [(j, i, c) for j in range(nq) for i in range(j, nq) for c in range(2)]

    def scores(t):
        j, i, c = tasks[t]
        st = _dot_nt(kb_ref[j * tq:(j + 1) * tq, :], qs_ref[c, i * tq:(i + 1) * tq, :])
        return jnp.where(keep, st, NEG_BIG) if i == j else st

    pending = {t: scores(t) for t in range(min(ATT_LOOKAHEAD, len(tasks)))}
    for t, (j, i, c) in enumerate(tasks):
        if t + ATT_LOOKAHEAD < len(tasks):
            pending[t + ATT_LOOKAHEAD] = scores(t + ATT_LOOKAHEAD)
        st = pending.pop(t)
        m_new = jnp.maximum(m[i][c], jnp.max(st, axis=0, keepdims=True))
        alpha = jnp.exp2(m[i][c] - m_new)
        p = jnp.exp2(st - m_new)
        acc_ref[i, c] = alpha * acc_ref[i, c] + _dot(vt_ref[j], p.astype(BF16))
        l[i][c] = alpha * l[i][c] + jnp.sum(p, axis=0, keepdims=True)
        m[i][c] = m_new
    lam = _diff_lambda(lq1_ref[...], lk1_ref[...], lq2_ref[...], lk2_ref[...], lam_init)
    for i in range(nq):
        ot = acc_ref[i, 0] * (1.0 / l[i][0]) - lam * (acc_ref[i, 1] * (1.0 / l[i][1]))
        ms = jnp.mean(ot * ot, axis=0, keepdims=True)
        ot = ot * lax.rsqrt(ms + RMS_EPS) * g_ref[...] * (1.0 - lam_init)
        o_ref[i * tq:(i + 1) * tq, :] = ot.T


def _attn_prompt(q, k, v, lam_p, g_col, *, heads, lam_init):
    nb, seq, aw = q.shape
    hd = aw // heads
    qk = hd // 2
    tq = min(ATT_ROWS, seq)
    nq = seq // tq
    kern = functools.partial(_attn_prompt_kernel, tq=tq, qk=qk, heads=heads, lam_init=lam_init, scale=qk ** -0.5)
    small = pl.BlockSpec((1, qk), lambda b, h: (0, 0))
    return pl.pallas_call(
        kern,
        grid=(nb, heads),
        in_specs=[
            pl.BlockSpec((None, seq, hd), lambda b, h: (b, 0, h)),
            pl.BlockSpec((None, seq * heads, hd), lambda b, h: (b, 0, 0)),
            pl.BlockSpec((None, seq * heads, hd), lambda b, h: (b, 0, 0)),
            small, small, small, small,
            pl.BlockSpec((hd, 1), lambda b, h: (0, 0)),
        ],
        out_specs=pl.BlockSpec((seq, hd), lambda b, h: (b, h)),
        out_shape=jax.ShapeDtypeStruct((nb * seq, aw), F32),
        scratch_shapes=[pltpu.VMEM((seq, hd), BF16), pltpu.VMEM((nq, hd, tq), BF16),
                        pltpu.VMEM((2, seq, hd), BF16), pltpu.VMEM((nq, 2, hd, tq), F32)],
        compiler_params=_params("arbitrary", "arbitrary"),
        name="diff_attn_prompt",
    )(q, k, v, *lam_p, g_col)


def _attn_sample_kernel(pt_ref, q_ref, ks_ref, vs_ref, lq1_ref, lk1_ref, lq2_ref, lk2_ref, g_ref, *rest,
                        n_pages, heads, qk, lam_init, scale):
    k_pages = rest[:n_pages]
    v_pages = rest[n_pages:2 * n_pages]
    o_ref = rest[2 * n_pages]
    del pt_ref
    hd = 2 * qk
    aw = heads * hd
    n_maps = 2 * heads
    q = q_ref[...] * scale
    row = lax.broadcasted_iota(I32, (n_maps, aw), 0)
    lane = lax.broadcasted_iota(I32, (n_maps, aw), 1)
    own = (lane // hd == row % heads) & ((lane // qk) % 2 == row // heads)
    qm = jnp.where(own, jnp.broadcast_to(q, (n_maps, aw)), 0.0)
    qmb = qm.astype(BF16)
    page = k_pages[0].shape[0] // heads

    def page_rows(ref):
        return jnp.concatenate([ref[pl.ds(h, page, stride=heads), :] for h in range(heads)], axis=1).astype(BF16)

    s_pages = [_dot_nt(qmb, page_rows(kp)) for kp in k_pages]
    s_self = jnp.sum(qm * ks_ref[...], axis=1, keepdims=True)
    m = s_self
    for s in s_pages:
        m = jnp.maximum(m, jnp.max(s, axis=1, keepdims=True))
    p_self = jnp.exp(s_self - m)
    p_pages = [jnp.exp(s - m) for s in s_pages]
    l = p_self
    for p in p_pages:
        l = l + jnp.sum(p, axis=1, keepdims=True)
    inv = 1.0 / l
    lam = _diff_lambda(lq1_ref[...], lk1_ref[...], lq2_ref[...], lk2_ref[...], lam_init)

    def mix(p):
        pn = p * inv
        return pn - lam * pltpu.roll(pn, heads, 0)

    acc = jnp.broadcast_to(mix(p_self), (n_maps, aw)) * vs_ref[...]
    for p, vp in zip(p_pages, v_pages):
        acc = acc + _dot(mix(p).astype(BF16), page_rows(vp))
    pick = (row < heads) & (lane // hd == row)
    o = jnp.sum(jnp.where(pick, acc, 0.0), axis=0, keepdims=True)
    g = g_ref[...]
    for h in range(heads):
        o_ref[:, h * hd:(h + 1) * hd] = _head_norm(o[:, h * hd:(h + 1) * hd], g, lam_init)


def _attn_sample(q, k_new, v_new, cache_k, cache_v, page_ids, lam_p, g, *, page, heads, lam_init):
    bs, aw = q.shape
    n_pages = page_ids.shape[1]
    hd = aw // heads
    qk = hd // 2
    kern = functools.partial(_attn_sample_kernel, n_pages=n_pages, heads=heads, qk=qk, lam_init=lam_init,
                             scale=qk ** -0.5)
    row_spec = pl.BlockSpec((None, 1, aw), lambda b, pt: (b, 0, 0))
    small = pl.BlockSpec((1, qk), lambda b, pt: (0, 0))

    def page_spec(j):
        return pl.BlockSpec((page * heads, hd), lambda b, pt, j=j: (pt[b * n_pages + j], 0))

    grid_spec = pltpu.PrefetchScalarGridSpec(
        num_scalar_prefetch=1,
        grid=(bs,),
        in_specs=[row_spec, row_spec, row_spec, small, small, small, small,
                  pl.BlockSpec((1, hd), lambda b, pt: (0, 0))]
                 + [page_spec(j) for j in range(n_pages)] * 2,
        out_specs=row_spec,
    )
    out = pl.pallas_call(
        kern,
        grid_spec=grid_spec,
        out_shape=jax.ShapeDtypeStruct((bs, 1, aw), F32),
        compiler_params=_params("arbitrary"),
        name="diff_attn_sample",
    )(page_ids.reshape(-1).astype(I32), q.reshape(bs, 1, aw), k_new.reshape(bs, 1, aw),
      v_new.reshape(bs, 1, aw), *lam_p, g, *([cache_k] * n_pages), *([cache_v] * n_pages))
    return out.reshape(bs, aw)


def _layer_norm(z, g, b):
    mu = jnp.mean(z, axis=-1, keepdims=True)
    zc = z - mu
    var = jnp.mean(zc * zc, axis=-1, keepdims=True)
    return zc * lax.rsqrt(var + LN_EPS) * g + b


def _out_kernel(xp_ref, sp_ref, ap_ref, xs_ref, ss_ref, as_ref, wo_ref, g_ref, b_ref, wrh_ref, wrl_ref,
                br_ref, h_ref, e_ref, gt_ref, *, n_prompt, sw, alpha, top_k):
    i = pl.program_id(0)
    is_s = i == n_prompt
    x = jnp.where(is_s, xs_ref[...], xp_ref[...])
    s = jnp.where(is_s, ss_ref[...], sp_ref[...]).astype(BF16)
    a = jnp.where(is_s, as_ref[...], ap_ref[...]).astype(BF16)
    mix = _dot(s, wo_ref[:sw, :]) + _dot(a, wo_ref[sw:, :])
    h = _layer_norm(alpha * x + mix, g_ref[...], b_ref[...])
    h_ref[...] = h
    h_hi = h.astype(BF16)
    h_lo = (h - h_hi.astype(F32)).astype(BF16)
    lg = (_dot_nt(wrh_ref[...], h_hi) + _dot_nt(wrl_ref[...], h_hi) + _dot_nt(wrh_ref[...], h_lo)
          + br_ref[...])
    eio = lax.broadcasted_iota(I32, lg.shape, 0)
    n_exp = lg.shape[0]
    vals, idxs = [], []
    for _ in range(top_k):
        m = jnp.max(lg, axis=0, keepdims=True)
        idx = jnp.min(jnp.where(lg == m, eio, n_exp), axis=0, keepdims=True)
        vals.append(m)
        idxs.append(idx)
        lg = jnp.where(eio == idx, NEG_BIG, lg)
    tv = jnp.concatenate(vals, axis=0)
    pe = jnp.exp(tv - tv[0:1, :])
    gt_ref[...] = pe / jnp.sum(pe, axis=0, keepdims=True)
    e_ref[...] = jnp.concatenate(idxs, axis=0)


def _out_route(xp, sp, ap, xs, ss, as_, w_out_bf, g, b, wr_hi, wr_lo, br, *, alpha):
    tp, d = xp.shape
    sw = sp.shape[1]
    rows = OUT_ROWS
    n_prompt = tp // rows
    total = tp + rows
    n_exp = wr_hi.shape[0]
    pidx = lambda i: (jnp.minimum(i, n_prompt - 1), 0)
    const = lambda i: (0, 0)
    kern = functools.partial(_out_kernel, n_prompt=n_prompt, sw=sw, alpha=alpha, top_k=TOP_K)
    return pl.pallas_call(
        kern,
        grid=(n_prompt + 1,),
        in_specs=[
            pl.BlockSpec((rows, d), pidx), pl.BlockSpec((rows, sw), pidx), pl.BlockSpec((rows, d - sw), pidx),
            pl.BlockSpec((rows, d), const), pl.BlockSpec((rows, sw), const), pl.BlockSpec((rows, d - sw), const),
            pl.BlockSpec(w_out_bf.shape, const), pl.BlockSpec((1, d), const), pl.BlockSpec((1, d), const),
            pl.BlockSpec((n_exp, d), const), pl.BlockSpec((n_exp, d), const), pl.BlockSpec((n_exp, 1), const),
        ],
        out_specs=[
            pl.BlockSpec((rows, d), lambda i: (i, 0)),
            pl.BlockSpec((TOP_K, rows), lambda i: (0, i)),
            pl.BlockSpec((TOP_K, rows), lambda i: (0, i)),
        ],
        out_shape=[
            jax.ShapeDtypeStruct((total, d), F32),
            jax.ShapeDtypeStruct((TOP_K, total), I32),
            jax.ShapeDtypeStruct((TOP_K, total), F32),
        ],
        compiler_params=_params("arbitrary"),
        name="out_proj_route",
    )(xp, sp, ap, xs, ss, as_, w_out_bf, g, b, wr_hi, wr_lo, br)


def _tile_plan(e_tile, tile_base, n_tok, n_exp):
    k_, tt = e_tile.shape
    valid = (tile_base + lax.broadcasted_iota(I32, (1, tt), 1)) < n_tok
    eio = lax.broadcasted_iota(I32, (n_exp, tt), 0)
    onehot = [jnp.where((eio == e_tile[k:k + 1, :]) & valid, 1.0, 0.0) for k in range(k_)]
    count_k = [jnp.sum(o, axis=1, keepdims=True) for o in onehot]
    count = functools.reduce(lambda a, b: a + b, count_k)
    padded = jnp.floor((count + (ROUTE_CHUNK - 1)) * (1.0 / ROUTE_CHUNK)) * ROUTE_CHUNK
    r = lax.broadcasted_iota(I32, (n_exp, n_exp), 0)
    c = lax.broadcasted_iota(I32, (n_exp, n_exp), 1)
    padded_row = jnp.sum(jnp.where(r == c, jnp.broadcast_to(padded, (n_exp, n_exp)), 0.0), axis=0, keepdims=True)
    offset = jnp.sum(jnp.where(c < r, jnp.broadcast_to(padded_row, (n_exp, n_exp)), 0.0), axis=1, keepdims=True)
    earlier = (lax.broadcasted_iota(I32, (tt, tt), 0) < lax.broadcasted_iota(I32, (tt, tt), 1)).astype(BF16)
    base = offset
    pos = []
    for k in range(k_):
        before = _dot(onehot[k].astype(BF16), earlier) + base
        pos.append(jnp.where(valid, jnp.sum(onehot[k] * before, axis=0, keepdims=True), -1.0))
        base = base + count_k[k]
    return pos


def _segment_copies(src_ref, dst_ref, n, make_copy):
    def body(c, carry):
        make_copy(pl.multiple_of(src_ref[0, c], ROUTE_CHUNK), pl.multiple_of(dst_ref[0, c], ROUTE_CHUNK)).start()
        return carry

    lax.fori_loop(0, n, body, 0)


def _wait_chunks(n, make_copy):
    def wait_many(c, carry):
        make_copy(ROUTE_WAIT * ROUTE_CHUNK).wait()
        return carry

    def wait_one(c, carry):
        make_copy(ROUTE_CHUNK).wait()
        return carry

    lax.fori_loop(0, n // ROUTE_WAIT, wait_many, 0)
    lax.fori_loop(0, n % ROUTE_WAIT, wait_one, 0)


def _dispatch_kernel(nchunk_ref, zlo_ref, zhi_ref, src_ref, dst_ref, h_ref, e_ref, x_hbm, xt_ref, zero_ref, sem,
                     *, n_tok, n_exp, block_rows):
    i = pl.program_id(0)
    tt = h_ref.shape[0]
    npos = xt_ref.shape[1]

    @pl.when(i == 0)
    def _():
        zero_ref[...] = jnp.zeros(zero_ref.shape, F32)

        def zero_block(b):
            rows0 = pl.multiple_of(b * block_rows, block_rows)
            return pltpu.make_async_copy(zero_ref, x_hbm.at[pl.ds(rows0, block_rows), :], sem)

        def start(b, c):
            zero_block(b).start()
            return c

        def wait(b, c):
            zero_block(b).wait()
            return c

        for e in range(n_exp + 1):
            lax.fori_loop(zlo_ref[e], zhi_ref[e], start, 0)
        for e in range(n_exp + 1):
            lax.fori_loop(zlo_ref[e], zhi_ref[e], wait, 0)

    slot = i % 2
    pos = _tile_plan(e_ref[...], i * tt, n_tok, n_exp)
    hb = h_ref[...].astype(BF16)
    for c in range(npos // ROUTE_ROWS):
        pio = (lax.broadcasted_iota(I32, (ROUTE_ROWS, tt), 0) + c * ROUTE_ROWS).astype(F32)
        sel = functools.reduce(lambda a, b: a | b, [pio == p for p in pos])
        xt_ref[slot, c * ROUTE_ROWS:(c + 1) * ROUTE_ROWS, :] = _dot(jnp.where(sel, 1.0, 0.0).astype(BF16), hb)

    def copy(src_row, dst_row, rows=ROUTE_CHUNK):
        return pltpu.make_async_copy(xt_ref.at[slot, pl.ds(src_row, rows), :], x_hbm.at[pl.ds(dst_row, rows), :], sem)

    @pl.when(i > 0)
    def _():
        _wait_chunks(nchunk_ref[i - 1], lambda rows: copy(0, 0, rows))

    _segment_copies(src_ref, dst_ref, nchunk_ref[i], copy)

    @pl.when(i == pl.num_programs(0) - 1)
    def _():
        _wait_chunks(nchunk_ref[i], lambda rows: copy(0, 0, rows))


def _dispatch(h_all, e_idx, tables, *, n_tok, n_exp, n_blocks):
    t_pad, d = h_all.shape
    tt = OUT_ROWS
    npos = TOP_K * tt + n_exp * ROUTE_CHUNK
    kern = functools.partial(_dispatch_kernel, n_tok=n_tok, n_exp=n_exp, block_rows=MOE_ROWS)
    chunk_list = pl.BlockSpec((None, 1, npos // ROUTE_CHUNK), lambda i, *_: (i, 0, 0), memory_space=pltpu.SMEM)
    grid_spec = pltpu.PrefetchScalarGridSpec(
        num_scalar_prefetch=3,
        grid=(t_pad // tt,),
        in_specs=[chunk_list, chunk_list,
                  pl.BlockSpec((tt, d), lambda i, *_: (i, 0)),
                  pl.BlockSpec((TOP_K, tt), lambda i, *_: (0, i))],
        out_specs=pl.BlockSpec(memory_space=pl.ANY),
        scratch_shapes=[pltpu.VMEM((2, npos, d), F32), pltpu.VMEM((MOE_ROWS, d), F32), pltpu.SemaphoreType.DMA],
    )
    return pl.pallas_call(
        kern,
        grid_spec=grid_spec,
        out_shape=jax.ShapeDtypeStruct((n_blocks * MOE_ROWS, d), F32),
        compiler_params=_params("arbitrary"),
        name="moe_dispatch",
    )(tables["nchunk"], tables["zlo"], tables["zhi"], tables["src"], tables["dst"], h_all, e_idx)


def _moe_kernel(be_ref, nu_ref, slot_ref, next_ref, x_ref, wgu_hbm, bgu_ref, wdn_hbm, bdn_ref, y_ref,
                wgu_f, wdn_f, wgu_bf, wdn_bf, wsem, *, de):
    i = pl.program_id(0)

    def fetch(e, slot):
        return (pltpu.make_async_copy(wgu_hbm.at[e], wgu_f.at[slot], wsem.at[slot]),
                pltpu.make_async_copy(wdn_hbm.at[e], wdn_f.at[slot], wsem.at[slot]))

    @pl.when(i >= nu_ref[0])
    def _():
        y_ref[...] = jnp.zeros(y_ref.shape, F32)

    @pl.when(i < nu_ref[0])
    def _():
        slot = slot_ref[i]

        @pl.when(i == 0)
        def _():
            for cp in fetch(be_ref[0], 0):
                cp.start()

        @pl.when((i == 0) | (be_ref[i] != be_ref[jnp.maximum(i - 1, 0)]))
        def _():
            for cp in fetch(be_ref[i], slot):
                cp.wait()
            wgu_bf[...] = wgu_f[slot].astype(BF16)
            wdn_bf[...] = wdn_f[slot].astype(BF16)

            @pl.when(next_ref[i] >= 0)
            def _():
                for cp in fetch(next_ref[i], 1 - slot):
                    cp.start()

        x = x_ref[...].astype(BF16)
        cw = MOE_HIDDEN_CHUNK
        n_chunks = de // cw

        def up(c):
            lo, hi = c * cw, (c + 1) * cw
            return (_dot(x, wgu_bf[:, lo:hi]) + bgu_ref[:, lo:hi],
                    _dot(x, wgu_bf[:, de + lo:de + hi]) + bgu_ref[:, de + lo:de + hi])

        pending = {c: up(c) for c in range(min(MOE_LOOKAHEAD, n_chunks))}
        y = bdn_ref[...]
        for c in range(n_chunks):
            if c + MOE_LOOKAHEAD < n_chunks:
                pending[c + MOE_LOOKAHEAD] = up(c + MOE_LOOKAHEAD)
            glu, lin = pending.pop(c)
            glu = jnp.minimum(glu, SWIGLU_LIMIT)
            lin = jnp.clip(lin, -SWIGLU_LIMIT, SWIGLU_LIMIT)
            act = glu * jax.nn.sigmoid(SWIGLU_ALPHA * glu) * (lin + 1.0)
            y = y + _dot(act.astype(BF16), wdn_bf[c * cw:(c + 1) * cw, :])
        y_ref[...] = y


def _moe(x_rows, block_expert, n_used, w_gu, b_gu, w_dn, b_dn):
    rows = MOE_ROWS
    n_blocks = block_expert.shape[0]
    n_exp, d, de2 = w_gu.shape
    de = de2 // 2
    kern = functools.partial(_moe_kernel, de=de)
    idx = jnp.arange(n_blocks, dtype=I32)
    in_use = idx < n_used[0]
    changed = jnp.concatenate([jnp.ones((1,), bool), block_expert[1:] != block_expert[:-1]])
    w_slot = ((jnp.cumsum(changed.astype(I32)) - 1) % 2).astype(I32)
    later = (block_expert[None, :] > block_expert[:, None]) & in_use[None, :]
    next_e = jnp.min(jnp.where(later, block_expert[None, :], n_exp), axis=1)
    next_e = jnp.where(next_e < n_exp, next_e, -1).astype(I32)

    def used(i, nu):
        return jnp.minimum(i, nu[0] - 1)

    def expert(i, be, nu, *_):
        return (be[used(i, nu)], 0, 0)

    grid_spec = pltpu.PrefetchScalarGridSpec(
        num_scalar_prefetch=4,
        grid=(n_blocks,),
        in_specs=[
            pl.BlockSpec((rows, d), lambda i, be, nu, *_: (used(i, nu), 0)),
            pl.BlockSpec(memory_space=pl.ANY),
            pl.BlockSpec((None, 1, de2), expert),
            pl.BlockSpec(memory_space=pl.ANY),
            pl.BlockSpec((None, 1, d), expert),
        ],
        out_specs=pl.BlockSpec((rows, d), lambda i, *_: (i, 0)),
        scratch_shapes=[pltpu.VMEM((2, d, de2), F32), pltpu.VMEM((2, de, d), F32),
                        pltpu.VMEM((d, de2), BF16), pltpu.VMEM((de, d), BF16), pltpu.SemaphoreType.DMA((2,))],
    )
    return pl.pallas_call(
        kern,
        grid_spec=grid_spec,
        out_shape=jax.ShapeDtypeStruct(x_rows.shape, F32),
        compiler_params=_params("arbitrary"),
        name="moe_experts",
    )(block_expert, n_used, w_slot, next_e, x_rows, w_gu, b_gu.reshape(n_exp, 1, de2), w_dn,
      b_dn.reshape(n_exp, 1, d))


def _route_tables(e_idx, n_tok, n_exp):
    k_, t_pad = e_idx.shape
    tt = OUT_ROWS
    rows = MOE_ROWS
    tiles = t_pad // tt
    valid = jnp.arange(t_pad, dtype=I32) < n_tok
    e_m = jnp.where(valid[None, :], e_idx, -1).reshape(k_, tiles, tt, 1)
    cnt = jnp.sum((e_m == jnp.arange(n_exp, dtype=I32)).astype(I32), axis=(0, 2))
    cpad = ((cnt + ROUTE_CHUNK - 1) // ROUTE_CHUNK) * ROUTE_CHUNK
    off = jnp.cumsum(cpad, axis=1) - cpad
    total = jnp.sum(cpad, axis=0)
    region = ((total + rows - 1) // rows) * rows
    pend = jnp.cumsum(region)
    pstart = pend - region
    gbase = pstart[None, :] + jnp.cumsum(cpad, axis=0) - cpad
    n_blocks = -(-(k_ * n_tok + n_exp * (tiles * (ROUTE_CHUNK - 1) + rows - 1)) // rows)
    block_start = jnp.arange(n_blocks, dtype=I32) * rows
    block_expert = jnp.minimum(jnp.sum((pend[None, :] <= block_start[:, None]).astype(I32), axis=1), n_exp - 1)
    n_used = (pend[-1] // rows).astype(I32)
    zlo = jnp.concatenate([(pstart + total) // rows, n_used[None]]).astype(I32)
    zhi = jnp.concatenate([pend // rows, jnp.full((1,), n_blocks, I32)]).astype(I32)
    nch = cpad // ROUTE_CHUNK
    cum = jnp.cumsum(nch, axis=1)
    c_id = jnp.arange((k_ * tt + n_exp * ROUTE_CHUNK) // ROUTE_CHUNK, dtype=I32)
    owner = jnp.sum((cum[:, None, :] <= c_id[None, :, None]).astype(I32), axis=2)
    pick = (jnp.minimum(owner, n_exp - 1)[:, :, None] == jnp.arange(n_exp, dtype=I32)).astype(I32)
    within = (c_id[None, :] - jnp.sum(pick * (cum - nch)[:, None, :], axis=2)) * ROUTE_CHUNK
    src = jnp.sum(pick * off[:, None, :], axis=2) + within
    dst = jnp.sum(pick * gbase[:, None, :], axis=2) + within
    tables = dict(nchunk=cum[:, -1].astype(I32), src=src[:, None, :].astype(I32), dst=dst[:, None, :].astype(I32),
                  zlo=zlo, zhi=zhi)
    return tables, block_expert.astype(I32), n_used.reshape(1), n_blocks


def _final_kernel(nchunk_ref, srcc_ref, dstc_ref, srcn_ref, dstn_ref, e_ref, gt_ref, h_ref, g_ref, b_ref, y_hbm,
                  op_ref, os_ref, yt_ref, sem, *, n_tok, n_exp, n_prompt, alpha):
    i = pl.program_id(0)
    n_tiles = pl.num_programs(0)
    tt = h_ref.shape[0]
    npos = yt_ref.shape[1]
    slot = i % 2

    def fetch(buf):
        def copy(local_row, global_row, rows=ROUTE_CHUNK):
            return pltpu.make_async_copy(y_hbm.at[pl.ds(global_row, rows), :],
                                         yt_ref.at[buf, pl.ds(local_row, rows), :], sem.at[buf])
        return copy

    @pl.when(i == 0)
    def _():
        yt_ref[...] = jnp.zeros(yt_ref.shape, F32)
        _segment_copies(srcc_ref, dstc_ref, nchunk_ref[0], fetch(0))

    @pl.when(i + 1 < n_tiles)
    def _():
        _segment_copies(srcn_ref, dstn_ref, nchunk_ref[jnp.minimum(i + 1, n_tiles - 1)], fetch(1 - slot))

    pos = _tile_plan(e_ref[...], i * tt, n_tok, n_exp)
    top_k = len(pos)
    rows = jnp.concatenate(pos + [gt_ref[...], jnp.zeros((LANES - 2 * top_k, tt), F32)], axis=0)
    cols = rows.T

    _wait_chunks(nchunk_ref[i], lambda rows: fetch(slot)(0, 0, rows))
    f = jnp.zeros((tt, h_ref.shape[1]), F32)
    for c in range(npos // ROUTE_ROWS):
        pio = (lax.broadcasted_iota(I32, (tt, ROUTE_ROWS), 1) + c * ROUTE_ROWS).astype(F32)
        w = functools.reduce(lambda a, b: a + b, [
            jnp.where(pio == cols[:, k:k + 1], cols[:, top_k + k:top_k + k + 1], 0.0) for k in range(top_k)])
        f = f + _dot(w.astype(BF16), yt_ref[slot, c * ROUTE_ROWS:(c + 1) * ROUTE_ROWS, :].astype(BF16))
    out = _layer_norm(alpha * h_ref[...] + f, g_ref[...], b_ref[...])

    @pl.when(i < n_prompt)
    def _():
        op_ref[...] = out

    @pl.when(i == n_prompt)
    def _():
        os_ref[...] = out[:os_ref.shape[0], :]


def _final(y_rows, e_idx, gates, h_all, g, b, tables, *, tp, ts, n_exp, alpha):
    tt = OUT_ROWS
    t_pad, d = h_all.shape
    n_prompt = tp // tt
    npos = TOP_K * tt + n_exp * ROUTE_CHUNK
    const = lambda i, *_: (0, 0)
    kern = functools.partial(_final_kernel, n_tok=tp + ts, n_exp=n_exp, n_prompt=n_prompt, alpha=alpha)
    n_tiles = t_pad // tt
    n_chunks = npos // ROUTE_CHUNK
    list_cur = pl.BlockSpec((None, 1, n_chunks), lambda i, *_: (i, 0, 0), memory_space=pltpu.SMEM)
    list_nxt = pl.BlockSpec((None, 1, n_chunks), lambda i, *_: (jnp.minimum(i + 1, n_tiles - 1), 0, 0),
                            memory_space=pltpu.SMEM)
    grid_spec = pltpu.PrefetchScalarGridSpec(
        num_scalar_prefetch=1,
        grid=(n_tiles,),
        in_specs=[list_cur, list_cur, list_nxt, list_nxt,
                  pl.BlockSpec((TOP_K, tt), lambda i, *_: (0, i)),
                  pl.BlockSpec((TOP_K, tt), lambda i, *_: (0, i)),
                  pl.BlockSpec((tt, d), lambda i, *_: (i, 0)),
                  pl.BlockSpec((1, d), const), pl.BlockSpec((1, d), const),
                  pl.BlockSpec(memory_space=pl.ANY)],
        out_specs=[pl.BlockSpec((tt, d), lambda i, *_: (jnp.minimum(i, n_prompt - 1), 0)),
                   pl.BlockSpec((ts, d), const)],
        scratch_shapes=[pltpu.VMEM((2, npos, d), F32), pltpu.SemaphoreType.DMA((2,))],
    )
    return pl.pallas_call(
        kern,
        grid_spec=grid_spec,
        out_shape=[jax.ShapeDtypeStruct((tp, d), F32), jax.ShapeDtypeStruct((ts, d), F32)],
        compiler_params=_params("arbitrary"),
        name="combine_ln",
    )(tables["nchunk"], tables["src"], tables["dst"], tables["src"], tables["dst"], e_idx, gates, h_all, g, b,
      y_rows)


def _pad_rows(a, rows):
    return jnp.pad(a, ((0, rows - a.shape[0]), (0, 0)))


def kernel(x_prompt, x_sample, cache_k, cache_v, state_ssm_re, state_ssm_im, page_table, w_in, w_out, ssm_a_re, ssm_a_im, ssm_log_dt, ssm_b_re, ssm_b_im, ssm_c_re, ssm_c_im, ssm_d, w_glu, b_glu, lambda_q1, lambda_k1, lambda_q2, lambda_k2, subln_g, ln1_g, ln1_b, w_router, b_router, w_gate_up, b_gate_up, w_down, b_down, ln2_g, ln2_b):
    depth = w_in.shape[0]
    bp, lp, d = x_prompt.shape
    bs, ls, _ = x_sample.shape
    n_phys, page, heads, hd = cache_k.shape[1:]
    assert ls == 1 and hd == LANES
    qk = hd // 2
    aw = heads * hd
    groups, n_state = ssm_a_re.shape[1:]
    sw = groups * ssm_b_re.shape[-1]
    gp = groups * n_state
    n_exp = w_router.shape[2]
    past_len = page_table.shape[1] * page
    alpha = (2 * depth) ** 0.25
    tp = bp * lp
    n_tok = tp + bs

    cos_p, sin_p = _rope_tables(jnp.arange(lp, dtype=I32), qk // 2)
    cos_s, sin_s = _rope_tables(jnp.full((bs,), past_len, I32), qk // 2)
    ck = cache_k.reshape(depth * n_phys * page * heads, hd)
    cv = cache_v.reshape(depth * n_phys * page * heads, hd)

    hp = x_prompt
    hs = x_sample.reshape(1, bs, d)
    outs = [[] for _ in range(8)]
    for l in range(depth):
        lam_init = 0.8 - 0.6 * math.exp(-0.3 * l)
        w_in_bf = w_in[l].astype(BF16)
        w_out_bf = w_out[l].astype(BF16)
        w_glu_bf = w_glu[l].astype(BF16)
        bg = b_glu[l].reshape(1, 2 * sw)
        lam_p = [p[l].reshape(1, qk).astype(F32) for p in (lambda_q1, lambda_k1, lambda_q2, lambda_k2)]
        prm = _ssm_params(ssm_a_re[l], ssm_a_im[l], ssm_log_dt[l], ssm_b_re[l], ssm_b_im[l],
                          ssm_c_re[l], ssm_c_im[l], ssm_d[l])

        tt = min(PROJ_ROWS // bp, lp)
        u_p, q_p, k_p, v_p = _project(hp, w_in_bf, cos_p, sin_p, tt=tt, sw=sw, aw=aw, half=qk // 2)
        zero = jnp.zeros((bp, gp), F32)
        ssm_p, sre_p, sim_p = _ssm(u_p, zero, zero, prm, nb=bp, steps=min(SSM_STEPS, lp), w_glu_bf=w_glu_bf,
                                   b_glu=bg)
        att_p = _attn_prompt(q_p, k_p, v_p, lam_p, subln_g[l].reshape(hd, 1), heads=heads, lam_init=lam_init)

        u_s, q_s, k_s, v_s = _project(hs, w_in_bf, cos_s, sin_s, tt=bs, sw=sw, aw=aw, half=qk // 2)
        ssm_s, sre_s, sim_s = _ssm(u_s, state_ssm_re[l].reshape(bs, gp), state_ssm_im[l].reshape(bs, gp), prm,
                                   nb=bs, steps=1, w_glu_bf=w_glu_bf, b_glu=bg)
        att_s = _attn_sample(q_s.reshape(bs, aw), k_s.reshape(bs, aw), v_s.reshape(bs, aw), ck, cv,
                             page_table + l * n_phys, lam_p, subln_g[l].reshape(1, hd), page=page, heads=heads,
                             lam_init=lam_init)

        wr = w_router[l].T.astype(F32)
        wr_hi = wr.astype(BF16)
        wr_lo = (wr - wr_hi.astype(F32)).astype(BF16)
        h_all, e_idx, gates = _out_route(
            hp.reshape(tp, d), ssm_p.reshape(tp, sw), att_p, _pad_rows(hs.reshape(bs, d), OUT_ROWS),
            _pad_rows(ssm_s, OUT_ROWS), _pad_rows(att_s, OUT_ROWS), w_out_bf, ln1_g[l].reshape(1, d),
            ln1_b[l].reshape(1, d), wr_hi, wr_lo, b_router[l].reshape(n_exp, 1), alpha=alpha)
        tables, block_expert, n_used, n_blocks = _route_tables(e_idx, n_tok, n_exp)
        x_rows = _dispatch(h_all, e_idx, tables, n_tok=n_tok, n_exp=n_exp, n_blocks=n_blocks)
        y_rows = _moe(x_rows, block_expert, n_used, w_gate_up[l], b_gate_up[l], w_down[l], b_down[l])
        y_p, y_s = _final(y_rows, e_idx, gates, h_all, ln2_g[l].reshape(1, d), ln2_b[l].reshape(1, d), tables,
                          tp=tp, ts=bs, n_exp=n_exp, alpha=alpha)
        hp = y_p.reshape(bp, lp, d)
        hs = y_s.reshape(1, bs, d)

        for lst, val in zip(outs, (
                k_p.reshape(bp, lp, heads, hd), v_p.reshape(bp, lp, heads, hd),
                sre_p.reshape(bp, groups, n_state), sim_p.reshape(bp, groups, n_state),
                k_s.reshape(bs, ls, heads, hd), v_s.reshape(bs, ls, heads, hd),
                sre_s.reshape(bs, groups, n_state), sim_s.reshape(bs, groups, n_state))):
            lst.append(val)

    return (hp, hs.reshape(bs, ls, d)) + tuple(jnp.stack(o) for o in outs)
```

```python
import functools
import math

import jax
import jax.numpy as jnp
from jax import lax
from jax.experimental import pallas as pl
from jax.experimental.pallas import tpu as pltpu

F32 = jnp.float32
BF16 = jnp.bfloat16
I32 = jnp.int32

TOP_K = 4
ROPE_THETA = 10000.0
SWIGLU_ALPHA = 1.702
SWIGLU_LIMIT = 7.0
LN_EPS = 1e-5
RMS_EPS = 1e-5
SSM_A_RE_MAX = -1e-4
GELU_C0 = math.sqrt(2.0 / math.pi)
GELU_C1 = 0.044715
NEG_BIG = -1e30
LOG2E = math.log2(math.e)

LANES = 128
SUBLANES = 8
MXU_DIM = 256
VMEM_LIMIT_BYTES = 56 * 1024 * 1024

PROJ_ROWS = 512
SSM_STEPS = 64
ATT_ROWS = 256
ATT_LOOKAHEAD = 4
ATT_PAGE_GROUP = 4
MOE_HIDDEN_CHUNK = 256
MOE_LOOKAHEAD = 2
ROUTE_CHUNK = 8
ROUTE_ROWS = 256
ROUTE_WAIT = 32
OUT_ROWS = 512
MOE_ROWS = 256
SCAN_LANES = 512


def _params(*sem):
    return pltpu.CompilerParams(dimension_semantics=sem, vmem_limit_bytes=VMEM_LIMIT_BYTES)


def _dot(a, b):
    return jnp.dot(a, b, preferred_element_type=F32)


def _dot_nt(a, b):
    return lax.dot_general(a, b, (((1,), (1,)), ((), ())), preferred_element_type=F32)


def _proj_kernel(x_ref, w_ref, cos_ref, sin_ref, u_ref, q_ref, k_ref, v_ref, *, nb, tt, sw, aw, half):
    d = x_ref.shape[-1]
    heads = aw // LANES
    acc = _dot(x_ref[...].reshape(nb * tt, d).astype(BF16), w_ref[...])
    c = cos_ref[...]
    s = sin_ref[...]
    lane = lax.broadcasted_iota(I32, c.shape, 1)
    first = (lane % (2 * half)) < half

    def rope(blk):
        swapped = jnp.where(first, pltpu.roll(blk, LANES - half, 1), pltpu.roll(blk, half, 1))
        return blk * c + swapped * s

    for b in range(nb):
        rows = slice(b * tt, (b + 1) * tt)
        for j in range(sw // LANES):
            u_ref[j, pl.ds(b, tt, stride=nb), :] = acc[rows, j * LANES:(j + 1) * LANES]
        q_ref[b] = jnp.concatenate(
            [rope(acc[rows, sw + j * LANES:sw + (j + 1) * LANES]) for j in range(heads)], axis=1)
        for j in range(heads):
            lo = sw + aw + j * LANES
            k_ref[b, pl.ds(j, tt, stride=heads), :] = rope(acc[rows, lo:lo + LANES])
            lo = sw + 2 * aw + j * LANES
            v_ref[b, pl.ds(j, tt, stride=heads), :] = acc[rows, lo:lo + LANES]


def _project(x3, w_bf, cos_t, sin_t, *, tt, sw, aw, half):
    nb, seq, d = x3.shape
    heads = aw // LANES
    kern = functools.partial(_proj_kernel, nb=nb, tt=tt, sw=sw, aw=aw, half=half)
    return pl.pallas_call(
        kern,
        grid=(seq // tt,),
        in_specs=[
            pl.BlockSpec((nb, tt, d), lambda i: (0, i, 0)),
            pl.BlockSpec(w_bf.shape, lambda i: (0, 0)),
            pl.BlockSpec((tt, LANES), lambda i: (i, 0)),
            pl.BlockSpec((tt, LANES), lambda i: (i, 0)),
        ],
        out_specs=[
            pl.BlockSpec((sw // LANES, tt * nb, LANES), lambda i: (0, i, 0)),
            pl.BlockSpec((nb, tt, aw), lambda i: (0, i, 0)),
            pl.BlockSpec((nb, tt * heads, LANES), lambda i: (0, i, 0)),
            pl.BlockSpec((nb, tt * heads, LANES), lambda i: (0, i, 0)),
        ],
        out_shape=[
            jax.ShapeDtypeStruct((sw // LANES, seq * nb, LANES), F32),
            jax.ShapeDtypeStruct((nb, seq, aw), F32),
            jax.ShapeDtypeStruct((nb, seq * heads, LANES), F32),
            jax.ShapeDtypeStruct((nb, seq * heads, LANES), F32),
        ],
        compiler_params=_params("arbitrary"),
        name="proj_rope",
    )(x3, w_bf, cos_t, sin_t)


def _rope_tables(pos, half):
    inv = ROPE_THETA ** (-jnp.arange(half, dtype=F32) / half)
    ang = pos.astype(F32)[:, None] * inv[None, :]
    reps = LANES // half
    cos = jnp.tile(jnp.cos(ang), (1, reps))
    sin = jnp.sin(ang)
    sin = jnp.tile(jnp.concatenate([-sin, sin], axis=1), (1, reps // 2))
    return cos, sin


def _ssm_kernel(u_ref, s0r_ref, s0i_ref, ar_ref, ai_ref, wb_ref, wc_ref, d_ref, wg_ref, bg_ref,
                o_ref, sr_ref, si_ref, bu_ref, st_ref, ot_ref, *, nb, steps, gp, sw):
    i = pl.program_id(0)
    rows = nb * steps

    @pl.when(i == 0)
    def _():
        st_ref[:, :gp] = s0r_ref[...]
        st_ref[:, gp:] = s0i_ref[...]

    u = jnp.concatenate([u_ref[j] for j in range(sw // LANES)], axis=1)
    ub = u.astype(BF16)
    n_tiles = 2 * gp // MXU_DIM
    per_half = gp // MXU_DIM
    in_tiles = sw // MXU_DIM
    for j in range(n_tiles):
        kt = ((j % per_half) * in_tiles) // per_half
        bu_ref[:, j * MXU_DIM:(j + 1) * MXU_DIM] = _dot(ub[:, kt * MXU_DIM:(kt + 1) * MXU_DIM], wb_ref[j])

    for c in range(gp // SCAN_LANES):
        lo_r = c * SCAN_LANES
        lo_i = gp + c * SCAN_LANES
        ar = jnp.broadcast_to(ar_ref[:, lo_r:lo_r + SCAN_LANES], (nb, SCAN_LANES))
        ai = jnp.broadcast_to(ai_ref[:, lo_r:lo_r + SCAN_LANES], (nb, SCAN_LANES))

        def step(t, carry, lo_r=lo_r, lo_i=lo_i, ar=ar, ai=ai):
            sr, si = carry
            r0 = pl.multiple_of(t * nb, nb)
            br = bu_ref[pl.ds(r0, nb), lo_r:lo_r + SCAN_LANES]
            bi = bu_ref[pl.ds(r0, nb), lo_i:lo_i + SCAN_LANES]
            nr = ar * sr - ai * si + br
            ni = ar * si + ai * sr + bi
            bu_ref[pl.ds(r0, nb), lo_r:lo_r + SCAN_LANES] = nr
            bu_ref[pl.ds(r0, nb), lo_i:lo_i + SCAN_LANES] = ni
            return nr, ni

        sr0 = st_ref[:, lo_r:lo_r + SCAN_LANES]
        si0 = st_ref[:, lo_i:lo_i + SCAN_LANES]
        if steps == 1:
            sr1, si1 = step(0, (sr0, si0))
        else:
            sr1, si1 = lax.fori_loop(0, steps, step, (sr0, si0))
        st_ref[:, lo_r:lo_r + SCAN_LANES] = sr1
        st_ref[:, lo_i:lo_i + SCAN_LANES] = si1

    sr_ref[...] = st_ref[:, :gp]
    si_ref[...] = st_ref[:, gp:]

    out_tiles = sw // MXU_DIM
    kw = gp // out_tiles
    ys = []
    for nt in range(out_tiles):
        s_re = bu_ref[:, nt * kw:(nt + 1) * kw].astype(BF16)
        s_im = bu_ref[:, gp + nt * kw:gp + (nt + 1) * kw].astype(BF16)
        ys.append(_dot(s_re, wc_ref[nt, 0]) + _dot(s_im, wc_ref[nt, 1]))
    y = jnp.concatenate(ys, axis=1) + d_ref[...] * u
    y = 0.5 * y * (1.0 + jnp.tanh(GELU_C0 * (y + GELU_C1 * (y * y * y))))
    z = _dot(y.astype(BF16), wg_ref[...]) + bg_ref[...]
    out = z[:, :sw] * jax.nn.sigmoid(z[:, sw:])
    if steps == 1:
        o_ref[...] = out
    else:
        for j in range(sw // LANES):
            ot_ref[j] = out[:, j * LANES:(j + 1) * LANES]
        for b in range(nb):
            for j in range(sw // LANES):
                o_ref[b, :, j * LANES:(j + 1) * LANES] = ot_ref[j, pl.ds(b, steps, stride=nb), :]


def _ssm(u_tb, s0r, s0i, prm, *, nb, steps, w_glu_bf, b_glu):
    n_slabs, n_rows, _ = u_tb.shape
    sw = n_slabs * LANES
    seq = n_rows // nb
    gp = prm["a_re"].shape[1]
    rows = nb * steps
    grid = (seq // steps,)
    const2 = lambda i: (0, 0)
    kern = functools.partial(_ssm_kernel, nb=nb, steps=steps, gp=gp, sw=sw)
    if steps == 1:
        o_spec = pl.BlockSpec((nb, sw), const2)
        o_shape = jax.ShapeDtypeStruct((nb, sw), F32)
    else:
        o_spec = pl.BlockSpec((nb, steps, sw), lambda i: (0, i, 0))
        o_shape = jax.ShapeDtypeStruct((nb, seq, sw), F32)
    return pl.pallas_call(
        kern,
        grid=grid,
        in_specs=[
            pl.BlockSpec((n_slabs, rows, LANES), lambda i: (0, i, 0)),
            pl.BlockSpec((nb, gp), const2),
            pl.BlockSpec((nb, gp), const2),
            pl.BlockSpec((1, gp), const2),
            pl.BlockSpec((1, gp), const2),
            pl.BlockSpec(prm["wb"].shape, lambda i: (0, 0, 0)),
            pl.BlockSpec(prm["wc"].shape, lambda i: (0, 0, 0, 0)),
            pl.BlockSpec((1, sw), const2),
            pl.BlockSpec(w_glu_bf.shape, const2),
            pl.BlockSpec((1, 2 * sw), const2),
        ],
        out_specs=[o_spec, pl.BlockSpec((nb, gp), const2), pl.BlockSpec((nb, gp), const2)],
        out_shape=[o_shape, jax.ShapeDtypeStruct((nb, gp), F32), jax.ShapeDtypeStruct((nb, gp), F32)],
        scratch_shapes=[
            pltpu.VMEM((rows, 2 * gp), F32),
            pltpu.VMEM((nb, 2 * gp), F32),
            pltpu.VMEM((sw // LANES, rows, LANES), F32),
        ],
        compiler_params=_params("arbitrary"),
        name="ssm_mixer",
    )(u_tb, s0r, s0i, prm["a_re"], prm["a_im"], prm["wb"], prm["wc"], prm["d"], w_glu_bf, b_glu)


def _ssm_params(a_re, a_im, log_dt, b_re, b_im, c_re, c_im, d):
    g_, p_ = a_re.shape
    h_ = b_re.shape[-1]
    gp = g_ * p_
    sw = g_ * h_
    a_re = jnp.minimum(a_re.astype(F32), SSM_A_RE_MAX)
    a_im = a_im.astype(F32)
    dt = jnp.exp(log_dt.astype(F32))[:, None]
    mag = jnp.exp(a_re * dt)
    lb_re = mag * jnp.cos(a_im * dt)
    lb_im = mag * jnp.sin(a_im * dt)
    den = jnp.square(a_re) + jnp.square(a_im)
    nr, ni = lb_re - 1.0, lb_im
    f_re = (nr * a_re + ni * a_im) / den
    f_im = (ni * a_re - nr * a_im) / den
    b_re = b_re.astype(F32)
    b_im = b_im.astype(F32)
    bb_re = f_re[..., None] * b_re - f_im[..., None] * b_im
    bb_im = f_re[..., None] * b_im + f_im[..., None] * b_re
    eye = jnp.eye(g_, dtype=F32)
    w_in_re = jnp.einsum("gph,gk->ghkp", bb_re, eye).reshape(sw, gp)
    w_in_im = jnp.einsum("gph,gk->ghkp", bb_im, eye).reshape(sw, gp)
    w_in = jnp.concatenate([w_in_re, w_in_im], axis=1)
    per_half = gp // MXU_DIM
    in_tiles = sw // MXU_DIM
    n_tiles = 2 * per_half
    kt = [((j % per_half) * in_tiles) // per_half for j in range(n_tiles)]
    w_tiles = w_in.astype(BF16).reshape(in_tiles, MXU_DIM, n_tiles, MXU_DIM).transpose(2, 0, 1, 3)
    wb = w_tiles[jnp.arange(n_tiles), jnp.asarray(kt, I32)]
    w_out_re = jnp.einsum("ghp,gk->gpkh", c_re.astype(F32), eye).reshape(gp, sw)
    w_out_im = -jnp.einsum("ghp,gk->gpkh", c_im.astype(F32), eye).reshape(gp, sw)
    out_tiles = sw // MXU_DIM
    kw = gp // out_tiles
    wc = jnp.stack([
        jnp.stack([w_out_re[nt * kw:(nt + 1) * kw, nt * MXU_DIM:(nt + 1) * MXU_DIM],
                   w_out_im[nt * kw:(nt + 1) * kw, nt * MXU_DIM:(nt + 1) * MXU_DIM]])
        for nt in range(out_tiles)]).astype(BF16)
    return dict(a_re=lb_re.reshape(1, gp), a_im=lb_im.reshape(1, gp), wb=wb, wc=wc,
                d=d.astype(F32).reshape(1, sw))


def _diff_lambda(lq1, lk1, lq2, lk2, lam_init):
    l1 = jnp.exp(jnp.sum(lq1 * lk1, axis=1, keepdims=True))
    l2 = jnp.exp(jnp.sum(lq2 * lk2, axis=1, keepdims=True))
    return l1 - l2 + lam_init


def _head_norm(o, g, lam_init):
    ms = jnp.mean(o * o, axis=-1, keepdims=True)
    return o * lax.rsqrt(ms + RMS_EPS) * g * (1.0 - lam_init)


def _attn_prompt_kernel(q_ref, k_ref, v_ref, lq1_ref, lk1_ref, lq2_ref, lk2_ref, g_ref, o_ref,
                        kb_ref, vt_ref, qs_ref, acc_ref, *, tq, qk, heads, lam_init, scale):
    h = pl.program_id(1)
    nq = vt_ref.shape[0]
    seq = nq * tq
    kb_ref[...] = k_ref[pl.ds(h, seq, stride=heads), :].astype(BF16)
    for jb in range(nq):
        vt_ref[jb] = v_ref[pl.ds(jb * tq * heads + h, tq, stride=heads), :].T.astype(BF16)
    q = q_ref[...] * (scale * LOG2E)
    lane = lax.broadcasted_iota(I32, q.shape, 1)
    qs_ref[0] = jnp.where(lane < qk, q, 0.0).astype(BF16)
    qs_ref[1] = jnp.where(lane >= qk, q, 0.0).astype(BF16)
    acc_ref[...] = jnp.zeros(acc_ref.shape, F32)
    key = lax.broadcasted_iota(I32, (tq, tq), 0)
    qry = lax.broadcasted_iota(I32, (tq, tq), 1)
    keep = key <= qry
    m = [[jnp.full((1, tq), NEG_BIG, F32)] * 2 for _ in range(nq)]
    l = [[jnp.zeros((1, tq), F32)] * 2 for _ in range(nq)]
    tasks = [(j, i, c) for j in range(nq) for i in range(j, nq) for c in range(2)]

    def scores(t):
        j, i, c = tasks[t]
        st = _dot_nt(kb_ref[j * tq:(j + 1) * tq, :], qs_ref[c, i * tq:(i + 1) * tq, :])
        return jnp.where(keep, st, NEG_BIG) if i == j else st

    pending = {t: scores(t) for t in range(min(ATT_LOOKAHEAD, len(tasks)))}
    for t, (j, i, c) in enumerate(tasks):
        if t + ATT_LOOKAHEAD < len(tasks):
            pending[t + ATT_LOOKAHEAD] = scores(t + ATT_LOOKAHEAD)
        st = pending.pop(t)
        m_new = jnp.maximum(m[i][c], jnp.max(st, axis=0, keepdims=True))
        alpha = jnp.exp2(m[i][c] - m_new)
        p = jnp.exp2(st - m_new)
        acc_ref[i, c] = alpha * acc_ref[i, c] + _dot(vt_ref[j], p.astype(BF16))
        l[i][c] = alpha * l[i][c] + jnp.sum(p, axis=0, keepdims=True)
        m[i][c] = m_new
    lam = _diff_lambda(lq1_ref[...], lk1_ref[...], lq2_ref[...], lk2_ref[...], lam_init)
    for i in range(nq):
        ot = acc_ref[i, 0] * (1.0 / l[i][0]) - lam * (acc_ref[i, 1] * (1.0 / l[i][1]))
        ms = jnp.mean(ot * ot, axis=0, keepdims=True)
        ot = ot * lax.rsqrt(ms + RMS_EPS) * g_ref[...] * (1.0 - lam_init)
        o_ref[i * tq:(i + 1) * tq, :] = ot.T


def _attn_prompt(q, k, v, lam_p, g_col, *, heads, lam_init):
    nb, seq, aw = q.shape
    hd = aw // heads
    qk = hd // 2
    tq = min(ATT_ROWS, seq)
    nq = seq // tq
    kern = functools.partial(_attn_prompt_kernel, tq=tq, qk=qk, heads=heads, lam_init=lam_init, scale=qk ** -0.5)
    small = pl.BlockSpec((1, qk), lambda b, h: (0, 0))
    return pl.pallas_call(
        kern,
        grid=(nb, heads),
        in_specs=[
            pl.BlockSpec((None, seq, hd), lambda b, h: (b, 0, h)),
            pl.BlockSpec((None, seq * heads, hd), lambda b, h: (b, 0, 0)),
            pl.BlockSpec((None, seq * heads, hd), lambda b, h: (b, 0, 0)),
            small, small, small, small,
            pl.BlockSpec((hd, 1), lambda b, h: (0, 0)),
        ],
        out_specs=pl.BlockSpec((seq, hd), lambda b, h: (b, h)),
        out_shape=jax.ShapeDtypeStruct((nb * seq, aw), F32),
        scratch_shapes=[pltpu.VMEM((seq, hd), BF16), pltpu.VMEM((nq, hd, tq), BF16),
                        pltpu.VMEM((2, seq, hd), BF16), pltpu.VMEM((nq, 2, hd, tq), F32)],
        compiler_params=_params("arbitrary", "arbitrary"),
        name="diff_attn_prompt",
    )(q, k, v, *lam_p, g_col)


def _attn_sample_kernel(pt_ref, q_ref, ks_ref, vs_ref, lq1_ref, lk1_ref, lq2_ref, lk2_ref, g_ref, ck_hbm, cv_hbm,
                        o_ref, kbuf, vbuf, sem, *, n_pages, page, heads, qk, lam_init, scale):
    b = pl.program_id(0)
    n_seq = pl.num_programs(0)
    slot = b % 2
    hd = 2 * qk
    aw = heads * hd
    n_maps = 2 * heads
    page_rows = page * heads
    seq_rows = n_pages * page_rows

    def page_copies(seq, buf, j):
        src = pl.ds(pl.multiple_of(pt_ref[seq * n_pages + j] * page_rows, page_rows), page_rows)
        dst = pl.ds(buf * seq_rows + j * page_rows, page_rows)
        return (pltpu.make_async_copy(ck_hbm.at[src, :], kbuf.at[dst, :], sem.at[buf]),
                pltpu.make_async_copy(cv_hbm.at[src, :], vbuf.at[dst, :], sem.at[buf]))

    def start_fetch(seq, buf):
        for j in range(n_pages):
            for cp in page_copies(seq, buf, j):
                cp.start()

    @pl.when(b == 0)
    def _():
        start_fetch(0, 0)

    @pl.when(b + 1 < n_seq)
    def _():
        start_fetch(b + 1, 1 - slot)

    q = q_ref[...] * scale
    row = lax.broadcasted_iota(I32, (n_maps, aw), 0)
    lane = lax.broadcasted_iota(I32, (n_maps, aw), 1)
    own = (lane // hd == row % heads) & ((lane // qk) % 2 == row // heads)
    qm = jnp.where(own, jnp.broadcast_to(q, (n_maps, aw)), 0.0)
    qmb = qm.astype(BF16)
    for j in range(n_pages):
        for cp in page_copies(b, slot, j):
            cp.wait()

    def group_rows(buf_ref, g):
        n_tok = min(ATT_PAGE_GROUP, n_pages - g) * page
        base = slot * seq_rows + g * page_rows
        return jnp.concatenate([buf_ref[pl.ds(base + h, n_tok, stride=heads), :] for h in range(heads)],
                               axis=1).astype(BF16)

    groups = range(0, n_pages, ATT_PAGE_GROUP)
    s_pages = [_dot_nt(qmb, group_rows(kbuf, g)) for g in groups]
    s_self = jnp.sum(qm * ks_ref[...], axis=1, keepdims=True)
    m = s_self
    for s in s_pages:
        m = jnp.maximum(m, jnp.max(s, axis=1, keepdims=True))
    p_self = jnp.exp(s_self - m)
    p_pages = [jnp.exp(s - m) for s in s_pages]
    l = p_self
    for p in p_pages:
        l = l + jnp.sum(p, axis=1, keepdims=True)
    inv = 1.0 / l
    lam = _diff_lambda(lq1_ref[...], lk1_ref[...], lq2_ref[...], lk2_ref[...], lam_init)

    def mix(p):
        pn = p * inv
        return pn - lam * pltpu.roll(pn, heads, 0)

    acc = jnp.broadcast_to(mix(p_self), (n_maps, aw)) * vs_ref[...]
    for p, g in zip(p_pages, groups):
        acc = acc + _dot(mix(p).astype(BF16), group_rows(vbuf, g))
    pick = (row < heads) & (lane // hd == row)
    o = jnp.sum(jnp.where(pick, acc, 0.0), axis=0, keepdims=True)
    g = g_ref[...]
    for h in range(heads):
        o_ref[:, h * hd:(h + 1) * hd] = _head_norm(o[:, h * hd:(h + 1) * hd], g, lam_init)


def _attn_sample(q, k_new, v_new, cache_k, cache_v, page_ids, lam_p, g, *, page, heads, lam_init):
    bs, aw = q.shape
    n_pages = page_ids.shape[1]
    hd = aw // heads
    qk = hd // 2
    kern = functools.partial(_attn_sample_kernel, n_pages=n_pages, page=page, heads=heads, qk=qk,
                             lam_init=lam_init, scale=qk ** -0.5)
    row_spec = pl.BlockSpec((None, 1, aw), lambda b, pt: (b, 0, 0))
    small = pl.BlockSpec((1, qk), lambda b, pt: (0, 0))
    buf_rows = 2 * n_pages * page * heads
    grid_spec = pltpu.PrefetchScalarGridSpec(
        num_scalar_prefetch=1,
        grid=(bs,),
        in_specs=[row_spec, row_spec, row_spec, small, small, small, small,
                  pl.BlockSpec((1, hd), lambda b, pt: (0, 0)),
                  pl.BlockSpec(memory_space=pl.ANY), pl.BlockSpec(memory_space=pl.ANY)],
        out_specs=row_spec,
        scratch_shapes=[pltpu.VMEM((buf_rows, hd), F32), pltpu.VMEM((buf_rows, hd), F32),
                        pltpu.SemaphoreType.DMA((2,))],
    )
    out = pl.pallas_call(
        kern,
        grid_spec=grid_spec,
        out_shape=jax.ShapeDtypeStruct((bs, 1, aw), F32),
        compiler_params=_params("arbitrary"),
        name="diff_attn_sample",
    )(page_ids.reshape(-1).astype(I32), q.reshape(bs, 1, aw), k_new.reshape(bs, 1, aw),
      v_new.reshape(bs, 1, aw), *lam_p, g, cache_k, cache_v)
    return out.reshape(bs, aw)


def _layer_norm(z, g, b):
    mu = jnp.mean(z, axis=-1, keepdims=True)
    zc = z - mu
    var = jnp.mean(zc * zc, axis=-1, keepdims=True)
    return zc * lax.rsqrt(var + LN_EPS) * g + b


def _out_kernel(xp_ref, sp_ref, ap_ref, xs_ref, ss_ref, as_ref, wo_ref, g_ref, b_ref, wrh_ref, wrl_ref,
                br_ref, h_ref, e_ref, gt_ref, *, n_prompt, sw, alpha, top_k):
    i = pl.program_id(0)
    is_s = i == n_prompt
    rows = h_ref.shape[0]
    halves = [slice(r, r + rows // 2) for r in (0, rows // 2)]
    mixes = []
    for hs in halves:
        s = jnp.where(is_s, ss_ref[hs, :], sp_ref[hs, :]).astype(BF16)
        a = jnp.where(is_s, as_ref[hs, :], ap_ref[hs, :]).astype(BF16)
        mixes.append(_dot(s, wo_ref[:sw, :]) + _dot(a, wo_ref[sw:, :]))
    for hs, mix in zip(halves, mixes):
        x = jnp.where(is_s, xs_ref[hs, :], xp_ref[hs, :])
        h = _layer_norm(alpha * x + mix, g_ref[...], b_ref[...])
        h_ref[hs, :] = h
        h_hi = h.astype(BF16)
        h_lo = (h - h_hi.astype(F32)).astype(BF16)
        lg = (_dot_nt(wrh_ref[...], h_hi) + _dot_nt(wrl_ref[...], h_hi) + _dot_nt(wrh_ref[...], h_lo)
              + br_ref[...])
        eio = lax.broadcasted_iota(I32, lg.shape, 0)
        n_exp = lg.shape[0]
        vals, idxs = [], []
        for _ in range(top_k):
            m = jnp.max(lg, axis=0, keepdims=True)
            idx = jnp.min(jnp.where(lg == m, eio, n_exp), axis=0, keepdims=True)
            vals.append(m)
            idxs.append(idx)
            lg = jnp.where(eio == idx, NEG_BIG, lg)
        tv = jnp.concatenate(vals, axis=0)
        pe = jnp.exp(tv - tv[0:1, :])
        gt_ref[:, hs] = pe / jnp.sum(pe, axis=0, keepdims=True)
        e_ref[:, hs] = jnp.concatenate(idxs, axis=0)


def _out_route(xp, sp, ap, xs, ss, as_, w_out_bf, g, b, wr_hi, wr_lo, br, *, alpha):
    tp, d = xp.shape
    sw = sp.shape[1]
    rows = OUT_ROWS
    n_prompt = tp // rows
    total = tp + rows
    n_exp = wr_hi.shape[0]
    pidx = lambda i: (jnp.minimum(i, n_prompt - 1), 0)
    const = lambda i: (0, 0)
    kern = functools.partial(_out_kernel, n_prompt=n_prompt, sw=sw, alpha=alpha, top_k=TOP_K)
    return pl.pallas_call(
        kern,
        grid=(n_prompt + 1,),
        in_specs=[
            pl.BlockSpec((rows, d), pidx), pl.BlockSpec((rows, sw), pidx), pl.BlockSpec((rows, d - sw), pidx),
            pl.BlockSpec((rows, d), const), pl.BlockSpec((rows, sw), const), pl.BlockSpec((rows, d - sw), const),
            pl.BlockSpec(w_out_bf.shape, const), pl.BlockSpec((1, d), const), pl.BlockSpec((1, d), const),
            pl.BlockSpec((n_exp, d), const), pl.BlockSpec((n_exp, d), const), pl.BlockSpec((n_exp, 1), const),
        ],
        out_specs=[
            pl.BlockSpec((rows, d), lambda i: (i, 0)),
            pl.BlockSpec((TOP_K, rows), lambda i: (0, i)),
            pl.BlockSpec((TOP_K, rows), lambda i: (0, i)),
        ],
        out_shape=[
            jax.ShapeDtypeStruct((total, d), F32),
            jax.ShapeDtypeStruct((TOP_K, total), I32),
            jax.ShapeDtypeStruct((TOP_K, total), F32),
        ],
        compiler_params=_params("arbitrary"),
        name="out_proj_route",
    )(xp, sp, ap, xs, ss, as_, w_out_bf, g, b, wr_hi, wr_lo, br)


def _tile_plan(e_tile, tile_base, n_tok, n_exp):
    k_, tt = e_tile.shape
    valid = (tile_base + lax.broadcasted_iota(I32, (1, tt), 1)) < n_tok
    eio = lax.broadcasted_iota(I32, (n_exp, tt), 0)
    onehot = [jnp.where((eio == e_tile[k:k + 1, :]) & valid, 1.0, 0.0) for k in range(k_)]
    count_k = [jnp.sum(o, axis=1, keepdims=True) for o in onehot]
    count = functools.reduce(lambda a, b: a + b, count_k)
    padded = jnp.floor((count + (ROUTE_CHUNK - 1)) * (1.0 / ROUTE_CHUNK)) * ROUTE_CHUNK
    r = lax.broadcasted_iota(I32, (n_exp, n_exp), 0)
    c = lax.broadcasted_iota(I32, (n_exp, n_exp), 1)
    padded_row = jnp.sum(jnp.where(r == c, jnp.broadcast_to(padded, (n_exp, n_exp)), 0.0), axis=0, keepdims=True)
    offset = jnp.sum(jnp.where(c < r, jnp.broadcast_to(padded_row, (n_exp, n_exp)), 0.0), axis=1, keepdims=True)
    earlier = (lax.broadcasted_iota(I32, (tt, tt), 0) < lax.broadcasted_iota(I32, (tt, tt), 1)).astype(BF16)
    base = offset
    pos = []
    for k in range(k_):
        before = _dot(onehot[k].astype(BF16), earlier) + base
        pos.append(jnp.where(valid, jnp.sum(onehot[k] * before, axis=0, keepdims=True), -1.0))
        base = base + count_k[k]
    return pos


def _segment_copies(src_ref, dst_ref, n, make_copy):
    def body(c, carry):
        make_copy(pl.multiple_of(src_ref[0, c], ROUTE_CHUNK), pl.multiple_of(dst_ref[0, c], ROUTE_CHUNK)).start()
        return carry

    lax.fori_loop(0, n, body, 0)


def _wait_chunks(n, make_copy):
    def wait_many(c, carry):
        make_copy(ROUTE_WAIT * ROUTE_CHUNK).wait()
        return carry

    def wait_one(c, carry):
        make_copy(ROUTE_CHUNK).wait()
        return carry

    lax.fori_loop(0, n // ROUTE_WAIT, wait_many, 0)
    lax.fori_loop(0, n % ROUTE_WAIT, wait_one, 0)


def _dispatch_kernel(nchunk_ref, zlo_ref, zhi_ref, src_ref, dst_ref, h_ref, e_ref, x_hbm, xt_ref, zero_ref, sem,
                     *, n_tok, n_exp, block_rows):
    i = pl.program_id(0)
    tt = h_ref.shape[0]
    npos = xt_ref.shape[1]

    @pl.when(i == 0)
    def _():
        zero_ref[...] = jnp.zeros(zero_ref.shape, F32)

        def zero_block(b):
            rows0 = pl.multiple_of(b * block_rows, block_rows)
            return pltpu.make_async_copy(zero_ref, x_hbm.at[pl.ds(rows0, block_rows), :], sem)

        def start(b, c):
            zero_block(b).start()
            return c

        def wait(b, c):
            zero_block(b).wait()
            return c

        for e in range(n_exp + 1):
            lax.fori_loop(zlo_ref[e], zhi_ref[e], start, 0)
        for e in range(n_exp + 1):
            lax.fori_loop(zlo_ref[e], zhi_ref[e], wait, 0)

    slot = i % 2
    pos = _tile_plan(e_ref[...], i * tt, n_tok, n_exp)
    hb = h_ref[...].astype(BF16)
    for c in range(npos // ROUTE_ROWS):
        pio = (lax.broadcasted_iota(I32, (ROUTE_ROWS, tt), 0) + c * ROUTE_ROWS).astype(F32)
        sel = functools.reduce(lambda a, b: a | b, [pio == p for p in pos])
        xt_ref[slot, c * ROUTE_ROWS:(c + 1) * ROUTE_ROWS, :] = _dot(jnp.where(sel, 1.0, 0.0).astype(BF16), hb)

    def copy(src_row, dst_row, rows=ROUTE_CHUNK):
        return pltpu.make_async_copy(xt_ref.at[slot, pl.ds(src_row, rows), :], x_hbm.at[pl.ds(dst_row, rows), :], sem)

    @pl.when(i > 0)
    def _():
        _wait_chunks(nchunk_ref[i - 1], lambda rows: copy(0, 0, rows))

    _segment_copies(src_ref, dst_ref, nchunk_ref[i], copy)

    @pl.when(i == pl.num_programs(0) - 1)
    def _():
        _wait_chunks(nchunk_ref[i], lambda rows: copy(0, 0, rows))


def _dispatch(h_all, e_idx, tables, *, n_tok, n_exp, n_blocks):
    t_pad, d = h_all.shape
    tt = OUT_ROWS
    npos = TOP_K * tt + n_exp * ROUTE_CHUNK
    kern = functools.partial(_dispatch_kernel, n_tok=n_tok, n_exp=n_exp, block_rows=MOE_ROWS)
    chunk_list = pl.BlockSpec((None, 1, npos // ROUTE_CHUNK), lambda i, *_: (i, 0, 0), memory_space=pltpu.SMEM)
    grid_spec = pltpu.PrefetchScalarGridSpec(
        num_scalar_prefetch=3,
        grid=(t_pad // tt,),
        in_specs=[chunk_list, chunk_list,
                  pl.BlockSpec((tt, d), lambda i, *_: (i, 0)),
                  pl.BlockSpec((TOP_K, tt), lambda i, *_: (0, i))],
        out_specs=pl.BlockSpec(memory_space=pl.ANY),
        scratch_shapes=[pltpu.VMEM((2, npos, d), F32), pltpu.VMEM((MOE_ROWS, d), F32), pltpu.SemaphoreType.DMA],
    )
    return pl.pallas_call(
        kern,
        grid_spec=grid_spec,
        out_shape=jax.ShapeDtypeStruct((n_blocks * MOE_ROWS, d), F32),
        compiler_params=_params("arbitrary"),
        name="moe_dispatch",
    )(tables["nchunk"], tables["zlo"], tables["zhi"], tables["src"], tables["dst"], h_all, e_idx)


def _moe_kernel(be_ref, nu_ref, slot_ref, next_ref, x_ref, wgu_hbm, bgu_ref, wdn_hbm, bdn_ref, y_ref,
                wgu_f, wdn_f, wgu_bf, wdn_bf, wsem, *, de):
    i = pl.program_id(0)

    def fetch(e, slot):
        return (pltpu.make_async_copy(wgu_hbm.at[e], wgu_f.at[slot], wsem.at[slot]),
                pltpu.make_async_copy(wdn_hbm.at[e], wdn_f.at[slot], wsem.at[slot]))

    @pl.when(i >= nu_ref[0])
    def _():
        y_ref[...] = jnp.zeros(y_ref.shape, F32)

    @pl.when(i < nu_ref[0])
    def _():
        slot = slot_ref[i]

        @pl.when(i == 0)
        def _():
            for cp in fetch(be_ref[0], 0):
                cp.start()

        @pl.when((i == 0) | (be_ref[i] != be_ref[jnp.maximum(i - 1, 0)]))
        def _():
            for cp in fetch(be_ref[i], slot):
                cp.wait()
            wgu_bf[...] = wgu_f[slot].astype(BF16)
            wdn_bf[...] = wdn_f[slot].astype(BF16)

            @pl.when(next_ref[i] >= 0)
            def _():
                for cp in fetch(next_ref[i], 1 - slot):
                    cp.start()

        x = x_ref[...].astype(BF16)
        cw = MOE_HIDDEN_CHUNK
        n_chunks = de // cw

        def up(c):
            lo, hi = c * cw, (c + 1) * cw
            return (_dot(x, wgu_bf[:, lo:hi]) + bgu_ref[:, lo:hi],
                    _dot(x, wgu_bf[:, de + lo:de + hi]) + bgu_ref[:, de + lo:de + hi])

        pending = {c: up(c) for c in range(min(MOE_LOOKAHEAD, n_chunks))}
        y = bdn_ref[...]
        for c in range(n_chunks):
            if c + MOE_LOOKAHEAD < n_chunks:
                pending[c + MOE_LOOKAHEAD] = up(c + MOE_LOOKAHEAD)
            glu, lin = pending.pop(c)
            glu = jnp.minimum(glu, SWIGLU_LIMIT)
            lin = jnp.clip(lin, -SWIGLU_LIMIT, SWIGLU_LIMIT)
            act = glu * jax.nn.sigmoid(SWIGLU_ALPHA * glu) * (lin + 1.0)
            y = y + _dot(act.astype(BF16), wdn_bf[c * cw:(c + 1) * cw, :])
        y_ref[...] = y


def _moe(x_rows, block_expert, n_used, w_gu, b_gu, w_dn, b_dn):
    rows = MOE_ROWS
    n_blocks = block_expert.shape[0]
    n_exp, d, de2 = w_gu.shape
    de = de2 // 2
    kern = functools.partial(_moe_kernel, de=de)
    idx = jnp.arange(n_blocks, dtype=I32)
    in_use = idx < n_used[0]
    changed = jnp.concatenate([jnp.ones((1,), bool), block_expert[1:] != block_expert[:-1]])
    w_slot = ((jnp.cumsum(changed.astype(I32)) - 1) % 2).astype(I32)
    later = (block_expert[None, :] > block_expert[:, None]) & in_use[None, :]
    next_e = jnp.min(jnp.where(later, block_expert[None, :], n_exp), axis=1)
    next_e = jnp.where(next_e < n_exp, next_e, -1).astype(I32)

    def used(i, nu):
        return jnp.minimum(i, nu[0] - 1)

    def expert(i, be, nu, *_):
        return (be[used(i, nu)], 0, 0)

    grid_spec = pltpu.PrefetchScalarGridSpec(
        num_scalar_prefetch=4,
        grid=(n_blocks,),
        in_specs=[
            pl.BlockSpec((rows, d), lambda i, be, nu, *_: (used(i, nu), 0)),
            pl.BlockSpec(memory_space=pl.ANY),
            pl.BlockSpec((None, 1, de2), expert),
            pl.BlockSpec(memory_space=pl.ANY),
            pl.BlockSpec((None, 1, d), expert),
        ],
        out_specs=pl.BlockSpec((rows, d), lambda i, *_: (i, 0)),
        scratch_shapes=[pltpu.VMEM((2, d, de2), F32), pltpu.VMEM((2, de, d), F32),
                        pltpu.VMEM((d, de2), BF16), pltpu.VMEM((de, d), BF16), pltpu.SemaphoreType.DMA((2,))],
    )
    return pl.pallas_call(
        kern,
        grid_spec=grid_spec,
        out_shape=jax.ShapeDtypeStruct(x_rows.shape, F32),
        compiler_params=_params("arbitrary"),
        name="moe_experts",
    )(block_expert, n_used, w_slot, next_e, x_rows, w_gu, b_gu.reshape(n_exp, 1, de2), w_dn,
      b_dn.reshape(n_exp, 1, d))


def _route_tables(e_idx, n_tok, n_exp):
    k_, t_pad = e_idx.shape
    tt = OUT_ROWS
    rows = MOE_ROWS
    tiles = t_pad // tt
    valid = jnp.arange(t_pad, dtype=I32) < n_tok
    e_m = jnp.where(valid[None, :], e_idx, -1).reshape(k_, tiles, tt, 1)
    cnt = jnp.sum((e_m == jnp.arange(n_exp, dtype=I32)).astype(I32), axis=(0, 2))
    cpad = ((cnt + ROUTE_CHUNK - 1) // ROUTE_CHUNK) * ROUTE_CHUNK
    off = jnp.cumsum(cpad, axis=1) - cpad
    total = jnp.sum(cpad, axis=0)
    region = ((total + rows - 1) // rows) * rows
    pend = jnp.cumsum(region)
    pstart = pend - region
    gbase = pstart[None, :] + jnp.cumsum(cpad, axis=0) - cpad
    n_blocks = -(-(k_ * n_tok + n_exp * (tiles * (ROUTE_CHUNK - 1) + rows - 1)) // rows)
    block_start = jnp.arange(n_blocks, dtype=I32) * rows
    block_expert = jnp.minimum(jnp.sum((pend[None, :] <= block_start[:, None]).astype(I32), axis=1), n_exp - 1)
    n_used = (pend[-1] // rows).astype(I32)
    zlo = jnp.concatenate([(pstart + total) // rows, n_used[None]]).astype(I32)
    zhi = jnp.concatenate([pend // rows, jnp.full((1,), n_blocks, I32)]).astype(I32)
    nch = cpad // ROUTE_CHUNK
    cum = jnp.cumsum(nch, axis=1)
    c_id = jnp.arange((k_ * tt + n_exp * ROUTE_CHUNK) // ROUTE_CHUNK, dtype=I32)
    owner = jnp.sum((cum[:, None, :] <= c_id[None, :, None]).astype(I32), axis=2)
    pick = (jnp.minimum(owner, n_exp - 1)[:, :, None] == jnp.arange(n_exp, dtype=I32)).astype(I32)
    within = (c_id[None, :] - jnp.sum(pick * (cum - nch)[:, None, :], axis=2)) * ROUTE_CHUNK
    src = jnp.sum(pick * off[:, None, :], axis=2) + within
    dst = jnp.sum(pick * gbase[:, None, :], axis=2) + within
    tables = dict(nchunk=cum[:, -1].astype(I32), src=src[:, None, :].astype(I32), dst=dst[:, None, :].astype(I32),
                  zlo=zlo, zhi=zhi)
    return tables, block_expert.astype(I32), n_used.reshape(1), n_blocks


def _final_kernel(nchunk_ref, srcc_ref, dstc_ref, srcn_ref, dstn_ref, e_ref, gt_ref, h_ref, g_ref, b_ref, y_hbm,
                  op_ref, os_ref, yt_ref, sem, *, n_tok, n_exp, n_prompt, alpha):
    i = pl.program_id(0)
    n_tiles = pl.num_programs(0)
    tt = h_ref.shape[0]
    npos = yt_ref.shape[1]
    slot = i % 2

    def fetch(buf):
        def copy(local_row, global_row, rows=ROUTE_CHUNK):
            return pltpu.make_async_copy(y_hbm.at[pl.ds(global_row, rows), :],
                                         yt_ref.at[buf, pl.ds(local_row, rows), :], sem.at[buf])
        return copy

    @pl.when(i == 0)
    def _():
        yt_ref[...] = jnp.zeros(yt_ref.shape, F32)
        _segment_copies(srcc_ref, dstc_ref, nchunk_ref[0], fetch(0))

    @pl.when(i + 1 < n_tiles)
    def _():
        _segment_copies(srcn_ref, dstn_ref, nchunk_ref[jnp.minimum(i + 1, n_tiles - 1)], fetch(1 - slot))

    pos = _tile_plan(e_ref[...], i * tt, n_tok, n_exp)
    top_k = len(pos)
    rows = jnp.concatenate(pos + [gt_ref[...], jnp.zeros((LANES - 2 * top_k, tt), F32)], axis=0)
    cols = rows.T

    _wait_chunks(nchunk_ref[i], lambda rows: fetch(slot)(0, 0, rows))
    f = jnp.zeros((tt, h_ref.shape[1]), F32)
    for c in range(npos // ROUTE_ROWS):
        pio = (lax.broadcasted_iota(I32, (tt, ROUTE_ROWS), 1) + c * ROUTE_ROWS).astype(F32)
        w = functools.reduce(lambda a, b: a + b, [
            jnp.where(pio == cols[:, k:k + 1], cols[:, top_k + k:top_k + k + 1], 0.0) for k in range(top_k)])
        f = f + _dot(w.astype(BF16), yt_ref[slot, c * ROUTE_ROWS:(c + 1) * ROUTE_ROWS, :].astype(BF16))
    out = _layer_norm(alpha * h_ref[...] + f, g_ref[...], b_ref[...])

    @pl.when(i < n_prompt)
    def _():
        op_ref[...] = out

    @pl.when(i == n_prompt)
    def _():
        os_ref[...] = out[:os_ref.shape[0], :]


def _final(y_rows, e_idx, gates, h_all, g, b, tables, *, tp, ts, n_exp, alpha):
    tt = OUT_ROWS
    t_pad, d = h_all.shape
    n_prompt = tp // tt
    npos = TOP_K * tt + n_exp * ROUTE_CHUNK
    const = lambda i, *_: (0, 0)
    kern = functools.partial(_final_kernel, n_tok=tp + ts, n_exp=n_exp, n_prompt=n_prompt, alpha=alpha)
    n_tiles = t_pad // tt
    n_chunks = npos // ROUTE_CHUNK
    list_cur = pl.BlockSpec((None, 1, n_chunks), lambda i, *_: (i, 0, 0), memory_space=pltpu.SMEM)
    list_nxt = pl.BlockSpec((None, 1, n_chunks), lambda i, *_: (jnp.minimum(i + 1, n_tiles - 1), 0, 0),
                            memory_space=pltpu.SMEM)
    grid_spec = pltpu.PrefetchScalarGridSpec(
        num_scalar_prefetch=1,
        grid=(n_tiles,),
        in_specs=[list_cur, list_cur, list_nxt, list_nxt,
                  pl.BlockSpec((TOP_K, tt), lambda i, *_: (0, i)),
                  pl.BlockSpec((TOP_K, tt), lambda i, *_: (0, i)),
                  pl.BlockSpec((tt, d), lambda i, *_: (i, 0)),
                  pl.BlockSpec((1, d), const), pl.BlockSpec((1, d), const),
                  pl.BlockSpec(memory_space=pl.ANY)],
        out_specs=[pl.BlockSpec((tt, d), lambda i, *_: (jnp.minimum(i, n_prompt - 1), 0)),
                   pl.BlockSpec((ts, d), const)],
        scratch_shapes=[pltpu.VMEM((2, npos, d), F32), pltpu.SemaphoreType.DMA((2,))],
    )
    return pl.pallas_call(
        kern,
        grid_spec=grid_spec,
        out_shape=[jax.ShapeDtypeStruct((tp, d), F32), jax.ShapeDtypeStruct((ts, d), F32)],
        compiler_params=_params("arbitrary"),
        name="combine_ln",
    )(tables["nchunk"], tables["src"], tables["dst"], tables["src"], tables["dst"], e_idx, gates, h_all, g, b,
      y_rows)


def _pad_rows(a, rows):
    return jnp.pad(a, ((0, rows - a.shape[0]), (0, 0)))


def kernel(x_prompt, x_sample, cache_k, cache_v, state_ssm_re, state_ssm_im, page_table, w_in, w_out, ssm_a_re, ssm_a_im, ssm_log_dt, ssm_b_re, ssm_b_im, ssm_c_re, ssm_c_im, ssm_d, w_glu, b_glu, lambda_q1, lambda_k1, lambda_q2, lambda_k2, subln_g, ln1_g, ln1_b, w_router, b_router, w_gate_up, b_gate_up, w_down, b_down, ln2_g, ln2_b):
    depth = w_in.shape[0]
    bp, lp, d = x_prompt.shape
    bs, ls, _ = x_sample.shape
    n_phys, page, heads, hd = cache_k.shape[1:]
    assert ls == 1 and hd == LANES
    qk = hd // 2
    aw = heads * hd
    groups, n_state = ssm_a_re.shape[1:]
    sw = groups * ssm_b_re.shape[-1]
    gp = groups * n_state
    n_exp = w_router.shape[2]
    past_len = page_table.shape[1] * page
    alpha = (2 * depth) ** 0.25
    tp = bp * lp
    n_tok = tp + bs

    cos_p, sin_p = _rope_tables(jnp.arange(lp, dtype=I32), qk // 2)
    cos_s, sin_s = _rope_tables(jnp.full((bs,), past_len, I32), qk // 2)
    ck = cache_k.reshape(depth * n_phys * page * heads, hd)
    cv = cache_v.reshape(depth * n_phys * page * heads, hd)

    hp = x_prompt
    hs = x_sample.reshape(1, bs, d)
    outs = [[] for _ in range(8)]
    for l in range(depth):
        lam_init = 0.8 - 0.6 * math.exp(-0.3 * l)
        w_in_bf = w_in[l].astype(BF16)
        w_out_bf = w_out[l].astype(BF16)
        w_glu_bf = w_glu[l].astype(BF16)
        bg = b_glu[l].reshape(1, 2 * sw)
        lam_p = [p[l].reshape(1, qk).astype(F32) for p in (lambda_q1, lambda_k1, lambda_q2, lambda_k2)]
        prm = _ssm_params(ssm_a_re[l], ssm_a_im[l], ssm_log_dt[l], ssm_b_re[l], ssm_b_im[l],
                          ssm_c_re[l], ssm_c_im[l], ssm_d[l])

        tt = min(PROJ_ROWS // bp, lp)
        u_p, q_p, k_p, v_p = _project(hp, w_in_bf, cos_p, sin_p, tt=tt, sw=sw, aw=aw, half=qk // 2)
        zero = jnp.zeros((bp, gp), F32)
        ssm_p, sre_p, sim_p = _ssm(u_p, zero, zero, prm, nb=bp, steps=min(SSM_STEPS, lp), w_glu_bf=w_glu_bf,
                                   b_glu=bg)
        att_p = _attn_prompt(q_p, k_p, v_p, lam_p, subln_g[l].reshape(hd, 1), heads=heads, lam_init=lam_init)

        u_s, q_s, k_s, v_s = _project(hs, w_in_bf, cos_s, sin_s, tt=bs, sw=sw, aw=aw, half=qk // 2)
        ssm_s, sre_s, sim_s = _ssm(u_s, state_ssm_re[l].reshape(bs, gp), state_ssm_im[l].reshape(bs, gp), prm,
                                   nb=bs, steps=1, w_glu_bf=w_glu_bf, b_glu=bg)
        att_s = _attn_sample(q_s.reshape(bs, aw), k_s.reshape(bs, aw), v_s.reshape(bs, aw), ck, cv,
                             page_table + l * n_phys, lam_p, subln_g[l].reshape(1, hd), page=page, heads=heads,
                             lam_init=lam_init)

        wr = w_router[l].T.astype(F32)
        wr_hi = wr.astype(BF16)
        wr_lo = (wr - wr_hi.astype(F32)).astype(BF16)
        h_all, e_idx, gates = _out_route(
            hp.reshape(tp, d), ssm_p.reshape(tp, sw), att_p, _pad_rows(hs.reshape(bs, d), OUT_ROWS),
            _pad_rows(ssm_s, OUT_ROWS), _pad_rows(att_s, OUT_ROWS), w_out_bf, ln1_g[l].reshape(1, d),
            ln1_b[l].reshape(1, d), wr_hi, wr_lo, b_router[l].reshape(n_exp, 1), alpha=alpha)
        tables, block_expert, n_used, n_blocks = _route_tables(e_idx, n_tok, n_exp)
        x_rows = _dispatch(h_all, e_idx, tables, n_tok=n_tok, n_exp=n_exp, n_blocks=n_blocks)
        y_rows = _moe(x_rows, block_expert, n_used, w_gate_up[l], b_gate_up[l], w_down[l], b_down[l])
        y_p, y_s = _final(y_rows, e_idx, gates, h_all, ln2_g[l].reshape(1, d), ln2_b[l].reshape(1, d), tables,
                          tp=tp, ts=bs, n_exp=n_exp, alpha=alpha)
        hp = y_p.reshape(bp, lp, d)
        hs = y_s.reshape(1, bs, d)

        for lst, val in zip(outs, (
                k_p.reshape(bp, lp, heads, hd), v_p.reshape(bp, lp, heads, hd),
                sre_p.reshape(bp, groups, n_state), sim_p.reshape(bp, groups, n_state),
                k_s.reshape(bs, ls, heads, hd), v_s.reshape(bs, ls, heads, hd),
                sre_s.reshape(bs, groups, n_state), sim_s.reshape(bs, groups, n_state))):
            lst.append(val)

    return (hp, hs.reshape(bs, ls, d)) + tuple(jnp.stack(o) for o in outs)
```

```python
import functools
import math

import jax
import jax.numpy as jnp
from jax import lax
from jax.experimental import pallas as pl
from jax.experimental.pallas import tpu as pltpu

F32 = jnp.float32
BF16 = jnp.bfloat16
I32 = jnp.int32

TOP_K = 4
ROPE_THETA = 10000.0
SWIGLU_ALPHA = 1.702
SWIGLU_LIMIT = 7.0
LN_EPS = 1e-5
RMS_EPS = 1e-5
SSM_A_RE_MAX = -1e-4
GELU_C0 = math.sqrt(2.0 / math.pi)
GELU_C1 = 0.044715
NEG_BIG = -1e30
LOG2E = math.log2(math.e)

LANES = 128
SUBLANES = 8
MXU_DIM = 256
VMEM_LIMIT_BYTES = 56 * 1024 * 1024

PROJ_ROWS = 512
SSM_STEPS = 64
ATT_ROWS = 256
ATT_LOOKAHEAD = 4
ATT_PAGE_GROUP = 4
MOE_HIDDEN_CHUNK = 256
MOE_LOOKAHEAD = 3
ROUTE_CHUNK = 8
ROUTE_ROWS = 256
ROUTE_WAIT = 32
OUT_ROWS = 512
MOE_ROWS = 256
SCAN_LANES = 512


def _params(*sem):
    return pltpu.CompilerParams(dimension_semantics=sem, vmem_limit_bytes=VMEM_LIMIT_BYTES)


def _dot(a, b):
    return jnp.dot(a, b, preferred_element_type=F32)


def _dot_nt(a, b):
    return lax.dot_general(a, b, (((1,), (1,)), ((), ())), preferred_element_type=F32)


def _proj_kernel(x_ref, w_ref, cos_ref, sin_ref, u_ref, q_ref, k_ref, v_ref, *, nb, tt, sw, aw, half):
    d = x_ref.shape[-1]
    heads = aw // LANES
    acc = _dot(x_ref[...].reshape(nb * tt, d).astype(BF16), w_ref[...])
    c = cos_ref[...]
    s = sin_ref[...]
    lane = lax.broadcasted_iota(I32, c.shape, 1)
    first = (lane % (2 * half)) < half

    def rope(blk):
        swapped = jnp.where(first, pltpu.roll(blk, LANES - half, 1), pltpu.roll(blk, half, 1))
        return blk * c + swapped * s

    for b in range(nb):
        rows = slice(b * tt, (b + 1) * tt)
        for j in range(sw // LANES):
            u_ref[j, pl.ds(b, tt, stride=nb), :] = acc[rows, j * LANES:(j + 1) * LANES]
        q_ref[b] = jnp.concatenate(
            [rope(acc[rows, sw + j * LANES:sw + (j + 1) * LANES]) for j in range(heads)], axis=1)
        for j in range(heads):
            lo = sw + aw + j * LANES
            k_ref[b, pl.ds(j, tt, stride=heads), :] = rope(acc[rows, lo:lo + LANES])
            lo = sw + 2 * aw + j * LANES
            v_ref[b, pl.ds(j, tt, stride=heads), :] = acc[rows, lo:lo + LANES]


def _project(x3, w_bf, cos_t, sin_t, *, tt, sw, aw, half):
    nb, seq, d = x3.shape
    heads = aw // LANES
    kern = functools.partial(_proj_kernel, nb=nb, tt=tt, sw=sw, aw=aw, half=half)
    return pl.pallas_call(
        kern,
        grid=(seq // tt,),
        in_specs=[
            pl.BlockSpec((nb, tt, d), lambda i: (0, i, 0)),
            pl.BlockSpec(w_bf.shape, lambda i: (0, 0)),
            pl.BlockSpec((tt, LANES), lambda i: (i, 0)),
            pl.BlockSpec((tt, LANES), lambda i: (i, 0)),
        ],
        out_specs=[
            pl.BlockSpec((sw // LANES, tt * nb, LANES), lambda i: (0, i, 0)),
            pl.BlockSpec((nb, tt, aw), lambda i: (0, i, 0)),
            pl.BlockSpec((nb, tt * heads, LANES), lambda i: (0, i, 0)),
            pl.BlockSpec((nb, tt * heads, LANES), lambda i: (0, i, 0)),
        ],
        out_shape=[
            jax.ShapeDtypeStruct((sw // LANES, seq * nb, LANES), F32),
            jax.ShapeDtypeStruct((nb, seq, aw), F32),
            jax.ShapeDtypeStruct((nb, seq * heads, LANES), F32),
            jax.ShapeDtypeStruct((nb, seq * heads, LANES), F32),
        ],
        compiler_params=_params("arbitrary"),
        name="proj_rope",
    )(x3, w_bf, cos_t, sin_t)


def _rope_tables(pos, half):
    inv = ROPE_THETA ** (-jnp.arange(half, dtype=F32) / half)
    ang = pos.astype(F32)[:, None] * inv[None, :]
    reps = LANES // half
    cos = jnp.tile(jnp.cos(ang), (1, reps))
    sin = jnp.sin(ang)
    sin = jnp.tile(jnp.concatenate([-sin, sin], axis=1), (1, reps // 2))
    return cos, sin


def _ssm_kernel(u_ref, s0r_ref, s0i_ref, ar_ref, ai_ref, wb_ref, wc_ref, d_ref, wg_ref, bg_ref,
                o_ref, sr_ref, si_ref, bu_ref, st_ref, ot_ref, *, nb, steps, gp, sw):
    i = pl.program_id(0)
    rows = nb * steps

    @pl.when(i == 0)
    def _():
        st_ref[:, :gp] = s0r_ref[...]
        st_ref[:, gp:] = s0i_ref[...]

    u = jnp.concatenate([u_ref[j] for j in range(sw // LANES)], axis=1)
    ub = u.astype(BF16)
    n_tiles = 2 * gp // MXU_DIM
    per_half = gp // MXU_DIM
    in_tiles = sw // MXU_DIM
    for j in range(n_tiles):
        kt = ((j % per_half) * in_tiles) // per_half
        bu_ref[:, j * MXU_DIM:(j + 1) * MXU_DIM] = _dot(ub[:, kt * MXU_DIM:(kt + 1) * MXU_DIM], wb_ref[j])

    for c in range(gp // SCAN_LANES):
        lo_r = c * SCAN_LANES
        lo_i = gp + c * SCAN_LANES
        ar = jnp.broadcast_to(ar_ref[:, lo_r:lo_r + SCAN_LANES], (nb, SCAN_LANES))
        ai = jnp.broadcast_to(ai_ref[:, lo_r:lo_r + SCAN_LANES], (nb, SCAN_LANES))

        def step(t, carry, lo_r=lo_r, lo_i=lo_i, ar=ar, ai=ai):
            sr, si = carry
            r0 = pl.multiple_of(t * nb, nb)
            br = bu_ref[pl.ds(r0, nb), lo_r:lo_r + SCAN_LANES]
            bi = bu_ref[pl.ds(r0, nb), lo_i:lo_i + SCAN_LANES]
            nr = ar * sr - ai * si + br
            ni = ar * si + ai * sr + bi
            bu_ref[pl.ds(r0, nb), lo_r:lo_r + SCAN_LANES] = nr
            bu_ref[pl.ds(r0, nb), lo_i:lo_i + SCAN_LANES] = ni
            return nr, ni

        sr0 = st_ref[:, lo_r:lo_r + SCAN_LANES]
        si0 = st_ref[:, lo_i:lo_i + SCAN_LANES]
        if steps == 1:
            sr1, si1 = step(0, (sr0, si0))
        else:
            sr1, si1 = lax.fori_loop(0, steps, step, (sr0, si0))
        st_ref[:, lo_r:lo_r + SCAN_LANES] = sr1
        st_ref[:, lo_i:lo_i + SCAN_LANES] = si1

    sr_ref[...] = st_ref[:, :gp]
    si_ref[...] = st_ref[:, gp:]

    out_tiles = sw // MXU_DIM
    kw = gp // out_tiles
    ys = []
    for nt in range(out_tiles):
        s_re = bu_ref[:, nt * kw:(nt + 1) * kw].astype(BF16)
        s_im = bu_ref[:, gp + nt * kw:gp + (nt + 1) * kw].astype(BF16)
        ys.append(_dot(s_re, wc_ref[nt, 0]) + _dot(s_im, wc_ref[nt, 1]))
    y = jnp.concatenate(ys, axis=1) + d_ref[...] * u
    y = 0.5 * y * (1.0 + jnp.tanh(GELU_C0 * (y + GELU_C1 * (y * y * y))))
    z = _dot(y.astype(BF16), wg_ref[...]) + bg_ref[...]
    out = z[:, :sw] * jax.nn.sigmoid(z[:, sw:])
    if steps == 1:
        o_ref[...] = out
    else:
        for j in range(sw // LANES):
            ot_ref[j] = out[:, j * LANES:(j + 1) * LANES]
        for b in range(nb):
            for j in range(sw // LANES):
                o_ref[b, :, j * LANES:(j + 1) * LANES] = ot_ref[j, pl.ds(b, steps, stride=nb), :]


def _ssm(u_tb, s0r, s0i, prm, *, nb, steps, w_glu_bf, b_glu):
    n_slabs, n_rows, _ = u_tb.shape
    sw = n_slabs * LANES
    seq = n_rows // nb
    gp = prm["a_re"].shape[1]
    rows = nb * steps
    grid = (seq // steps,)
    const2 = lambda i: (0, 0)
    kern = functools.partial(_ssm_kernel, nb=nb, steps=steps, gp=gp, sw=sw)
    if steps == 1:
        o_spec = pl.BlockSpec((nb, sw), const2)
        o_shape = jax.ShapeDtypeStruct((nb, sw), F32)
    else:
        o_spec = pl.BlockSpec((nb, steps, sw), lambda i: (0, i, 0))
        o_shape = jax.ShapeDtypeStruct((nb, seq, sw), F32)
    return pl.pallas_call(
        kern,
        grid=grid,
        in_specs=[
            pl.BlockSpec((n_slabs, rows, LANES), lambda i: (0, i, 0)),
            pl.BlockSpec((nb, gp), const2),
            pl.BlockSpec((nb, gp), const2),
            pl.BlockSpec((1, gp), const2),
            pl.BlockSpec((1, gp), const2),
            pl.BlockSpec(prm["wb"].shape, lambda i: (0, 0, 0)),
            pl.BlockSpec(prm["wc"].shape, lambda i: (0, 0, 0, 0)),
            pl.BlockSpec((1, sw), const2),
            pl.BlockSpec(w_glu_bf.shape, const2),
            pl.BlockSpec((1, 2 * sw), const2),
        ],
        out_specs=[o_spec, pl.BlockSpec((nb, gp), const2), pl.BlockSpec((nb, gp), const2)],
        out_shape=[o_shape, jax.ShapeDtypeStruct((nb, gp), F32), jax.ShapeDtypeStruct((nb, gp), F32)],
        scratch_shapes=[
            pltpu.VMEM((rows, 2 * gp), F32),
            pltpu.VMEM((nb, 2 * gp), F32),
            pltpu.VMEM((sw // LANES, rows, LANES), F32),
        ],
        compiler_params=_params("arbitrary"),
        name="ssm_mixer",
    )(u_tb, s0r, s0i, prm["a_re"], prm["a_im"], prm["wb"], prm["wc"], prm["d"], w_glu_bf, b_glu)


def _ssm_params(a_re, a_im, log_dt, b_re, b_im, c_re, c_im, d):
    g_, p_ = a_re.shape
    h_ = b_re.shape[-1]
    gp = g_ * p_
    sw = g_ * h_
    a_re = jnp.minimum(a_re.astype(F32), SSM_A_RE_MAX)
    a_im = a_im.astype(F32)
    dt = jnp.exp(log_dt.astype(F32))[:, None]
    mag = jnp.exp(a_re * dt)
    lb_re = mag * jnp.cos(a_im * dt)
    lb_im = mag * jnp.sin(a_im * dt)
    den = jnp.square(a_re) + jnp.square(a_im)
    nr, ni = lb_re - 1.0, lb_im
    f_re = (nr * a_re + ni * a_im) / den
    f_im = (ni * a_re - nr * a_im) / den
    b_re = b_re.astype(F32)
    b_im = b_im.astype(F32)
    bb_re = f_re[..., None] * b_re - f_im[..., None] * b_im
    bb_im = f_re[..., None] * b_im + f_im[..., None] * b_re
    eye = jnp.eye(g_, dtype=F32)
    w_in_re = jnp.einsum("gph,gk->ghkp", bb_re, eye).reshape(sw, gp)
    w_in_im = jnp.einsum("gph,gk->ghkp", bb_im, eye).reshape(sw, gp)
    w_in = jnp.concatenate([w_in_re, w_in_im], axis=1)
    per_half = gp // MXU_DIM
    in_tiles = sw // MXU_DIM
    n_tiles = 2 * per_half
    kt = [((j % per_half) * in_tiles) // per_half for j in range(n_tiles)]
    w_tiles = w_in.astype(BF16).reshape(in_tiles, MXU_DIM, n_tiles, MXU_DIM).transpose(2, 0, 1, 3)
    wb = w_tiles[jnp.arange(n_tiles), jnp.asarray(kt, I32)]
    w_out_re = jnp.einsum("ghp,gk->gpkh", c_re.astype(F32), eye).reshape(gp, sw)
    w_out_im = -jnp.einsum("ghp,gk->gpkh", c_im.astype(F32), eye).reshape(gp, sw)
    out_tiles = sw // MXU_DIM
    kw = gp // out_tiles
    wc = jnp.stack([
        jnp.stack([w_out_re[nt * kw:(nt + 1) * kw, nt * MXU_DIM:(nt + 1) * MXU_DIM],
                   w_out_im[nt * kw:(nt + 1) * kw, nt * MXU_DIM:(nt + 1) * MXU_DIM]])
        for nt in range(out_tiles)]).astype(BF16)
    return dict(a_re=lb_re.reshape(1, gp), a_im=lb_im.reshape(1, gp), wb=wb, wc=wc,
                d=d.astype(F32).reshape(1, sw))


def _diff_lambda(lq1, lk1, lq2, lk2, lam_init):
    l1 = jnp.exp(jnp.sum(lq1 * lk1, axis=1, keepdims=True))
    l2 = jnp.exp(jnp.sum(lq2 * lk2, axis=1, keepdims=True))
    return l1 - l2 + lam_init


def _head_norm(o, g, lam_init):
    ms = jnp.mean(o * o, axis=-1, keepdims=True)
    return o * lax.rsqrt(ms + RMS_EPS) * g * (1.0 - lam_init)


def _attn_prompt_kernel(q_ref, k_ref, v_ref, lq1_ref, lk1_ref, lq2_ref, lk2_ref, g_ref, o_ref,
                        kb_ref, vt_ref, qs_ref, acc_ref, *, tq, qk, heads, lam_init, scale):
    h = pl.program_id(1)
    nq = vt_ref.shape[0]
    seq = nq * tq
    kb_ref[...] = k_ref[pl.ds(h, seq, stride=heads), :].astype(BF16)
    for jb in range(nq):
        vt_ref[jb] = v_ref[pl.ds(jb * tq * heads + h, tq, stride=heads), :].T.astype(BF16)
    q = q_ref[...] * (scale * LOG2E)
    lane = lax.broadcasted_iota(I32, q.shape, 1)
    qs_ref[0] = jnp.where(lane < qk, q, 0.0).astype(BF16)
    qs_ref[1] = jnp.where(lane >= qk, q, 0.0).astype(BF16)
    acc_ref[...] = jnp.zeros(acc_ref.shape, F32)
    key = lax.broadcasted_iota(I32, (tq, tq), 0)
    qry = lax.broadcasted_iota(I32, (tq, tq), 1)
    keep = key <= qry
    m = [[jnp.full((1, tq), NEG_BIG, F32)] * 2 for _ in range(nq)]
    l = [[jnp.zeros((1, tq), F32)] * 2 for _ in range(nq)]
    tasks = [(j, i, c) for j in range(nq) for i in range(j, nq) for c in range(2)]

    def scores(t):
        j, i, c = tasks[t]
        st = _dot_nt(kb_ref[j * tq:(j + 1) * tq, :], qs_ref[c, i * tq:(i + 1) * tq, :])
        return jnp.where(keep, st, NEG_BIG) if i == j else st

    pending = {t: scores(t) for t in range(min(ATT_LOOKAHEAD, len(tasks)))}
    for t, (j, i, c) in enumerate(tasks):
        if t + ATT_LOOKAHEAD < len(tasks):
            pending[t + ATT_LOOKAHEAD] = scores(t + ATT_LOOKAHEAD)
        st = pending.pop(t)
        m_new = jnp.maximum(m[i][c], jnp.max(st, axis=0, keepdims=True))
        alpha = jnp.exp2(m[i][c] - m_new)
        p = jnp.exp2(st - m_new)
        acc_ref[i, c] = alpha * acc_ref[i, c] + _dot(vt_ref[j], p.astype(BF16))
        l[i][c] = alpha * l[i][c] + jnp.sum(p, axis=0, keepdims=True)
        m[i][c] = m_new
    lam = _diff_lambda(lq1_ref[...], lk1_ref[...], lq2_ref[...], lk2_ref[...], lam_init)
    for i in range(nq):
        ot = acc_ref[i, 0] * (1.0 / l[i][0]) - lam * (acc_ref[i, 1] * (1.0 / l[i][1]))
        ms = jnp.mean(ot * ot, axis=0, keepdims=True)
        ot = ot * lax.rsqrt(ms + RMS_EPS) * g_ref[...] * (1.0 - lam_init)
        o_ref[i * tq:(i + 1) * tq, :] = ot.T


def _attn_prompt(q, k, v, lam_p, g_col, *, heads, lam_init):
    nb, seq, aw = q.shape
    hd = aw // heads
    qk = hd // 2
    tq = min(ATT_ROWS, seq)
    nq = seq // tq
    kern = functools.partial(_attn_prompt_kernel, tq=tq, qk=qk, heads=heads, lam_init=lam_init, scale=qk ** -0.5)
    small = pl.BlockSpec((1, qk), lambda b, h: (0, 0))
    return pl.pallas_call(
        kern,
        grid=(nb, heads),
        in_specs=[
            pl.BlockSpec((None, seq, hd), lambda b, h: (b, 0, h)),
            pl.BlockSpec((None, seq * heads, hd), lambda b, h: (b, 0, 0)),
            pl.BlockSpec((None, seq * heads, hd), lambda b, h: (b, 0, 0)),
            small, small, small, small,
            pl.BlockSpec((hd, 1), lambda b, h: (0, 0)),
        ],
        out_specs=pl.BlockSpec((seq, hd), lambda b, h: (b, h)),
        out_shape=jax.ShapeDtypeStruct((nb * seq, aw), F32),
        scratch_shapes=[pltpu.VMEM((seq, hd), BF16), pltpu.VMEM((nq, hd, tq), BF16),
                        pltpu.VMEM((2, seq, hd), BF16), pltpu.VMEM((nq, 2, hd, tq), F32)],
        compiler_params=_params("arbitrary", "arbitrary"),
        name="diff_attn_prompt",
    )(q, k, v, *lam_p, g_col)


def _attn_sample_kernel(pt_ref, q_ref, ks_ref, vs_ref, lq1_ref, lk1_ref, lq2_ref, lk2_ref, g_ref, ck_hbm, cv_hbm,
                        o_ref, kbuf, vbuf, sem, *, n_pages, page, heads, qk, lam_init, scale):
    b = pl.program_id(0)
    n_seq = pl.num_programs(0)
    slot = b % 2
    hd = 2 * qk
    aw = heads * hd
    n_maps = 2 * heads
    page_rows = page * heads
    seq_rows = n_pages * page_rows

    def page_copies(seq, buf, j):
        src = pl.ds(pl.multiple_of(pt_ref[seq * n_pages + j] * page_rows, page_rows), page_rows)
        dst = pl.ds(buf * seq_rows + j * page_rows, page_rows)
        return (pltpu.make_async_copy(ck_hbm.at[src, :], kbuf.at[dst, :], sem.at[buf]),
                pltpu.make_async_copy(cv_hbm.at[src, :], vbuf.at[dst, :], sem.at[buf]))

    def start_fetch(seq, buf):
        for j in range(n_pages):
            for cp in page_copies(seq, buf, j):
                cp.start()

    @pl.when(b == 0)
    def _():
        start_fetch(0, 0)

    @pl.when(b + 1 < n_seq)
    def _():
        start_fetch(b + 1, 1 - slot)

    q = q_ref[...] * scale
    row = lax.broadcasted_iota(I32, (n_maps, aw), 0)
    lane = lax.broadcasted_iota(I32, (n_maps, aw), 1)
    own = (lane // hd == row % heads) & ((lane // qk) % 2 == row // heads)
    qm = jnp.where(own, jnp.broadcast_to(q, (n_maps, aw)), 0.0)
    qmb = qm.astype(BF16)
    for j in range(n_pages):
        for cp in page_copies(b, slot, j):
            cp.wait()

    def group_rows(buf_ref, g):
        n_tok = min(ATT_PAGE_GROUP, n_pages - g) * page
        base = slot * seq_rows + g * page_rows
        return jnp.concatenate([buf_ref[pl.ds(base + h, n_tok, stride=heads), :] for h in range(heads)],
                               axis=1).astype(BF16)

    groups = range(0, n_pages, ATT_PAGE_GROUP)
    s_pages = [_dot_nt(qmb, group_rows(kbuf, g)) for g in groups]
    s_self = jnp.sum(qm * ks_ref[...], axis=1, keepdims=True)
    m = s_self
    for s in s_pages:
        m = jnp.maximum(m, jnp.max(s, axis=1, keepdims=True))
    p_self = jnp.exp(s_self - m)
    p_pages = [jnp.exp(s - m) for s in s_pages]
    l = p_self
    for p in p_pages:
        l = l + jnp.sum(p, axis=1, keepdims=True)
    inv = 1.0 / l
    lam = _diff_lambda(lq1_ref[...], lk1_ref[...], lq2_ref[...], lk2_ref[...], lam_init)

    def mix(p):
        pn = p * inv
        return pn - lam * pltpu.roll(pn, heads, 0)

    acc = jnp.broadcast_to(mix(p_self), (n_maps, aw)) * vs_ref[...]
    for p, g in zip(p_pages, groups):
        acc = acc + _dot(mix(p).astype(BF16), group_rows(vbuf, g))
    pick = (row < heads) & (lane // hd == row)
    o = jnp.sum(jnp.where(pick, acc, 0.0), axis=0, keepdims=True)
    g = g_ref[...]
    for h in range(heads):
        o_ref[:, h * hd:(h + 1) * hd] = _head_norm(o[:, h * hd:(h + 1) * hd], g, lam_init)


def _attn_sample(q, k_new, v_new, cache_k, cache_v, page_ids, lam_p, g, *, page, heads, lam_init):
    bs, aw = q.shape
    n_pages = page_ids.shape[1]
    hd = aw // heads
    qk = hd // 2
    kern = functools.partial(_attn_sample_kernel, n_pages=n_pages, page=page, heads=heads, qk=qk,
                             lam_init=lam_init, scale=qk ** -0.5)
    row_spec = pl.BlockSpec((None, 1, aw), lambda b, pt: (b, 0, 0))
    small = pl.BlockSpec((1, qk), lambda b, pt: (0, 0))
    buf_rows = 2 * n_pages * page * heads
    grid_spec = pltpu.PrefetchScalarGridSpec(
        num_scalar_prefetch=1,
        grid=(bs,),
        in_specs=[row_spec, row_spec, row_spec, small, small, small, small,
                  pl.BlockSpec((1, hd), lambda b, pt: (0, 0)),
                  pl.BlockSpec(memory_space=pl.ANY), pl.BlockSpec(memory_space=pl.ANY)],
        out_specs=row_spec,
        scratch_shapes=[pltpu.VMEM((buf_rows, hd), F32), pltpu.VMEM((buf_rows, hd), F32),
                        pltpu.SemaphoreType.DMA((2,))],
    )
    out = pl.pallas_call(
        kern,
        grid_spec=grid_spec,
        out_shape=jax.ShapeDtypeStruct((bs, 1, aw), F32),
        compiler_params=_params("arbitrary"),
        name="diff_attn_sample",
    )(page_ids.reshape(-1).astype(I32), q.reshape(bs, 1, aw), k_new.reshape(bs, 1, aw),
      v_new.reshape(bs, 1, aw), *lam_p, g, cache_k, cache_v)
    return out.reshape(bs, aw)


def _layer_norm(z, g, b):
    mu = jnp.mean(z, axis=-1, keepdims=True)
    zc = z - mu
    var = jnp.mean(zc * zc, axis=-1, keepdims=True)
    return zc * lax.rsqrt(var + LN_EPS) * g + b


def _out_kernel(xp_ref, sp_ref, ap_ref, xs_ref, ss_ref, as_ref, wo_ref, g_ref, b_ref, wrh_ref, wrl_ref,
                br_ref, h_ref, e_ref, gt_ref, *, n_prompt, sw, alpha, top_k):
    i = pl.program_id(0)
    is_s = i == n_prompt
    rows = h_ref.shape[0]
    halves = [slice(r, r + rows // 2) for r in (0, rows // 2)]
    mixes = []
    for hs in halves:
        s = jnp.where(is_s, ss_ref[hs, :], sp_ref[hs, :]).astype(BF16)
        a = jnp.where(is_s, as_ref[hs, :], ap_ref[hs, :]).astype(BF16)
        mixes.append(_dot(s, wo_ref[:sw, :]) + _dot(a, wo_ref[sw:, :]))
    for hs, mix in zip(halves, mixes):
        x = jnp.where(is_s, xs_ref[hs, :], xp_ref[hs, :])
        h = _layer_norm(alpha * x + mix, g_ref[...], b_ref[...])
        h_ref[hs, :] = h
        h_hi = h.astype(BF16)
        h_lo = (h - h_hi.astype(F32)).astype(BF16)
        lg = (_dot_nt(wrh_ref[...], h_hi) + _dot_nt(wrl_ref[...], h_hi) + _dot_nt(wrh_ref[...], h_lo)
              + br_ref[...])
        eio = lax.broadcasted_iota(I32, lg.shape, 0)
        n_exp = lg.shape[0]
        vals, idxs = [], []
        for _ in range(top_k):
            m = jnp.max(lg, axis=0, keepdims=True)
            idx = jnp.min(jnp.where(lg == m, eio, n_exp), axis=0, keepdims=True)
            vals.append(m)
            idxs.append(idx)
            lg = jnp.where(eio == idx, NEG_BIG, lg)
        tv = jnp.concatenate(vals, axis=0)
        pe = jnp.exp(tv - tv[0:1, :])
        gt_ref[:, hs] = pe / jnp.sum(pe, axis=0, keepdims=True)
        e_ref[:, hs] = jnp.concatenate(idxs, axis=0)


def _out_route(xp, sp, ap, xs, ss, as_, w_out_bf, g, b, wr_hi, wr_lo, br, *, alpha):
    tp, d = xp.shape
    sw = sp.shape[1]
    rows = OUT_ROWS
    n_prompt = tp // rows
    total = tp + rows
    n_exp = wr_hi.shape[0]
    pidx = lambda i: (jnp.minimum(i, n_prompt - 1), 0)
    const = lambda i: (0, 0)
    kern = functools.partial(_out_kernel, n_prompt=n_prompt, sw=sw, alpha=alpha, top_k=TOP_K)
    return pl.pallas_call(
        kern,
        grid=(n_prompt + 1,),
        in_specs=[
            pl.BlockSpec((rows, d), pidx), pl.BlockSpec((rows, sw), pidx), pl.BlockSpec((rows, d - sw), pidx),
            pl.BlockSpec((rows, d), const), pl.BlockSpec((rows, sw), const), pl.BlockSpec((rows, d - sw), const),
            pl.BlockSpec(w_out_bf.shape, const), pl.BlockSpec((1, d), const), pl.BlockSpec((1, d), const),
            pl.BlockSpec((n_exp, d), const), pl.BlockSpec((n_exp, d), const), pl.BlockSpec((n_exp, 1), const),
        ],
        out_specs=[
            pl.BlockSpec((rows, d), lambda i: (i, 0)),
            pl.BlockSpec((TOP_K, rows), lambda i: (0, i)),
            pl.BlockSpec((TOP_K, rows), lambda i: (0, i)),
        ],
        out_shape=[
            jax.ShapeDtypeStruct((total, d), F32),
            jax.ShapeDtypeStruct((TOP_K, total), I32),
            jax.ShapeDtypeStruct((TOP_K, total), F32),
        ],
        compiler_params=_params("arbitrary"),
        name="out_proj_route",
    )(xp, sp, ap, xs, ss, as_, w_out_bf, g, b, wr_hi, wr_lo, br)


def _tile_plan(e_tile, tile_base, n_tok, n_exp):
    k_, tt = e_tile.shape
    valid = (tile_base + lax.broadcasted_iota(I32, (1, tt), 1)) < n_tok
    eio = lax.broadcasted_iota(I32, (n_exp, tt), 0)
    onehot = [jnp.where((eio == e_tile[k:k + 1, :]) & valid, 1.0, 0.0) for k in range(k_)]
    count_k = [jnp.sum(o, axis=1, keepdims=True) for o in onehot]
    count = functools.reduce(lambda a, b: a + b, count_k)
    padded = jnp.floor((count + (ROUTE_CHUNK - 1)) * (1.0 / ROUTE_CHUNK)) * ROUTE_CHUNK
    r = lax.broadcasted_iota(I32, (n_exp, n_exp), 0)
    c = lax.broadcasted_iota(I32, (n_exp, n_exp), 1)
    padded_row = jnp.sum(jnp.where(r == c, jnp.broadcast_to(padded, (n_exp, n_exp)), 0.0), axis=0, keepdims=True)
    offset = jnp.sum(jnp.where(c < r, jnp.broadcast_to(padded_row, (n_exp, n_exp)), 0.0), axis=1, keepdims=True)
    earlier = (lax.broadcasted_iota(I32, (tt, tt), 0) < lax.broadcasted_iota(I32, (tt, tt), 1)).astype(BF16)
    base = offset
    pos = []
    for k in range(k_):
        before = _dot(onehot[k].astype(BF16), earlier) + base
        pos.append(jnp.where(valid, jnp.sum(onehot[k] * before, axis=0, keepdims=True), -1.0))
        base = base + count_k[k]
    return pos


def _block_relative(pos, c):
    rel = pos - float(c * ROUTE_ROWS)
    return jnp.where((rel >= 0.0) & (rel < float(ROUTE_ROWS)), rel, -1.0).astype(BF16)


def _segment_copies(src_ref, dst_ref, n, make_copy):
    def body(c, carry):
        make_copy(pl.multiple_of(src_ref[0, c], ROUTE_CHUNK), pl.multiple_of(dst_ref[0, c], ROUTE_CHUNK)).start()
        return carry

    lax.fori_loop(0, n, body, 0)


def _wait_chunks(n, make_copy):
    def wait_many(c, carry):
        make_copy(ROUTE_WAIT * ROUTE_CHUNK).wait()
        return carry

    def wait_one(c, carry):
        make_copy(ROUTE_CHUNK).wait()
        return carry

    lax.fori_loop(0, n // ROUTE_WAIT, wait_many, 0)
    lax.fori_loop(0, n % ROUTE_WAIT, wait_one, 0)


def _dispatch_kernel(nchunk_ref, zlo_ref, zhi_ref, src_ref, dst_ref, h_ref, e_ref, x_hbm, xt_ref, zero_ref, sem,
                     *, n_tok, n_exp, block_rows):
    i = pl.program_id(0)
    tt = h_ref.shape[0]
    npos = xt_ref.shape[1]

    @pl.when(i == 0)
    def _():
        zero_ref[...] = jnp.zeros(zero_ref.shape, F32)

        def zero_block(b):
            rows0 = pl.multiple_of(b * block_rows, block_rows)
            return pltpu.make_async_copy(zero_ref, x_hbm.at[pl.ds(rows0, block_rows), :], sem)

        def start(b, c):
            zero_block(b).start()
            return c

        def wait(b, c):
            zero_block(b).wait()
            return c

        for e in range(n_exp + 1):
            lax.fori_loop(zlo_ref[e], zhi_ref[e], start, 0)
        for e in range(n_exp + 1):
            lax.fori_loop(zlo_ref[e], zhi_ref[e], wait, 0)

    slot = i % 2
    pos = _tile_plan(e_ref[...], i * tt, n_tok, n_exp)
    hb = h_ref[...].astype(BF16)
    rio = lax.broadcasted_iota(I32, (ROUTE_ROWS, tt), 0).astype(F32).astype(BF16)
    for c in range(npos // ROUTE_ROWS):
        rel = [_block_relative(p, c) for p in pos]
        sel = functools.reduce(lambda a, b: a | b, [rio == r for r in rel])
        one_hot = jnp.where(sel, jnp.ones((), BF16), jnp.zeros((), BF16))
        xt_ref[slot, c * ROUTE_ROWS:(c + 1) * ROUTE_ROWS, :] = _dot(one_hot, hb)

    def copy(src_row, dst_row, rows=ROUTE_CHUNK):
        return pltpu.make_async_copy(xt_ref.at[slot, pl.ds(src_row, rows), :], x_hbm.at[pl.ds(dst_row, rows), :], sem)

    @pl.when(i > 0)
    def _():
        _wait_chunks(nchunk_ref[i - 1], lambda rows: copy(0, 0, rows))

    _segment_copies(src_ref, dst_ref, nchunk_ref[i], copy)

    @pl.when(i == pl.num_programs(0) - 1)
    def _():
        _wait_chunks(nchunk_ref[i], lambda rows: copy(0, 0, rows))


def _dispatch(h_all, e_idx, tables, *, n_tok, n_exp, n_blocks):
    t_pad, d = h_all.shape
    tt = OUT_ROWS
    npos = TOP_K * tt + n_exp * ROUTE_CHUNK
    kern = functools.partial(_dispatch_kernel, n_tok=n_tok, n_exp=n_exp, block_rows=MOE_ROWS)
    chunk_list = pl.BlockSpec((None, 1, npos // ROUTE_CHUNK), lambda i, *_: (i, 0, 0), memory_space=pltpu.SMEM)
    grid_spec = pltpu.PrefetchScalarGridSpec(
        num_scalar_prefetch=3,
        grid=(t_pad // tt,),
        in_specs=[chunk_list, chunk_list,
                  pl.BlockSpec((tt, d), lambda i, *_: (i, 0)),
                  pl.BlockSpec((TOP_K, tt), lambda i, *_: (0, i))],
        out_specs=pl.BlockSpec(memory_space=pl.ANY),
        scratch_shapes=[pltpu.VMEM((2, npos, d), F32), pltpu.VMEM((MOE_ROWS, d), F32), pltpu.SemaphoreType.DMA],
    )
    return pl.pallas_call(
        kern,
        grid_spec=grid_spec,
        out_shape=jax.ShapeDtypeStruct((n_blocks * MOE_ROWS, d), F32),
        compiler_params=_params("arbitrary"),
        name="moe_dispatch",
    )(tables["nchunk"], tables["zlo"], tables["zhi"], tables["src"], tables["dst"], h_all, e_idx)


def _moe_kernel(be_ref, nu_ref, slot_ref, next_ref, x_ref, wgu_hbm, bgu_ref, wdn_hbm, bdn_ref, y_ref,
                wgu_f, wdn_f, wgu_bf, wdn_bf, wsem, *, de):
    i = pl.program_id(0)

    def fetch(e, slot):
        return (pltpu.make_async_copy(wgu_hbm.at[e], wgu_f.at[slot], wsem.at[slot]),
                pltpu.make_async_copy(wdn_hbm.at[e], wdn_f.at[slot], wsem.at[slot]))

    @pl.when(i >= nu_ref[0])
    def _():
        y_ref[...] = jnp.zeros(y_ref.shape, F32)

    @pl.when(i < nu_ref[0])
    def _():
        slot = slot_ref[i]

        @pl.when(i == 0)
        def _():
            for cp in fetch(be_ref[0], 0):
                cp.start()

        @pl.when((i == 0) | (be_ref[i] != be_ref[jnp.maximum(i - 1, 0)]))
        def _():
            for cp in fetch(be_ref[i], slot):
                cp.wait()
            wgu_bf[...] = wgu_f[slot].astype(BF16)
            wdn_bf[...] = wdn_f[slot].astype(BF16)

            @pl.when(next_ref[i] >= 0)
            def _():
                for cp in fetch(next_ref[i], 1 - slot):
                    cp.start()

        x = x_ref[...].astype(BF16)
        cw = MOE_HIDDEN_CHUNK
        n_chunks = de // cw

        def up(c):
            lo, hi = c * cw, (c + 1) * cw
            return (_dot(x, wgu_bf[:, lo:hi]) + bgu_ref[:, lo:hi],
                    _dot(x, wgu_bf[:, de + lo:de + hi]) + bgu_ref[:, de + lo:de + hi])

        pending = {c: up(c) for c in range(min(MOE_LOOKAHEAD, n_chunks))}
        y = bdn_ref[...]
        for c in range(n_chunks):
            if c + MOE_LOOKAHEAD < n_chunks:
                pending[c + MOE_LOOKAHEAD] = up(c + MOE_LOOKAHEAD)
            glu, lin = pending.pop(c)
            glu = jnp.minimum(glu, SWIGLU_LIMIT)
            lin = jnp.clip(lin, -SWIGLU_LIMIT, SWIGLU_LIMIT)
            act = glu * jax.nn.sigmoid(SWIGLU_ALPHA * glu) * (lin + 1.0)
            y = y + _dot(act.astype(BF16), wdn_bf[c * cw:(c + 1) * cw, :])
        y_ref[...] = y


def _moe(x_rows, block_expert, n_used, w_gu, b_gu, w_dn, b_dn):
    rows = MOE_ROWS
    n_blocks = block_expert.shape[0]
    n_exp, d, de2 = w_gu.shape
    de = de2 // 2
    kern = functools.partial(_moe_kernel, de=de)
    idx = jnp.arange(n_blocks, dtype=I32)
    in_use = idx < n_used[0]
    changed = jnp.concatenate([jnp.ones((1,), bool), block_expert[1:] != block_expert[:-1]])
    w_slot = ((jnp.cumsum(changed.astype(I32)) - 1) % 2).astype(I32)
    later = (block_expert[None, :] > block_expert[:, None]) & in_use[None, :]
    next_e = jnp.min(jnp.where(later, block_expert[None, :], n_exp), axis=1)
    next_e = jnp.where(next_e < n_exp, next_e, -1).astype(I32)

    def used(i, nu):
        return jnp.minimum(i, nu[0] - 1)

    def expert(i, be, nu, *_):
        return (be[used(i, nu)], 0, 0)

    grid_spec = pltpu.PrefetchScalarGridSpec(
        num_scalar_prefetch=4,
        grid=(n_blocks,),
        in_specs=[
            pl.BlockSpec((rows, d), lambda i, be, nu, *_: (used(i, nu), 0)),
            pl.BlockSpec(memory_space=pl.ANY),
            pl.BlockSpec((None, 1, de2), expert),
            pl.BlockSpec(memory_space=pl.ANY),
            pl.BlockSpec((None, 1, d), expert),
        ],
        out_specs=pl.BlockSpec((rows, d), lambda i, *_: (i, 0)),
        scratch_shapes=[pltpu.VMEM((2, d, de2), F32), pltpu.VMEM((2, de, d), F32),
                        pltpu.VMEM((d, de2), BF16), pltpu.VMEM((de, d), BF16), pltpu.SemaphoreType.DMA((2,))],
    )
    return pl.pallas_call(
        kern,
        grid_spec=grid_spec,
        out_shape=jax.ShapeDtypeStruct(x_rows.shape, F32),
        compiler_params=_params("arbitrary"),
        name="moe_experts",
    )(block_expert, n_used, w_slot, next_e, x_rows, w_gu, b_gu.reshape(n_exp, 1, de2), w_dn,
      b_dn.reshape(n_exp, 1, d))


def _route_tables(e_idx, n_tok, n_exp):
    k_, t_pad = e_idx.shape
    tt = OUT_ROWS
    rows = MOE_ROWS
    tiles = t_pad // tt
    valid = jnp.arange(t_pad, dtype=I32) < n_tok
    e_m = jnp.where(valid[None, :], e_idx, -1).reshape(k_, tiles, tt, 1)
    cnt = jnp.sum((e_m == jnp.arange(n_exp, dtype=I32)).astype(I32), axis=(0, 2))
    cpad = ((cnt + ROUTE_CHUNK - 1) // ROUTE_CHUNK) * ROUTE_CHUNK
    off = jnp.cumsum(cpad, axis=1) - cpad
    total = jnp.sum(cpad, axis=0)
    region = ((total + rows - 1) // rows) * rows
    pend = jnp.cumsum(region)
    pstart = pend - region
    gbase = pstart[None, :] + jnp.cumsum(cpad, axis=0) - cpad
    n_blocks = -(-(k_ * n_tok + n_exp * (tiles * (ROUTE_CHUNK - 1) + rows - 1)) // rows)
    block_start = jnp.arange(n_blocks, dtype=I32) * rows
    block_expert = jnp.minimum(jnp.sum((pend[None, :] <= block_start[:, None]).astype(I32), axis=1), n_exp - 1)
    n_used = (pend[-1] // rows).astype(I32)
    zlo = jnp.concatenate([(pstart + total) // rows, n_used[None]]).astype(I32)
    zhi = jnp.concatenate([pend // rows, jnp.full((1,), n_blocks, I32)]).astype(I32)
    nch = cpad // ROUTE_CHUNK
    cum = jnp.cumsum(nch, axis=1)
    c_id = jnp.arange((k_ * tt + n_exp * ROUTE_CHUNK) // ROUTE_CHUNK, dtype=I32)
    owner = jnp.sum((cum[:, None, :] <= c_id[None, :, None]).astype(I32), axis=2)
    pick = (jnp.minimum(owner, n_exp - 1)[:, :, None] == jnp.arange(n_exp, dtype=I32)).astype(I32)
    within = (c_id[None, :] - jnp.sum(pick * (cum - nch)[:, None, :], axis=2)) * ROUTE_CHUNK
    src = jnp.sum(pick * off[:, None, :], axis=2) + within
    dst = jnp.sum(pick * gbase[:, None, :], axis=2) + within
    tables = dict(nchunk=cum[:, -1].astype(I32), src=src[:, None, :].astype(I32), dst=dst[:, None, :].astype(I32),
                  zlo=zlo, zhi=zhi)
    return tables, block_expert.astype(I32), n_used.reshape(1), n_blocks


def _final_kernel(nchunk_ref, srcc_ref, dstc_ref, srcn_ref, dstn_ref, e_ref, gt_ref, h_ref, g_ref, b_ref, y_hbm,
                  op_ref, os_ref, yt_ref, sem, *, n_tok, n_exp, n_prompt, alpha):
    i = pl.program_id(0)
    n_tiles = pl.num_programs(0)
    tt = h_ref.shape[0]
    npos = yt_ref.shape[1]
    slot = i % 2

    def fetch(buf):
        def copy(local_row, global_row, rows=ROUTE_CHUNK):
            return pltpu.make_async_copy(y_hbm.at[pl.ds(global_row, rows), :],
                                         yt_ref.at[buf, pl.ds(local_row, rows), :], sem.at[buf])
        return copy

    @pl.when(i == 0)
    def _():
        yt_ref[...] = jnp.zeros(yt_ref.shape, F32)
        _segment_copies(srcc_ref, dstc_ref, nchunk_ref[0], fetch(0))

    @pl.when(i + 1 < n_tiles)
    def _():
        _segment_copies(srcn_ref, dstn_ref, nchunk_ref[jnp.minimum(i + 1, n_tiles - 1)], fetch(1 - slot))

    pos = _tile_plan(e_ref[...], i * tt, n_tok, n_exp)
    top_k = len(pos)
    rows = jnp.concatenate(pos + [gt_ref[...], jnp.zeros((LANES - 2 * top_k, tt), F32)], axis=0)
    cols = rows.T

    _wait_chunks(nchunk_ref[i], lambda rows: fetch(slot)(0, 0, rows))
    f = jnp.zeros((tt, h_ref.shape[1]), F32)
    lio = lax.broadcasted_iota(I32, (tt, ROUTE_ROWS), 1).astype(F32).astype(BF16)
    gate_cols = [cols[:, top_k + k:top_k + k + 1].astype(BF16) for k in range(top_k)]
    for c in range(npos // ROUTE_ROWS):
        w = jnp.zeros((tt, ROUTE_ROWS), BF16)
        for k in range(top_k):
            w = jnp.where(lio == _block_relative(cols[:, k:k + 1], c), gate_cols[k], w)
        f = f + _dot(w, yt_ref[slot, c * ROUTE_ROWS:(c + 1) * ROUTE_ROWS, :].astype(BF16))
    out = _layer_norm(alpha * h_ref[...] + f, g_ref[...], b_ref[...])

    @pl.when(i < n_prompt)
    def _():
        op_ref[...] = out

    @pl.when(i == n_prompt)
    def _():
        os_ref[...] = out[:os_ref.shape[0], :]


def _final(y_rows, e_idx, gates, h_all, g, b, tables, *, tp, ts, n_exp, alpha):
    tt = OUT_ROWS
    t_pad, d = h_all.shape
    n_prompt = tp // tt
    npos = TOP_K * tt + n_exp * ROUTE_CHUNK
    const = lambda i, *_: (0, 0)
    kern = functools.partial(_final_kernel, n_tok=tp + ts, n_exp=n_exp, n_prompt=n_prompt, alpha=alpha)
    n_tiles = t_pad // tt
    n_chunks = npos // ROUTE_CHUNK
    list_cur = pl.BlockSpec((None, 1, n_chunks), lambda i, *_: (i, 0, 0), memory_space=pltpu.SMEM)
    list_nxt = pl.BlockSpec((None, 1, n_chunks), lambda i, *_: (jnp.minimum(i + 1, n_tiles - 1), 0, 0),
                            memory_space=pltpu.SMEM)
    grid_spec = pltpu.PrefetchScalarGridSpec(
        num_scalar_prefetch=1,
        grid=(n_tiles,),
        in_specs=[list_cur, list_cur, list_nxt, list_nxt,
                  pl.BlockSpec((TOP_K, tt), lambda i, *_: (0, i)),
                  pl.BlockSpec((TOP_K, tt), lambda i, *_: (0, i)),
                  pl.BlockSpec((tt, d), lambda i, *_: (i, 0)),
                  pl.BlockSpec((1, d), const), pl.BlockSpec((1, d), const),
                  pl.BlockSpec(memory_space=pl.ANY)],
        out_specs=[pl.BlockSpec((tt, d), lambda i, *_: (jnp.minimum(i, n_prompt - 1), 0)),
                   pl.BlockSpec((ts, d), const)],
        scratch_shapes=[pltpu.VMEM((2, npos, d), F32), pltpu.SemaphoreType.DMA((2,))],
    )
    return pl.pallas_call(
        kern,
        grid_spec=grid_spec,
        out_shape=[jax.ShapeDtypeStruct((tp, d), F32), jax.ShapeDtypeStruct((ts, d), F32)],
        compiler_params=_params("arbitrary"),
        name="combine_ln",
    )(tables["nchunk"], tables["src"], tables["dst"], tables["src"], tables["dst"], e_idx, gates, h_all, g, b,
      y_rows)


def _pad_rows(a, rows):
    return jnp.pad(a, ((0, rows - a.shape[0]), (0, 0)))


def kernel(x_prompt, x_sample, cache_k, cache_v, state_ssm_re, state_ssm_im, page_table, w_in, w_out, ssm_a_re, ssm_a_im, ssm_log_dt, ssm_b_re, ssm_b_im, ssm_c_re, ssm_c_im, ssm_d, w_glu, b_glu, lambda_q1, lambda_k1, lambda_q2, lambda_k2, subln_g, ln1_g, ln1_b, w_router, b_router, w_gate_up, b_gate_up, w_down, b_down, ln2_g, ln2_b):
    depth = w_in.shape[0]
    bp, lp, d = x_prompt.shape
    bs, ls, _ = x_sample.shape
    n_phys, page, heads, hd = cache_k.shape[1:]
    assert ls == 1 and hd == LANES
    qk = hd // 2
    aw = heads * hd
    groups, n_state = ssm_a_re.shape[1:]
    sw = groups * ssm_b_re.shape[-1]
    gp = groups * n_state
    n_exp = w_router.shape[2]
    past_len = page_table.shape[1] * page
    alpha = (2 * depth) ** 0.25
    tp = bp * lp
    n_tok = tp + bs

    cos_p, sin_p = _rope_tables(jnp.arange(lp, dtype=I32), qk // 2)
    cos_s, sin_s = _rope_tables(jnp.full((bs,), past_len, I32), qk // 2)
    ck = cache_k.reshape(depth * n_phys * page * heads, hd)
    cv = cache_v.reshape(depth * n_phys * page * heads, hd)

    hp = x_prompt
    hs = x_sample.reshape(1, bs, d)
    outs = [[] for _ in range(8)]
    for l in range(depth):
        lam_init = 0.8 - 0.6 * math.exp(-0.3 * l)
        w_in_bf = w_in[l].astype(BF16)
        w_out_bf = w_out[l].astype(BF16)
        w_glu_bf = w_glu[l].astype(BF16)
        bg = b_glu[l].reshape(1, 2 * sw)
        lam_p = [p[l].reshape(1, qk).astype(F32) for p in (lambda_q1, lambda_k1, lambda_q2, lambda_k2)]
        prm = _ssm_params(ssm_a_re[l], ssm_a_im[l], ssm_log_dt[l], ssm_b_re[l], ssm_b_im[l],
                          ssm_c_re[l], ssm_c_im[l], ssm_d[l])

        tt = min(PROJ_ROWS // bp, lp)
        u_p, q_p, k_p, v_p = _project(hp, w_in_bf, cos_p, sin_p, tt=tt, sw=sw, aw=aw, half=qk // 2)
        zero = jnp.zeros((bp, gp), F32)
        ssm_p, sre_p, sim_p = _ssm(u_p, zero, zero, prm, nb=bp, steps=min(SSM_STEPS, lp), w_glu_bf=w_glu_bf,
                                   b_glu=bg)
        att_p = _attn_prompt(q_p, k_p, v_p, lam_p, subln_g[l].reshape(hd, 1), heads=heads, lam_init=lam_init)

        u_s, q_s, k_s, v_s = _project(hs, w_in_bf, cos_s, sin_s, tt=bs, sw=sw, aw=aw, half=qk // 2)
        ssm_s, sre_s, sim_s = _ssm(u_s, state_ssm_re[l].reshape(bs, gp), state_ssm_im[l].reshape(bs, gp), prm,
                                   nb=bs, steps=1, w_glu_bf=w_glu_bf, b_glu=bg)
        att_s = _attn_sample(q_s.reshape(bs, aw), k_s.reshape(bs, aw), v_s.reshape(bs, aw), ck, cv,
                             page_table + l * n_phys, lam_p, subln_g[l].reshape(1, hd), page=page, heads=heads,
                             lam_init=lam_init)

        wr = w_router[l].T.astype(F32)
        wr_hi = wr.astype(BF16)
        wr_lo = (wr - wr_hi.astype(F32)).astype(BF16)
        h_all, e_idx, gates = _out_route(
            hp.reshape(tp, d), ssm_p.reshape(tp, sw), att_p, _pad_rows(hs.reshape(bs, d), OUT_ROWS),
            _pad_rows(ssm_s, OUT_ROWS), _pad_rows(att_s, OUT_ROWS), w_out_bf, ln1_g[l].reshape(1, d),
            ln1_b[l].reshape(1, d), wr_hi, wr_lo, b_router[l].reshape(n_exp, 1), alpha=alpha)
        tables, block_expert, n_used, n_blocks = _route_tables(e_idx, n_tok, n_exp)
        x_rows = _dispatch(h_all, e_idx, tables, n_tok=n_tok, n_exp=n_exp, n_blocks=n_blocks)
        y_rows = _moe(x_rows, block_expert, n_used, w_gate_up[l], b_gate_up[l], w_down[l], b_down[l])
        y_p, y_s = _final(y_rows, e_idx, gates, h_all, ln2_g[l].reshape(1, d), ln2_b[l].reshape(1, d), tables,
                          tp=tp, ts=bs, n_exp=n_exp, alpha=alpha)
        hp = y_p.reshape(bp, lp, d)
        hs = y_s.reshape(1, bs, d)

        for lst, val in zip(outs, (
                k_p.reshape(bp, lp, heads, hd), v_p.reshape(bp, lp, heads, hd),
                sre_p.reshape(bp, groups, n_state), sim_p.reshape(bp, groups, n_state),
                k_s.reshape(bs, ls, heads, hd), v_s.reshape(bs, ls, heads, hd),
                sre_s.reshape(bs, groups, n_state), sim_s.reshape(bs, groups, n_state))):
            lst.append(val)

    return (hp, hs.reshape(bs, ls, d)) + tuple(jnp.stack(o) for o in outs)
```

```python
import functools
import math

import jax
import jax.numpy as jnp
from jax import lax
from jax.experimental import pallas as pl
from jax.experimental.pallas import tpu as pltpu

F32 = jnp.float32
BF16 = jnp.bfloat16
I32 = jnp.int32

TOP_K = 4
ROPE_THETA = 10000.0
SWIGLU_ALPHA = 1.702
SWIGLU_LIMIT = 7.0
LN_EPS = 1e-5
RMS_EPS = 1e-5
SSM_A_RE_MAX = -1e-4
GELU_C0 = math.sqrt(2.0 / math.pi)
GELU_C1 = 0.044715
NEG_BIG = -1e30
LOG2E = math.log2(math.e)

LANES = 128
SUBLANES = 8
MXU_DIM = 256
VMEM_LIMIT_BYTES = 56 * 1024 * 1024

PROJ_ROWS = 512
SSM_STEPS = 64
ATT_ROWS = 256
ATT_LOOKAHEAD = 4
ATT_PAGE_GROUP = 4
MOE_HIDDEN_CHUNK = 256
MOE_LOOKAHEAD = 3
ROUTE_CHUNK = 8
ROUTE_ROWS = 256
ROUTE_WAIT = 32
OUT_ROWS = 512
MOE_ROWS = 256
SCAN_LANES = 512


def _params(*sem):
    return pltpu.CompilerParams(dimension_semantics=sem, vmem_limit_bytes=VMEM_LIMIT_BYTES)


def _dot(a, b):
    return jnp.dot(a, b, preferred_element_type=F32)


def _dot_nt(a, b):
    return lax.dot_general(a, b, (((1,), (1,)), ((), ())), preferred_element_type=F32)


def _proj_kernel(x_ref, w_ref, cos_ref, sin_ref, u_ref, q_ref, k_ref, v_ref, *, nb, tt, sw, aw, half):
    d = x_ref.shape[-1]
    heads = aw // LANES
    acc = _dot(x_ref[...].reshape(nb * tt, d).astype(BF16), w_ref[...])
    c = cos_ref[...]
    s = sin_ref[...]
    lane = lax.broadcasted_iota(I32, c.shape, 1)
    first = (lane % (2 * half)) < half

    def rope(blk):
        swapped = jnp.where(first, pltpu.roll(blk, LANES - half, 1), pltpu.roll(blk, half, 1))
        return blk * c + swapped * s

    for b in range(nb):
        rows = slice(b * tt, (b + 1) * tt)
        for j in range(sw // LANES):
            u_ref[j, pl.ds(b, tt, stride=nb), :] = acc[rows, j * LANES:(j + 1) * LANES]
        q_ref[b] = jnp.concatenate(
            [rope(acc[rows, sw + j * LANES:sw + (j + 1) * LANES]) for j in range(heads)], axis=1)
        for j in range(heads):
            lo = sw + aw + j * LANES
            k_ref[b, pl.ds(j, tt, stride=heads), :] = rope(acc[rows, lo:lo + LANES])
            lo = sw + 2 * aw + j * LANES
            v_ref[b, pl.ds(j, tt, stride=heads), :] = acc[rows, lo:lo + LANES]


def _project(x3, w_bf, cos_t, sin_t, *, tt, sw, aw, half):
    nb, seq, d = x3.shape
    heads = aw // LANES
    kern = functools.partial(_proj_kernel, nb=nb, tt=tt, sw=sw, aw=aw, half=half)
    return pl.pallas_call(
        kern,
        grid=(seq // tt,),
        in_specs=[
            pl.BlockSpec((nb, tt, d), lambda i: (0, i, 0)),
            pl.BlockSpec(w_bf.shape, lambda i: (0, 0)),
            pl.BlockSpec((tt, LANES), lambda i: (i, 0)),
            pl.BlockSpec((tt, LANES), lambda i: (i, 0)),
        ],
        out_specs=[
            pl.BlockSpec((sw // LANES, tt * nb, LANES), lambda i: (0, i, 0)),
            pl.BlockSpec((nb, tt, aw), lambda i: (0, i, 0)),
            pl.BlockSpec((nb, tt * heads, LANES), lambda i: (0, i, 0)),
            pl.BlockSpec((nb, tt * heads, LANES), lambda i: (0, i, 0)),
        ],
        out_shape=[
            jax.ShapeDtypeStruct((sw // LANES, seq * nb, LANES), F32),
            jax.ShapeDtypeStruct((nb, seq, aw), F32),
            jax.ShapeDtypeStruct((nb, seq * heads, LANES), F32),
            jax.ShapeDtypeStruct((nb, seq * heads, LANES), F32),
        ],
        compiler_params=_params("arbitrary"),
        name="proj_rope",
    )(x3, w_bf, cos_t, sin_t)


def _rope_tables(pos, half):
    inv = ROPE_THETA ** (-jnp.arange(half, dtype=F32) / half)
    ang = pos.astype(F32)[:, None] * inv[None, :]
    reps = LANES // half
    cos = jnp.tile(jnp.cos(ang), (1, reps))
    sin = jnp.sin(ang)
    sin = jnp.tile(jnp.concatenate([-sin, sin], axis=1), (1, reps // 2))
    return cos, sin


def _ssm_kernel(u_ref, s0r_ref, s0i_ref, ar_ref, ai_ref, wb_ref, wc_ref, d_ref, wg_ref, bg_ref,
                o_ref, sr_ref, si_ref, bu_ref, st_ref, ot_ref, *, nb, steps, gp, sw):
    i = pl.program_id(0)
    rows = nb * steps

    @pl.when(i == 0)
    def _():
        st_ref[:, :gp] = s0r_ref[...]
        st_ref[:, gp:] = s0i_ref[...]

    u = jnp.concatenate([u_ref[j] for j in range(sw // LANES)], axis=1)
    ub = u.astype(BF16)
    n_tiles = 2 * gp // MXU_DIM
    per_half = gp // MXU_DIM
    in_tiles = sw // MXU_DIM
    for j in range(n_tiles):
        kt = ((j % per_half) * in_tiles) // per_half
        bu_ref[:, j * MXU_DIM:(j + 1) * MXU_DIM] = _dot(ub[:, kt * MXU_DIM:(kt + 1) * MXU_DIM], wb_ref[j])

    for c in range(gp // SCAN_LANES):
        lo_r = c * SCAN_LANES
        lo_i = gp + c * SCAN_LANES
        ar = jnp.broadcast_to(ar_ref[:, lo_r:lo_r + SCAN_LANES], (nb, SCAN_LANES))
        ai = jnp.broadcast_to(ai_ref[:, lo_r:lo_r + SCAN_LANES], (nb, SCAN_LANES))

        def step(t, carry, lo_r=lo_r, lo_i=lo_i, ar=ar, ai=ai):
            sr, si = carry
            r0 = pl.multiple_of(t * nb, nb)
            br = bu_ref[pl.ds(r0, nb), lo_r:lo_r + SCAN_LANES]
            bi = bu_ref[pl.ds(r0, nb), lo_i:lo_i + SCAN_LANES]
            nr = ar * sr - ai * si + br
            ni = ar * si + ai * sr + bi
            bu_ref[pl.ds(r0, nb), lo_r:lo_r + SCAN_LANES] = nr
            bu_ref[pl.ds(r0, nb), lo_i:lo_i + SCAN_LANES] = ni
            return nr, ni

        sr0 = st_ref[:, lo_r:lo_r + SCAN_LANES]
        si0 = st_ref[:, lo_i:lo_i + SCAN_LANES]
        if steps == 1:
            sr1, si1 = step(0, (sr0, si0))
        else:
            sr1, si1 = lax.fori_loop(0, steps, step, (sr0, si0))
        st_ref[:, lo_r:lo_r + SCAN_LANES] = sr1
        st_ref[:, lo_i:lo_i + SCAN_LANES] = si1

    sr_ref[...] = st_ref[:, :gp]
    si_ref[...] = st_ref[:, gp:]

    out_tiles = sw // MXU_DIM
    kw = gp // out_tiles
    ys = []
    for nt in range(out_tiles):
        s_re = bu_ref[:, nt * kw:(nt + 1) * kw].astype(BF16)
        s_im = bu_ref[:, gp + nt * kw:gp + (nt + 1) * kw].astype(BF16)
        ys.append(_dot(s_re, wc_ref[nt, 0]) + _dot(s_im, wc_ref[nt, 1]))
    y = jnp.concatenate(ys, axis=1) + d_ref[...] * u
    y = 0.5 * y * (1.0 + jnp.tanh(GELU_C0 * (y + GELU_C1 * (y * y * y))))
    z = _dot(y.astype(BF16), wg_ref[...]) + bg_ref[...]
    out = z[:, :sw] * jax.nn.sigmoid(z[:, sw:])
    if steps == 1:
        o_ref[...] = out
    else:
        for j in range(sw // LANES):
            ot_ref[j] = out[:, j * LANES:(j + 1) * LANES]
        for b in range(nb):
            for j in range(sw // LANES):
                o_ref[b, :, j * LANES:(j + 1) * LANES] = ot_ref[j, pl.ds(b, steps, stride=nb), :]


def _ssm(u_tb, s0r, s0i, prm, *, nb, steps, w_glu_bf, b_glu):
    n_slabs, n_rows, _ = u_tb.shape
    sw = n_slabs * LANES
    seq = n_rows // nb
    gp = prm["a_re"].shape[1]
    rows = nb * steps
    grid = (seq // steps,)
    const2 = lambda i: (0, 0)
    kern = functools.partial(_ssm_kernel, nb=nb, steps=steps, gp=gp, sw=sw)
    if steps == 1:
        o_spec = pl.BlockSpec((nb, sw), const2)
        o_shape = jax.ShapeDtypeStruct((nb, sw), F32)
    else:
        o_spec = pl.BlockSpec((nb, steps, sw), lambda i: (0, i, 0))
        o_shape = jax.ShapeDtypeStruct((nb, seq, sw), F32)
    return pl.pallas_call(
        kern,
        grid=grid,
        in_specs=[
            pl.BlockSpec((n_slabs, rows, LANES), lambda i: (0, i, 0)),
            pl.BlockSpec((nb, gp), const2),
            pl.BlockSpec((nb, gp), const2),
            pl.BlockSpec((1, gp), const2),
            pl.BlockSpec((1, gp), const2),
            pl.BlockSpec(prm["wb"].shape, lambda i: (0, 0, 0)),
            pl.BlockSpec(prm["wc"].shape, lambda i: (0, 0, 0, 0)),
            pl.BlockSpec((1, sw), const2),
            pl.BlockSpec(w_glu_bf.shape, const2),
            pl.BlockSpec((1, 2 * sw), const2),
        ],
        out_specs=[o_spec, pl.BlockSpec((nb, gp), const2), pl.BlockSpec((nb, gp), const2)],
        out_shape=[o_shape, jax.ShapeDtypeStruct((nb, gp), F32), jax.ShapeDtypeStruct((nb, gp), F32)],
        scratch_shapes=[
            pltpu.VMEM((rows, 2 * gp), F32),
            pltpu.VMEM((nb, 2 * gp), F32),
            pltpu.VMEM((sw // LANES, rows, LANES), F32),
        ],
        compiler_params=_params("arbitrary"),
        name="ssm_mixer",
    )(u_tb, s0r, s0i, prm["a_re"], prm["a_im"], prm["wb"], prm["wc"], prm["d"], w_glu_bf, b_glu)


def _ssm_params(a_re, a_im, log_dt, b_re, b_im, c_re, c_im, d):
    g_, p_ = a_re.shape
    h_ = b_re.shape[-1]
    gp = g_ * p_
    sw = g_ * h_
    a_re = jnp.minimum(a_re.astype(F32), SSM_A_RE_MAX)
    a_im = a_im.astype(F32)
    dt = jnp.exp(log_dt.astype(F32))[:, None]
    mag = jnp.exp(a_re * dt)
    lb_re = mag * jnp.cos(a_im * dt)
    lb_im = mag * jnp.sin(a_im * dt)
    den = jnp.square(a_re) + jnp.square(a_im)
    nr, ni = lb_re - 1.0, lb_im
    f_re = (nr * a_re + ni * a_im) / den
    f_im = (ni * a_re - nr * a_im) / den
    b_re = b_re.astype(F32)
    b_im = b_im.astype(F32)
    bb_re = f_re[..., None] * b_re - f_im[..., None] * b_im
    bb_im = f_re[..., None] * b_im + f_im[..., None] * b_re
    eye = jnp.eye(g_, dtype=F32)
    w_in_re = jnp.einsum("gph,gk->ghkp", bb_re, eye).reshape(sw, gp)
    w_in_im = jnp.einsum("gph,gk->ghkp", bb_im, eye).reshape(sw, gp)
    w_in = jnp.concatenate([w_in_re, w_in_im], axis=1)
    per_half = gp // MXU_DIM
    in_tiles = sw // MXU_DIM
    n_tiles = 2 * per_half
    kt = [((j % per_half) * in_tiles) // per_half for j in range(n_tiles)]
    w_tiles = w_in.astype(BF16).reshape(in_tiles, MXU_DIM, n_tiles, MXU_DIM).transpose(2, 0, 1, 3)
    wb = w_tiles[jnp.arange(n_tiles), jnp.asarray(kt, I32)]
    w_out_re = jnp.einsum("ghp,gk->gpkh", c_re.astype(F32), eye).reshape(gp, sw)
    w_out_im = -jnp.einsum("ghp,gk->gpkh", c_im.astype(F32), eye).reshape(gp, sw)
    out_tiles = sw // MXU_DIM
    kw = gp // out_tiles
    wc = jnp.stack([
        jnp.stack([w_out_re[nt * kw:(nt + 1) * kw, nt * MXU_DIM:(nt + 1) * MXU_DIM],
                   w_out_im[nt * kw:(nt + 1) * kw, nt * MXU_DIM:(nt + 1) * MXU_DIM]])
        for nt in range(out_tiles)]).astype(BF16)
    return dict(a_re=lb_re.reshape(1, gp), a_im=lb_im.reshape(1, gp), wb=wb, wc=wc,
                d=d.astype(F32).reshape(1, sw))


def _diff_lambda(lq1, lk1, lq2, lk2, lam_init):
    l1 = jnp.exp(jnp.sum(lq1 * lk1, axis=1, keepdims=True))
    l2 = jnp.exp(jnp.sum(lq2 * lk2, axis=1, keepdims=True))
    return l1 - l2 + lam_init


def _head_norm(o, g, lam_init):
    ms = jnp.mean(o * o, axis=-1, keepdims=True)
    return o * lax.rsqrt(ms + RMS_EPS) * g * (1.0 - lam_init)


def _attn_prompt_kernel(q_ref, k_ref, v_ref, lq1_ref, lk1_ref, lq2_ref, lk2_ref, g_ref, o_ref,
                        kb_ref, vt_ref, qs_ref, acc_ref, *, tq, qk, heads, lam_init, scale):
    h = pl.program_id(1)
    nq = vt_ref.shape[0]
    seq = nq * tq
    kb_ref[...] = k_ref[pl.ds(h, seq, stride=heads), :].astype(BF16)
    for jb in range(nq):
        vt_ref[jb] = v_ref[pl.ds(jb * tq * heads + h, tq, stride=heads), :].T.astype(BF16)
    q = q_ref[...] * (scale * LOG2E)
    lane = lax.broadcasted_iota(I32, q.shape, 1)
    qs_ref[0] = jnp.where(lane < qk, q, 0.0).astype(BF16)
    qs_ref[1] = jnp.where(lane >= qk, q, 0.0).astype(BF16)
    acc_ref[...] = jnp.zeros(acc_ref.shape, F32)
    key = lax.broadcasted_iota(I32, (tq, tq), 0)
    qry = lax.broadcasted_iota(I32, (tq, tq), 1)
    keep = key <= qry
    m = [[jnp.full((1, tq), NEG_BIG, F32)] * 2 for _ in range(nq)]
    l = [[jnp.zeros((1, tq), F32)] * 2 for _ in range(nq)]
    tasks = [(j, i, c) for j in range(nq) for i in range(j, nq) for c in range(2)]

    def scores(t):
        j, i, c = tasks[t]
        st = _dot_nt(kb_ref[j * tq:(j + 1) * tq, :], qs_ref[c, i * tq:(i + 1) * tq, :])
        return jnp.where(keep, st, NEG_BIG) if i == j else st

    pending = {t: scores(t) for t in range(min(ATT_LOOKAHEAD, len(tasks)))}
    for t, (j, i, c) in enumerate(tasks):
        if t + ATT_LOOKAHEAD < len(tasks):
            pending[t + ATT_LOOKAHEAD] = scores(t + ATT_LOOKAHEAD)
        st = pending.pop(t)
        m_new = jnp.maximum(m[i][c], jnp.max(st, axis=0, keepdims=True))
        alpha = jnp.exp2(m[i][c] - m_new)
        p = jnp.exp2(st - m_new)
        acc_ref[i, c] = alpha * acc_ref[i, c] + _dot(vt_ref[j], p.astype(BF16))
        l[i][c] = alpha * l[i][c] + jnp.sum(p, axis=0, keepdims=True)
        m[i][c] = m_new
    lam = _diff_lambda(lq1_ref[...], lk1_ref[...], lq2_ref[...], lk2_ref[...], lam_init)
    for i in range(nq):
        ot = acc_ref[i, 0] * (1.0 / l[i][0]) - lam * (acc_ref[i, 1] * (1.0 / l[i][1]))
        ms = jnp.mean(ot * ot, axis=0, keepdims=True)
        ot = ot * lax.rsqrt(ms + RMS_EPS) * g_ref[...] * (1.0 - lam_init)
        o_ref[i * tq:(i + 1) * tq, :] = ot.T


def _attn_prompt(q, k, v, lam_p, g_col, *, heads, lam_init):
    nb, seq, aw = q.shape
    hd = aw // heads
    qk = hd // 2
    tq = min(ATT_ROWS, seq)
    nq = seq // tq
    kern = functools.partial(_attn_prompt_kernel, tq=tq, qk=qk, heads=heads, lam_init=lam_init, scale=qk ** -0.5)
    small = pl.BlockSpec((1, qk), lambda b, h: (0, 0))
    return pl.pallas_call(
        kern,
        grid=(nb, heads),
        in_specs=[
            pl.BlockSpec((None, seq, hd), lambda b, h: (b, 0, h)),
            pl.BlockSpec((None, seq * heads, hd), lambda b, h: (b, 0, 0)),
            pl.BlockSpec((None, seq * heads, hd), lambda b, h: (b, 0, 0)),
            small, small, small, small,
            pl.BlockSpec((hd, 1), lambda b, h: (0, 0)),
        ],
        out_specs=pl.BlockSpec((seq, hd), lambda b, h: (b, h)),
        out_shape=jax.ShapeDtypeStruct((nb * seq, aw), F32),
        scratch_shapes=[pltpu.VMEM((seq, hd), BF16), pltpu.VMEM((nq, hd, tq), BF16),
                        pltpu.VMEM((2, seq, hd), BF16), pltpu.VMEM((nq, 2, hd, tq), F32)],
        compiler_params=_params("arbitrary", "arbitrary"),
        name="diff_attn_prompt",
    )(q, k, v, *lam_p, g_col)


def _attn_sample_kernel(pt_ref, q_ref, ks_ref, vs_ref, lq1_ref, lk1_ref, lq2_ref, lk2_ref, g_ref, ck_hbm, cv_hbm,
                        o_ref, kbuf, vbuf, sem, *, n_pages, page, heads, qk, lam_init, scale):
    b = pl.program_id(0)
    n_seq = pl.num_programs(0)
    slot = b % 2
    hd = 2 * qk
    aw = heads * hd
    n_maps = 2 * heads
    page_rows = page * heads
    seq_rows = n_pages * page_rows

    def page_copies(seq, buf, j):
        src = pl.ds(pl.multiple_of(pt_ref[seq * n_pages + j] * page_rows, page_rows), page_rows)
        dst = pl.ds(buf * seq_rows + j * page_rows, page_rows)
        return (pltpu.make_async_copy(ck_hbm.at[src, :], kbuf.at[dst, :], sem.at[buf]),
                pltpu.make_async_copy(cv_hbm.at[src, :], vbuf.at[dst, :], sem.at[buf]))

    def start_fetch(seq, buf):
        for j in range(n_pages):
            for cp in page_copies(seq, buf, j):
                cp.start()

    @pl.when(b == 0)
    def _():
        start_fetch(0, 0)

    @pl.when(b + 1 < n_seq)
    def _():
        start_fetch(b + 1, 1 - slot)

    q = q_ref[...] * scale
    row = lax.broadcasted_iota(I32, (n_maps, aw), 0)
    lane = lax.broadcasted_iota(I32, (n_maps, aw), 1)
    own = (lane // hd == row % heads) & ((lane // qk) % 2 == row // heads)
    qm = jnp.where(own, jnp.broadcast_to(q, (n_maps, aw)), 0.0)
    qmb = qm.astype(BF16)
    for j in range(n_pages):
        for cp in page_copies(b, slot, j):
            cp.wait()

    def group_rows(buf_ref, g):
        n_tok = min(ATT_PAGE_GROUP, n_pages - g) * page
        base = slot * seq_rows + g * page_rows
        return jnp.concatenate([buf_ref[pl.ds(base + h, n_tok, stride=heads), :] for h in range(heads)],
                               axis=1).astype(BF16)

    groups = range(0, n_pages, ATT_PAGE_GROUP)
    s_pages = [_dot_nt(qmb, group_rows(kbuf, g)) for g in groups]
    s_self = jnp.sum(qm * ks_ref[...], axis=1, keepdims=True)
    m = s_self
    for s in s_pages:
        m = jnp.maximum(m, jnp.max(s, axis=1, keepdims=True))
    p_self = jnp.exp(s_self - m)
    p_pages = [jnp.exp(s - m) for s in s_pages]
    l = p_self
    for p in p_pages:
        l = l + jnp.sum(p, axis=1, keepdims=True)
    inv = 1.0 / l
    lam = _diff_lambda(lq1_ref[...], lk1_ref[...], lq2_ref[...], lk2_ref[...], lam_init)

    def mix(p):
        pn = p * inv
        return pn - lam * pltpu.roll(pn, heads, 0)

    acc = jnp.broadcast_to(mix(p_self), (n_maps, aw)) * vs_ref[...]
    for p, g in zip(p_pages, groups):
        acc = acc + _dot(mix(p).astype(BF16), group_rows(vbuf, g))
    pick = (row < heads) & (lane // hd == row)
    o = jnp.sum(jnp.where(pick, acc, 0.0), axis=0, keepdims=True)
    g = g_ref[...]
    for h in range(heads):
        o_ref[:, h * hd:(h + 1) * hd] = _head_norm(o[:, h * hd:(h + 1) * hd], g, lam_init)


def _attn_sample(q, k_new, v_new, cache_k, cache_v, page_ids, lam_p, g, *, page, heads, lam_init):
    bs, aw = q.shape
    n_pages = page_ids.shape[1]
    hd = aw // heads
    qk = hd // 2
    kern = functools.partial(_attn_sample_kernel, n_pages=n_pages, page=page, heads=heads, qk=qk,
                             lam_init=lam_init, scale=qk ** -0.5)
    row_spec = pl.BlockSpec((None, 1, aw), lambda b, pt: (b, 0, 0))
    small = pl.BlockSpec((1, qk), lambda b, pt: (0, 0))
    buf_rows = 2 * n_pages * page * heads
    grid_spec = pltpu.PrefetchScalarGridSpec(
        num_scalar_prefetch=1,
        grid=(bs,),
        in_specs=[row_spec, row_spec, row_spec, small, small, small, small,
                  pl.BlockSpec((1, hd), lambda b, pt: (0, 0)),
                  pl.BlockSpec(memory_space=pl.ANY), pl.BlockSpec(memory_space=pl.ANY)],
        out_specs=row_spec,
        scratch_shapes=[pltpu.VMEM((buf_rows, hd), F32), pltpu.VMEM((buf_rows, hd), F32),
                        pltpu.SemaphoreType.DMA((2,))],
    )
    out = pl.pallas_call(
        kern,
        grid_spec=grid_spec,
        out_shape=jax.ShapeDtypeStruct((bs, 1, aw), F32),
        compiler_params=_params("arbitrary"),
        name="diff_attn_sample",
    )(page_ids.reshape(-1).astype(I32), q.reshape(bs, 1, aw), k_new.reshape(bs, 1, aw),
      v_new.reshape(bs, 1, aw), *lam_p, g, cache_k, cache_v)
    return out.reshape(bs, aw)


def _layer_norm(z, g, b):
    mu = jnp.mean(z, axis=-1, keepdims=True)
    zc = z - mu
    var = jnp.mean(zc * zc, axis=-1, keepdims=True)
    return zc * lax.rsqrt(var + LN_EPS) * g + b


def _out_kernel(xp_ref, sp_ref, ap_ref, xs_ref, ss_ref, as_ref, wo_ref, g_ref, b_ref, wrh_ref, wrl_ref,
                br_ref, h_ref, e_ref, gt_ref, *, n_prompt, sw, alpha, top_k):
    i = pl.program_id(0)
    is_s = i == n_prompt
    rows = h_ref.shape[0]
    halves = [slice(r, r + rows // 2) for r in (0, rows // 2)]
    mixes = []
    for hs in halves:
        s = jnp.where(is_s, ss_ref[hs, :], sp_ref[hs, :]).astype(BF16)
        a = jnp.where(is_s, as_ref[hs, :], ap_ref[hs, :]).astype(BF16)
        mixes.append(_dot(s, wo_ref[:sw, :]) + _dot(a, wo_ref[sw:, :]))
    for hs, mix in zip(halves, mixes):
        x = jnp.where(is_s, xs_ref[hs, :], xp_ref[hs, :])
        h = _layer_norm(alpha * x + mix, g_ref[...], b_ref[...])
        h_ref[hs, :] = h
        h_hi = h.astype(BF16)
        h_lo = (h - h_hi.astype(F32)).astype(BF16)
        lg = (_dot_nt(wrh_ref[...], h_hi) + _dot_nt(wrl_ref[...], h_hi) + _dot_nt(wrh_ref[...], h_lo)
              + br_ref[...])
        eio = lax.broadcasted_iota(I32, lg.shape, 0)
        n_exp = lg.shape[0]
        vals, idxs = [], []
        for _ in range(top_k):
            m = jnp.max(lg, axis=0, keepdims=True)
            idx = jnp.min(jnp.where(lg == m, eio, n_exp), axis=0, keepdims=True)
            vals.append(m)
            idxs.append(idx)
            lg = jnp.where(eio == idx, NEG_BIG, lg)
        tv = jnp.concatenate(vals, axis=0)
        pe = jnp.exp(tv - tv[0:1, :])
        gt_ref[:, hs] = pe / jnp.sum(pe, axis=0, keepdims=True)
        e_ref[:, hs] = jnp.concatenate(idxs, axis=0)


def _out_route(xp, sp, ap, xs, ss, as_, w_out_bf, g, b, wr_hi, wr_lo, br, *, alpha):
    tp, d = xp.shape
    sw = sp.shape[1]
    rows = OUT_ROWS
    n_prompt = tp // rows
    total = tp + rows
    n_exp = wr_hi.shape[0]
    pidx = lambda i: (jnp.minimum(i, n_prompt - 1), 0)
    const = lambda i: (0, 0)
    kern = functools.partial(_out_kernel, n_prompt=n_prompt, sw=sw, alpha=alpha, top_k=TOP_K)
    return pl.pallas_call(
        kern,
        grid=(n_prompt + 1,),
        in_specs=[
            pl.BlockSpec((rows, d), pidx), pl.BlockSpec((rows, sw), pidx), pl.BlockSpec((rows, d - sw), pidx),
            pl.BlockSpec((rows, d), const), pl.BlockSpec((rows, sw), const), pl.BlockSpec((rows, d - sw), const),
            pl.BlockSpec(w_out_bf.shape, const), pl.BlockSpec((1, d), const), pl.BlockSpec((1, d), const),
            pl.BlockSpec((n_exp, d), const), pl.BlockSpec((n_exp, d), const), pl.BlockSpec((n_exp, 1), const),
        ],
        out_specs=[
            pl.BlockSpec((rows, d), lambda i: (i, 0)),
            pl.BlockSpec((TOP_K, rows), lambda i: (0, i)),
            pl.BlockSpec((TOP_K, rows), lambda i: (0, i)),
        ],
        out_shape=[
            jax.ShapeDtypeStruct((total, d), F32),
            jax.ShapeDtypeStruct((TOP_K, total), I32),
            jax.ShapeDtypeStruct((TOP_K, total), F32),
        ],
        compiler_params=_params("arbitrary"),
        name="out_proj_route",
    )(xp, sp, ap, xs, ss, as_, w_out_bf, g, b, wr_hi, wr_lo, br)


def _tile_plan(e_tile, tile_base, n_tok, n_exp):
    k_, tt = e_tile.shape
    valid = (tile_base + lax.broadcasted_iota(I32, (1, tt), 1)) < n_tok
    eio = lax.broadcasted_iota(I32, (n_exp, tt), 0)
    onehot = [jnp.where((eio == e_tile[k:k + 1, :]) & valid, 1.0, 0.0) for k in range(k_)]
    count_k = [jnp.sum(o, axis=1, keepdims=True) for o in onehot]
    count = functools.reduce(lambda a, b: a + b, count_k)
    padded = jnp.floor((count + (ROUTE_CHUNK - 1)) * (1.0 / ROUTE_CHUNK)) * ROUTE_CHUNK
    r = lax.broadcasted_iota(I32, (n_exp, n_exp), 0)
    c = lax.broadcasted_iota(I32, (n_exp, n_exp), 1)
    padded_row = jnp.sum(jnp.where(r == c, jnp.broadcast_to(padded, (n_exp, n_exp)), 0.0), axis=0, keepdims=True)
    offset = jnp.sum(jnp.where(c < r, jnp.broadcast_to(padded_row, (n_exp, n_exp)), 0.0), axis=1, keepdims=True)
    earlier = (lax.broadcasted_iota(I32, (tt, tt), 0) < lax.broadcasted_iota(I32, (tt, tt), 1)).astype(BF16)
    base = offset
    pos = []
    for k in range(k_):
        before = _dot(onehot[k].astype(BF16), earlier) + base
        pos.append(jnp.where(valid, jnp.sum(onehot[k] * before, axis=0, keepdims=True), -1.0))
        base = base + count_k[k]
    return pos


def _block_relative(pos, c):
    rel = pos - float(c * ROUTE_ROWS)
    return jnp.where((rel >= 0.0) & (rel < float(ROUTE_ROWS)), rel, -1.0).astype(BF16)


def _segment_copies(src_ref, dst_ref, n, make_copy):
    def body(c, carry):
        make_copy(pl.multiple_of(src_ref[0, c], ROUTE_CHUNK), pl.multiple_of(dst_ref[0, c], ROUTE_CHUNK)).start()
        return carry

    lax.fori_loop(0, n, body, 0)


def _wait_chunks(n, make_copy):
    def wait_many(c, carry):
        make_copy(ROUTE_WAIT * ROUTE_CHUNK).wait()
        return carry

    def wait_one(c, carry):
        make_copy(ROUTE_CHUNK).wait()
        return carry

    lax.fori_loop(0, n // ROUTE_WAIT, wait_many, 0)
    lax.fori_loop(0, n % ROUTE_WAIT, wait_one, 0)


def _dispatch_kernel(nchunk_ref, zlo_ref, zhi_ref, src_ref, dst_ref, h_ref, e_ref, x_hbm, pos_ref, xt_ref, zero_ref,
                     sem, *, n_tok, n_exp, block_rows):
    i = pl.program_id(0)
    tt = h_ref.shape[0]
    npos = xt_ref.shape[1]

    @pl.when(i == 0)
    def _():
        zero_ref[...] = jnp.zeros(zero_ref.shape, F32)

        def zero_block(b):
            rows0 = pl.multiple_of(b * block_rows, block_rows)
            return pltpu.make_async_copy(zero_ref, x_hbm.at[pl.ds(rows0, block_rows), :], sem)

        def start(b, c):
            zero_block(b).start()
            return c

        def wait(b, c):
            zero_block(b).wait()
            return c

        for e in range(n_exp + 1):
            lax.fori_loop(zlo_ref[e], zhi_ref[e], start, 0)
        for e in range(n_exp + 1):
            lax.fori_loop(zlo_ref[e], zhi_ref[e], wait, 0)

    slot = i % 2
    pos = _tile_plan(e_ref[...], i * tt, n_tok, n_exp)
    pos_ref[...] = jnp.concatenate(pos, axis=0)
    hb = h_ref[...].astype(BF16)
    rio = lax.broadcasted_iota(I32, (ROUTE_ROWS, tt), 0).astype(F32).astype(BF16)
    for c in range(npos // ROUTE_ROWS):
        rel = [_block_relative(p, c) for p in pos]
        sel = functools.reduce(lambda a, b: a | b, [rio == r for r in rel])
        one_hot = jnp.where(sel, jnp.ones((), BF16), jnp.zeros((), BF16))
        xt_ref[slot, c * ROUTE_ROWS:(c + 1) * ROUTE_ROWS, :] = _dot(one_hot, hb)

    def copy(src_row, dst_row, rows=ROUTE_CHUNK):
        return pltpu.make_async_copy(xt_ref.at[slot, pl.ds(src_row, rows), :], x_hbm.at[pl.ds(dst_row, rows), :], sem)

    @pl.when(i > 0)
    def _():
        _wait_chunks(nchunk_ref[i - 1], lambda rows: copy(0, 0, rows))

    _segment_copies(src_ref, dst_ref, nchunk_ref[i], copy)

    @pl.when(i == pl.num_programs(0) - 1)
    def _():
        _wait_chunks(nchunk_ref[i], lambda rows: copy(0, 0, rows))


def _dispatch(h_all, e_idx, tables, *, n_tok, n_exp, n_blocks):
    t_pad, d = h_all.shape
    tt = OUT_ROWS
    npos = TOP_K * tt + n_exp * ROUTE_CHUNK
    kern = functools.partial(_dispatch_kernel, n_tok=n_tok, n_exp=n_exp, block_rows=MOE_ROWS)
    chunk_list = pl.BlockSpec((None, 1, npos // ROUTE_CHUNK), lambda i, *_: (i, 0, 0), memory_space=pltpu.SMEM)
    grid_spec = pltpu.PrefetchScalarGridSpec(
        num_scalar_prefetch=3,
        grid=(t_pad // tt,),
        in_specs=[chunk_list, chunk_list,
                  pl.BlockSpec((tt, d), lambda i, *_: (i, 0)),
                  pl.BlockSpec((TOP_K, tt), lambda i, *_: (0, i))],
        out_specs=[pl.BlockSpec(memory_space=pl.ANY), pl.BlockSpec((TOP_K, tt), lambda i, *_: (0, i))],
        scratch_shapes=[pltpu.VMEM((2, npos, d), F32), pltpu.VMEM((MOE_ROWS, d), F32), pltpu.SemaphoreType.DMA],
    )
    return pl.pallas_call(
        kern,
        grid_spec=grid_spec,
        out_shape=[jax.ShapeDtypeStruct((n_blocks * MOE_ROWS, d), F32), jax.ShapeDtypeStruct((TOP_K, t_pad), F32)],
        compiler_params=_params("arbitrary"),
        name="moe_dispatch",
    )(tables["nchunk"], tables["zlo"], tables["zhi"], tables["src"], tables["dst"], h_all, e_idx)


def _moe_kernel(be_ref, nu_ref, slot_ref, next_ref, x_ref, wgu_hbm, bgu_ref, wdn_hbm, bdn_ref, y_ref,
                wgu_f, wdn_f, wgu_bf, wdn_bf, wsem, *, de):
    i = pl.program_id(0)

    def fetch(e, slot):
        return (pltpu.make_async_copy(wgu_hbm.at[e], wgu_f.at[slot], wsem.at[slot]),
                pltpu.make_async_copy(wdn_hbm.at[e], wdn_f.at[slot], wsem.at[slot]))

    @pl.when(i >= nu_ref[0])
    def _():
        y_ref[...] = jnp.zeros(y_ref.shape, F32)

    @pl.when(i < nu_ref[0])
    def _():
        slot = slot_ref[i]

        @pl.when(i == 0)
        def _():
            for cp in fetch(be_ref[0], 0):
                cp.start()

        @pl.when((i == 0) | (be_ref[i] != be_ref[jnp.maximum(i - 1, 0)]))
        def _():
            for cp in fetch(be_ref[i], slot):
                cp.wait()
            wgu_bf[...] = wgu_f[slot].astype(BF16)
            wdn_bf[...] = wdn_f[slot].astype(BF16)

            @pl.when(next_ref[i] >= 0)
            def _():
                for cp in fetch(next_ref[i], 1 - slot):
                    cp.start()

        x = x_ref[...].astype(BF16)
        cw = MOE_HIDDEN_CHUNK
        n_chunks = de // cw

        def up(c):
            lo, hi = c * cw, (c + 1) * cw
            return (_dot(x, wgu_bf[:, lo:hi]) + bgu_ref[:, lo:hi],
                    _dot(x, wgu_bf[:, de + lo:de + hi]) + bgu_ref[:, de + lo:de + hi])

        pending = {c: up(c) for c in range(min(MOE_LOOKAHEAD, n_chunks))}
        y = bdn_ref[...]
        for c in range(n_chunks):
            if c + MOE_LOOKAHEAD < n_chunks:
                pending[c + MOE_LOOKAHEAD] = up(c + MOE_LOOKAHEAD)
            glu, lin = pending.pop(c)
            glu = jnp.minimum(glu, SWIGLU_LIMIT)
            lin = jnp.clip(lin, -SWIGLU_LIMIT, SWIGLU_LIMIT)
            act = glu * jax.nn.sigmoid(SWIGLU_ALPHA * glu) * (lin + 1.0)
            y = y + _dot(act.astype(BF16), wdn_bf[c * cw:(c + 1) * cw, :])
        y_ref[...] = y


def _moe(x_rows, block_expert, n_used, w_gu, b_gu, w_dn, b_dn):
    rows = MOE_ROWS
    n_blocks = block_expert.shape[0]
    n_exp, d, de2 = w_gu.shape
    de = de2 // 2
    kern = functools.partial(_moe_kernel, de=de)
    idx = jnp.arange(n_blocks, dtype=I32)
    in_use = idx < n_used[0]
    changed = jnp.concatenate([jnp.ones((1,), bool), block_expert[1:] != block_expert[:-1]])
    w_slot = ((jnp.cumsum(changed.astype(I32)) - 1) % 2).astype(I32)
    later = (block_expert[None, :] > block_expert[:, None]) & in_use[None, :]
    next_e = jnp.min(jnp.where(later, block_expert[None, :], n_exp), axis=1)
    next_e = jnp.where(next_e < n_exp, next_e, -1).astype(I32)

    def used(i, nu):
        return jnp.minimum(i, nu[0] - 1)

    def expert(i, be, nu, *_):
        return (be[used(i, nu)], 0, 0)

    grid_spec = pltpu.PrefetchScalarGridSpec(
        num_scalar_prefetch=4,
        grid=(n_blocks,),
        in_specs=[
            pl.BlockSpec((rows, d), lambda i, be, nu, *_: (used(i, nu), 0)),
            pl.BlockSpec(memory_space=pl.ANY),
            pl.BlockSpec((None, 1, de2), expert),
            pl.BlockSpec(memory_space=pl.ANY),
            pl.BlockSpec((None, 1, d), expert),
        ],
        out_specs=pl.BlockSpec((rows, d), lambda i, *_: (i, 0)),
        scratch_shapes=[pltpu.VMEM((2, d, de2), F32), pltpu.VMEM((2, de, d), F32),
                        pltpu.VMEM((d, de2), BF16), pltpu.VMEM((de, d), BF16), pltpu.SemaphoreType.DMA((2,))],
    )
    return pl.pallas_call(
        kern,
        grid_spec=grid_spec,
        out_shape=jax.ShapeDtypeStruct(x_rows.shape, F32),
        compiler_params=_params("arbitrary"),
        name="moe_experts",
    )(block_expert, n_used, w_slot, next_e, x_rows, w_gu, b_gu.reshape(n_exp, 1, de2), w_dn,
      b_dn.reshape(n_exp, 1, d))


def _route_tables(e_idx, n_tok, n_exp):
    k_, t_pad = e_idx.shape
    tt = OUT_ROWS
    rows = MOE_ROWS
    tiles = t_pad // tt
    valid = jnp.arange(t_pad, dtype=I32) < n_tok
    e_m = jnp.where(valid[None, :], e_idx, -1).reshape(k_, tiles, tt, 1)
    cnt = jnp.sum((e_m == jnp.arange(n_exp, dtype=I32)).astype(I32), axis=(0, 2))
    cpad = ((cnt + ROUTE_CHUNK - 1) // ROUTE_CHUNK) * ROUTE_CHUNK
    off = jnp.cumsum(cpad, axis=1) - cpad
    total = jnp.sum(cpad, axis=0)
    region = ((total + rows - 1) // rows) * rows
    pend = jnp.cumsum(region)
    pstart = pend - region
    gbase = pstart[None, :] + jnp.cumsum(cpad, axis=0) - cpad
    n_blocks = -(-(k_ * n_tok + n_exp * (tiles * (ROUTE_CHUNK - 1) + rows - 1)) // rows)
    block_start = jnp.arange(n_blocks, dtype=I32) * rows
    block_expert = jnp.minimum(jnp.sum((pend[None, :] <= block_start[:, None]).astype(I32), axis=1), n_exp - 1)
    n_used = (pend[-1] // rows).astype(I32)
    zlo = jnp.concatenate([(pstart + total) // rows, n_used[None]]).astype(I32)
    zhi = jnp.concatenate([pend // rows, jnp.full((1,), n_blocks, I32)]).astype(I32)
    nch = cpad // ROUTE_CHUNK
    cum = jnp.cumsum(nch, axis=1)
    c_id = jnp.arange((k_ * tt + n_exp * ROUTE_CHUNK) // ROUTE_CHUNK, dtype=I32)
    owner = jnp.sum((cum[:, None, :] <= c_id[None, :, None]).astype(I32), axis=2)
    pick = (jnp.minimum(owner, n_exp - 1)[:, :, None] == jnp.arange(n_exp, dtype=I32)).astype(I32)
    within = (c_id[None, :] - jnp.sum(pick * (cum - nch)[:, None, :], axis=2)) * ROUTE_CHUNK
    src = jnp.sum(pick * off[:, None, :], axis=2) + within
    dst = jnp.sum(pick * gbase[:, None, :], axis=2) + within
    tables = dict(nchunk=cum[:, -1].astype(I32), src=src[:, None, :].astype(I32), dst=dst[:, None, :].astype(I32),
                  zlo=zlo, zhi=zhi)
    return tables, block_expert.astype(I32), n_used.reshape(1), n_blocks


def _final_kernel(nchunk_ref, srcc_ref, dstc_ref, srcn_ref, dstn_ref, pos_ref, gt_ref, h_ref, g_ref, b_ref, y_hbm,
                  op_ref, os_ref, yt_ref, sem, *, n_prompt, alpha):
    i = pl.program_id(0)
    n_tiles = pl.num_programs(0)
    tt = h_ref.shape[0]
    npos = yt_ref.shape[1]
    slot = i % 2

    def fetch(buf):
        def copy(local_row, global_row, rows=ROUTE_CHUNK):
            return pltpu.make_async_copy(y_hbm.at[pl.ds(global_row, rows), :],
                                         yt_ref.at[buf, pl.ds(local_row, rows), :], sem.at[buf])
        return copy

    @pl.when(i == 0)
    def _():
        yt_ref[...] = jnp.zeros(yt_ref.shape, F32)
        _segment_copies(srcc_ref, dstc_ref, nchunk_ref[0], fetch(0))

    @pl.when(i + 1 < n_tiles)
    def _():
        _segment_copies(srcn_ref, dstn_ref, nchunk_ref[jnp.minimum(i + 1, n_tiles - 1)], fetch(1 - slot))

    top_k = pos_ref.shape[0]
    rows = jnp.concatenate([pos_ref[...], gt_ref[...], jnp.zeros((LANES - 2 * top_k, tt), F32)], axis=0)
    cols = rows.T

    _wait_chunks(nchunk_ref[i], lambda rows: fetch(slot)(0, 0, rows))
    f = jnp.zeros((tt, h_ref.shape[1]), F32)
    lio = lax.broadcasted_iota(I32, (tt, ROUTE_ROWS), 1).astype(F32).astype(BF16)
    gate_cols = [cols[:, top_k + k:top_k + k + 1].astype(BF16) for k in range(top_k)]
    for c in range(npos // ROUTE_ROWS):
        w = jnp.zeros((tt, ROUTE_ROWS), BF16)
        for k in range(top_k):
            w = jnp.where(lio == _block_relative(cols[:, k:k + 1], c), gate_cols[k], w)
        f = f + _dot(w, yt_ref[slot, c * ROUTE_ROWS:(c + 1) * ROUTE_ROWS, :].astype(BF16))
    out = _layer_norm(alpha * h_ref[...] + f, g_ref[...], b_ref[...])

    @pl.when(i < n_prompt)
    def _():
        op_ref[...] = out

    @pl.when(i == n_prompt)
    def _():
        os_ref[...] = out[:os_ref.shape[0], :]


def _final(y_rows, pos, gates, h_all, g, b, tables, *, tp, ts, n_exp, alpha):
    tt = OUT_ROWS
    t_pad, d = h_all.shape
    n_prompt = tp // tt
    npos = TOP_K * tt + n_exp * ROUTE_CHUNK
    const = lambda i, *_: (0, 0)
    kern = functools.partial(_final_kernel, n_prompt=n_prompt, alpha=alpha)
    n_tiles = t_pad // tt
    n_chunks = npos // ROUTE_CHUNK
    list_cur = pl.BlockSpec((None, 1, n_chunks), lambda i, *_: (i, 0, 0), memory_space=pltpu.SMEM)
    list_nxt = pl.BlockSpec((None, 1, n_chunks), lambda i, *_: (jnp.minimum(i + 1, n_tiles - 1), 0, 0),
                            memory_space=pltpu.SMEM)
    grid_spec = pltpu.PrefetchScalarGridSpec(
        num_scalar_prefetch=1,
        grid=(n_tiles,),
        in_specs=[list_cur, list_cur, list_nxt, list_nxt,
                  pl.BlockSpec((TOP_K, tt), lambda i, *_: (0, i)),
                  pl.BlockSpec((TOP_K, tt), lambda i, *_: (0, i)),
                  pl.BlockSpec((tt, d), lambda i, *_: (i, 0)),
                  pl.BlockSpec((1, d), const), pl.BlockSpec((1, d), const),
                  pl.BlockSpec(memory_space=pl.ANY)],
        out_specs=[pl.BlockSpec((tt, d), lambda i, *_: (jnp.minimum(i, n_prompt - 1), 0)),
                   pl.BlockSpec((ts, d), const)],
        scratch_shapes=[pltpu.VMEM((2, npos, d), F32), pltpu.SemaphoreType.DMA((2,))],
    )
    return pl.pallas_call(
        kern,
        grid_spec=grid_spec,
        out_shape=[jax.ShapeDtypeStruct((tp, d), F32), jax.ShapeDtypeStruct((ts, d), F32)],
        compiler_params=_params("arbitrary"),
        name="combine_ln",
    )(tables["nchunk"], tables["src"], tables["dst"], tables["src"], tables["dst"], pos, gates, h_all, g, b,
      y_rows)


def _pad_rows(a, rows):
    return jnp.pad(a, ((0, rows - a.shape[0]), (0, 0)))


def kernel(x_prompt, x_sample, cache_k, cache_v, state_ssm_re, state_ssm_im, page_table, w_in, w_out, ssm_a_re, ssm_a_im, ssm_log_dt, ssm_b_re, ssm_b_im, ssm_c_re, ssm_c_im, ssm_d, w_glu, b_glu, lambda_q1, lambda_k1, lambda_q2, lambda_k2, subln_g, ln1_g, ln1_b, w_router, b_router, w_gate_up, b_gate_up, w_down, b_down, ln2_g, ln2_b):
    depth = w_in.shape[0]
    bp, lp, d = x_prompt.shape
    bs, ls, _ = x_sample.shape
    n_phys, page, heads, hd = cache_k.shape[1:]
    assert ls == 1 and hd == LANES
    qk = hd // 2
    aw = heads * hd
    groups, n_state = ssm_a_re.shape[1:]
    sw = groups * ssm_b_re.shape[-1]
    gp = groups * n_state
    n_exp = w_router.shape[2]
    past_len = page_table.shape[1] * page
    alpha = (2 * depth) ** 0.25
    tp = bp * lp
    n_tok = tp + bs

    cos_p, sin_p = _rope_tables(jnp.arange(lp, dtype=I32), qk // 2)
    cos_s, sin_s = _rope_tables(jnp.full((bs,), past_len, I32), qk // 2)
    ck = cache_k.reshape(depth * n_phys * page * heads, hd)
    cv = cache_v.reshape(depth * n_phys * page * heads, hd)

    hp = x_prompt
    hs = x_sample.reshape(1, bs, d)
    outs = [[] for _ in range(8)]
    for l in range(depth):
        lam_init = 0.8 - 0.6 * math.exp(-0.3 * l)
        w_in_bf = w_in[l].astype(BF16)
        w_out_bf = w_out[l].astype(BF16)
        w_glu_bf = w_glu[l].astype(BF16)
        bg = b_glu[l].reshape(1, 2 * sw)
        lam_p = [p[l].reshape(1, qk).astype(F32) for p in (lambda_q1, lambda_k1, lambda_q2, lambda_k2)]
        prm = _ssm_params(ssm_a_re[l], ssm_a_im[l], ssm_log_dt[l], ssm_b_re[l], ssm_b_im[l],
                          ssm_c_re[l], ssm_c_im[l], ssm_d[l])

        tt = min(PROJ_ROWS // bp, lp)
        u_p, q_p, k_p, v_p = _project(hp, w_in_bf, cos_p, sin_p, tt=tt, sw=sw, aw=aw, half=qk // 2)
        zero = jnp.zeros((bp, gp), F32)
        ssm_p, sre_p, sim_p = _ssm(u_p, zero, zero, prm, nb=bp, steps=min(SSM_STEPS, lp), w_glu_bf=w_glu_bf,
                                   b_glu=bg)
        att_p = _attn_prompt(q_p, k_p, v_p, lam_p, subln_g[l].reshape(hd, 1), heads=heads, lam_init=lam_init)

        u_s, q_s, k_s, v_s = _project(hs, w_in_bf, cos_s, sin_s, tt=bs, sw=sw, aw=aw, half=qk // 2)
        ssm_s, sre_s, sim_s = _ssm(u_s, state_ssm_re[l].reshape(bs, gp), state_ssm_im[l].reshape(bs, gp), prm,
                                   nb=bs, steps=1, w_glu_bf=w_glu_bf, b_glu=bg)
        att_s = _attn_sample(q_s.reshape(bs, aw), k_s.reshape(bs, aw), v_s.reshape(bs, aw), ck, cv,
                             page_table + l * n_phys, lam_p, subln_g[l].reshape(1, hd), page=page, heads=heads,
                             lam_init=lam_init)

        wr = w_router[l].T.astype(F32)
        wr_hi = wr.astype(BF16)
        wr_lo = (wr - wr_hi.astype(F32)).astype(BF16)
        h_all, e_idx, gates = _out_route(
            hp.reshape(tp, d), ssm_p.reshape(tp, sw), att_p, _pad_rows(hs.reshape(bs, d), OUT_ROWS),
            _pad_rows(ssm_s, OUT_ROWS), _pad_rows(att_s, OUT_ROWS), w_out_bf, ln1_g[l].reshape(1, d),
            ln1_b[l].reshape(1, d), wr_hi, wr_lo, b_router[l].reshape(n_exp, 1), alpha=alpha)
        tables, block_expert, n_used, n_blocks = _route_tables(e_idx, n_tok, n_exp)
        x_rows, pos = _dispatch(h_all, e_idx, tables, n_tok=n_tok, n_exp=n_exp, n_blocks=n_blocks)
        y_rows = _moe(x_rows, block_expert, n_used, w_gate_up[l], b_gate_up[l], w_down[l], b_down[l])
        y_p, y_s = _final(y_rows, pos, gates, h_all, ln2_g[l].reshape(1, d), ln2_b[l].reshape(1, d), tables,
                          tp=tp, ts=bs, n_exp=n_exp, alpha=alpha)
        hp = y_p.reshape(bp, lp, d)
        hs = y_s.reshape(1, bs, d)

        for lst, val in zip(outs, (
                k_p.reshape(bp, lp, heads, hd), v_p.reshape(bp, lp, heads, hd),
                sre_p.reshape(bp, groups, n_state), sim_p.reshape(bp, groups, n_state),
                k_s.reshape(bs, ls, heads, hd), v_s.reshape(bs, ls, heads, hd),
                sre_s.reshape(bs, groups, n_state), sim_s.reshape(bs, groups, n_state))):
            lst.append(val)

    return (hp, hs.reshape(bs, ls, d)) + tuple(jnp.stack(o) for o in outs)
```

```python
import functools
import math

import jax
import jax.numpy as jnp
from jax import lax
from jax.experimental import pallas as pl
from jax.experimental.pallas import tpu as pltpu

F32 = jnp.float32
BF16 = jnp.bfloat16
I32 = jnp.int32

TOP_K = 4
ROPE_THETA = 10000.0
SWIGLU_ALPHA = 1.702
SWIGLU_LIMIT = 7.0
LN_EPS = 1e-5
RMS_EPS = 1e-5
SSM_A_RE_MAX = -1e-4
GELU_C0 = math.sqrt(2.0 / math.pi)
GELU_C1 = 0.044715
NEG_BIG = -1e30
LOG2E = math.log2(math.e)

LANES = 128
SUBLANES = 8
MXU_DIM = 256
VMEM_LIMIT_BYTES = 56 * 1024 * 1024

PROJ_ROWS = 1024
SSM_STEPS = 128
ATT_ROWS = 256
ATT_LOOKAHEAD = 4
ATT_PAGE_GROUP = 4
MOE_HIDDEN_CHUNK = 256
MOE_LOOKAHEAD = 2
ROUTE_CHUNK = 8
ROUTE_ROWS = 256
ROUTE_WAIT = 32
OUT_ROWS = 512
MOE_ROWS = 256
SCAN_LANES = 512


def _params(*sem):
    return pltpu.CompilerParams(dimension_semantics=sem, vmem_limit_bytes=VMEM_LIMIT_BYTES)


def _dot(a, b):
    return jnp.dot(a, b, preferred_element_type=F32)


def _dot_nt(a, b):
    return lax.dot_general(a, b, (((1,), (1,)), ((), ())), preferred_element_type=F32)


def _proj_kernel(x_ref, w_ref, cos_ref, sin_ref, u_ref, q_ref, k_ref, v_ref, *, nb, tt, sw, aw, half):
    d = x_ref.shape[-1]
    heads = aw // LANES
    acc = _dot(x_ref[...].reshape(nb * tt, d).astype(BF16), w_ref[...])
    c = cos_ref[...]
    s = sin_ref[...]
    lane = lax.broadcasted_iota(I32, c.shape, 1)
    first = (lane % (2 * half)) < half

    def rope(blk):
        swapped = jnp.where(first, pltpu.roll(blk, LANES - half, 1), pltpu.roll(blk, half, 1))
        return blk * c + swapped * s

    for b in range(nb):
        rows = slice(b * tt, (b + 1) * tt)
        for j in range(sw // LANES):
            u_ref[j, pl.ds(b, tt, stride=nb), :] = acc[rows, j * LANES:(j + 1) * LANES]
        q_ref[b] = jnp.concatenate(
            [rope(acc[rows, sw + j * LANES:sw + (j + 1) * LANES]) for j in range(heads)], axis=1)
        for j in range(heads):
            lo = sw + aw + j * LANES
            k_ref[b, pl.ds(j, tt, stride=heads), :] = rope(acc[rows, lo:lo + LANES])
            lo = sw + 2 * aw + j * LANES
            v_ref[b, pl.ds(j, tt, stride=heads), :] = acc[rows, lo:lo + LANES]


def _project(x3, w_bf, cos_t, sin_t, *, tt, sw, aw, half):
    nb, seq, d = x3.shape
    heads = aw // LANES
    kern = functools.partial(_proj_kernel, nb=nb, tt=tt, sw=sw, aw=aw, half=half)
    return pl.pallas_call(
        kern,
        grid=(seq // tt,),
        in_specs=[
            pl.BlockSpec((nb, tt, d), lambda i: (0, i, 0)),
            pl.BlockSpec(w_bf.shape, lambda i: (0, 0)),
            pl.BlockSpec((tt, LANES), lambda i: (i, 0)),
            pl.BlockSpec((tt, LANES), lambda i: (i, 0)),
        ],
        out_specs=[
            pl.BlockSpec((sw // LANES, tt * nb, LANES), lambda i: (0, i, 0)),
            pl.BlockSpec((nb, tt, aw), lambda i: (0, i, 0)),
            pl.BlockSpec((nb, tt * heads, LANES), lambda i: (0, i, 0)),
            pl.BlockSpec((nb, tt * heads, LANES), lambda i: (0, i, 0)),
        ],
        out_shape=[
            jax.ShapeDtypeStruct((sw // LANES, seq * nb, LANES), F32),
            jax.ShapeDtypeStruct((nb, seq, aw), F32),
            jax.ShapeDtypeStruct((nb, seq * heads, LANES), F32),
            jax.ShapeDtypeStruct((nb, seq * heads, LANES), F32),
        ],
        compiler_params=_params("arbitrary"),
        name="proj_rope",
    )(x3, w_bf, cos_t, sin_t)


def _rope_tables(pos, half):
    inv = ROPE_THETA ** (-jnp.arange(half, dtype=F32) / half)
    ang = pos.astype(F32)[:, None] * inv[None, :]
    reps = LANES // half
    cos = jnp.tile(jnp.cos(ang), (1, reps))
    sin = jnp.sin(ang)
    sin = jnp.tile(jnp.concatenate([-sin, sin], axis=1), (1, reps // 2))
    return cos, sin


def _ssm_kernel(u_ref, s0r_ref, s0i_ref, ar_ref, ai_ref, wb_ref, wc_ref, d_ref, wg_ref, bg_ref,
                o_ref, sr_ref, si_ref, bu_ref, st_ref, ot_ref, *, nb, steps, gp, sw):
    i = pl.program_id(0)
    rows = nb * steps

    @pl.when(i == 0)
    def _():
        st_ref[:, :gp] = s0r_ref[...]
        st_ref[:, gp:] = s0i_ref[...]

    u = jnp.concatenate([u_ref[j] for j in range(sw // LANES)], axis=1)
    ub = u.astype(BF16)
    n_tiles = 2 * gp // MXU_DIM
    per_half = gp // MXU_DIM
    in_tiles = sw // MXU_DIM
    for j in range(n_tiles):
        kt = ((j % per_half) * in_tiles) // per_half
        bu_ref[:, j * MXU_DIM:(j + 1) * MXU_DIM] = _dot(ub[:, kt * MXU_DIM:(kt + 1) * MXU_DIM], wb_ref[j])

    for c in range(gp // SCAN_LANES):
        lo_r = c * SCAN_LANES
        lo_i = gp + c * SCAN_LANES
        ar = jnp.broadcast_to(ar_ref[:, lo_r:lo_r + SCAN_LANES], (nb, SCAN_LANES))
        ai = jnp.broadcast_to(ai_ref[:, lo_r:lo_r + SCAN_LANES], (nb, SCAN_LANES))

        def step(t, carry, lo_r=lo_r, lo_i=lo_i, ar=ar, ai=ai):
            sr, si = carry
            r0 = pl.multiple_of(t * nb, nb)
            br = bu_ref[pl.ds(r0, nb), lo_r:lo_r + SCAN_LANES]
            bi = bu_ref[pl.ds(r0, nb), lo_i:lo_i + SCAN_LANES]
            nr = ar * sr - ai * si + br
            ni = ar * si + ai * sr + bi
            bu_ref[pl.ds(r0, nb), lo_r:lo_r + SCAN_LANES] = nr
            bu_ref[pl.ds(r0, nb), lo_i:lo_i + SCAN_LANES] = ni
            return nr, ni

        sr0 = st_ref[:, lo_r:lo_r + SCAN_LANES]
        si0 = st_ref[:, lo_i:lo_i + SCAN_LANES]
        if steps == 1:
            sr1, si1 = step(0, (sr0, si0))
        else:
            sr1, si1 = lax.fori_loop(0, steps, step, (sr0, si0))
        st_ref[:, lo_r:lo_r + SCAN_LANES] = sr1
        st_ref[:, lo_i:lo_i + SCAN_LANES] = si1

    sr_ref[...] = st_ref[:, :gp]
    si_ref[...] = st_ref[:, gp:]

    out_tiles = sw // MXU_DIM
    kw = gp // out_tiles
    ys = []
    for nt in range(out_tiles):
        s_re = bu_ref[:, nt * kw:(nt + 1) * kw].astype(BF16)
        s_im = bu_ref[:, gp + nt * kw:gp + (nt + 1) * kw].astype(BF16)
        ys.append(_dot(s_re, wc_ref[nt, 0]) + _dot(s_im, wc_ref[nt, 1]))
    y = jnp.concatenate(ys, axis=1) + d_ref[...] * u
    y = 0.5 * y * (1.0 + jnp.tanh(GELU_C0 * (y + GELU_C1 * (y * y * y))))
    z = _dot(y.astype(BF16), wg_ref[...]) + bg_ref[...]
    out = z[:, :sw] * jax.nn.sigmoid(z[:, sw:])
    if steps == 1:
        o_ref[...] = out
    else:
        for j in range(sw // LANES):
            ot_ref[j] = out[:, j * LANES:(j + 1) * LANES]
        for b in range(nb):
            for j in range(sw // LANES):
                o_ref[b, :, j * LANES:(j + 1) * LANES] = ot_ref[j, pl.ds(b, steps, stride=nb), :]


def _ssm(u_tb, s0r, s0i, prm, *, nb, steps, w_glu_bf, b_glu):
    n_slabs, n_rows, _ = u_tb.shape
    sw = n_slabs * LANES
    seq = n_rows // nb
    gp = prm["a_re"].shape[1]
    rows = nb * steps
    grid = (seq // steps,)
    const2 = lambda i: (0, 0)
    kern = functools.partial(_ssm_kernel, nb=nb, steps=steps, gp=gp, sw=sw)
    if steps == 1:
        o_spec = pl.BlockSpec((nb, sw), const2)
        o_shape = jax.ShapeDtypeStruct((nb, sw), F32)
    else:
        o_spec = pl.BlockSpec((nb, steps, sw), lambda i: (0, i, 0))
        o_shape = jax.ShapeDtypeStruct((nb, seq, sw), F32)
    return pl.pallas_call(
        kern,
        grid=grid,
        in_specs=[
            pl.BlockSpec((n_slabs, rows, LANES), lambda i: (0, i, 0)),
            pl.BlockSpec((nb, gp), const2),
            pl.BlockSpec((nb, gp), const2),
            pl.BlockSpec((1, gp), const2),
            pl.BlockSpec((1, gp), const2),
            pl.BlockSpec(prm["wb"].shape, lambda i: (0, 0, 0)),
            pl.BlockSpec(prm["wc"].shape, lambda i: (0, 0, 0, 0)),
            pl.BlockSpec((1, sw), const2),
            pl.BlockSpec(w_glu_bf.shape, const2),
            pl.BlockSpec((1, 2 * sw), const2),
        ],
        out_specs=[o_spec, pl.BlockSpec((nb, gp), const2), pl.BlockSpec((nb, gp), const2)],
        out_shape=[o_shape, jax.ShapeDtypeStruct((nb, gp), F32), jax.ShapeDtypeStruct((nb, gp), F32)],
        scratch_shapes=[
            pltpu.VMEM((rows, 2 * gp), F32),
            pltpu.VMEM((nb, 2 * gp), F32),
            pltpu.VMEM((sw // LANES, rows, LANES), F32),
        ],
        compiler_params=_params("arbitrary"),
        name="ssm_mixer",
    )(u_tb, s0r, s0i, prm["a_re"], prm["a_im"], prm["wb"], prm["wc"], prm["d"], w_glu_bf, b_glu)


def _ssm_params(a_re, a_im, log_dt, b_re, b_im, c_re, c_im, d):
    g_, p_ = a_re.shape
    h_ = b_re.shape[-1]
    gp = g_ * p_
    sw = g_ * h_
    a_re = jnp.minimum(a_re.astype(F32), SSM_A_RE_MAX)
    a_im = a_im.astype(F32)
    dt = jnp.exp(log_dt.astype(F32))[:, None]
    mag = jnp.exp(a_re * dt)
    lb_re = mag * jnp.cos(a_im * dt)
    lb_im = mag * jnp.sin(a_im * dt)
    den = jnp.square(a_re) + jnp.square(a_im)
    nr, ni = lb_re - 1.0, lb_im
    f_re = (nr * a_re + ni * a_im) / den
    f_im = (ni * a_re - nr * a_im) / den
    b_re = b_re.astype(F32)
    b_im = b_im.astype(F32)
    bb_re = f_re[..., None] * b_re - f_im[..., None] * b_im
    bb_im = f_re[..., None] * b_im + f_im[..., None] * b_re
    eye = jnp.eye(g_, dtype=F32)
    w_in_re = jnp.einsum("gph,gk->ghkp", bb_re, eye).reshape(sw, gp)
    w_in_im = jnp.einsum("gph,gk->ghkp", bb_im, eye).reshape(sw, gp)
    w_in = jnp.concatenate([w_in_re, w_in_im], axis=1)
    per_half = gp // MXU_DIM
    in_tiles = sw // MXU_DIM
    n_tiles = 2 * per_half
    kt = [((j % per_half) * in_tiles) // per_half for j in range(n_tiles)]
    w_tiles = w_in.astype(BF16).reshape(in_tiles, MXU_DIM, n_tiles, MXU_DIM).transpose(2, 0, 1, 3)
    wb = w_tiles[jnp.arange(n_tiles), jnp.asarray(kt, I32)]
    w_out_re = jnp.einsum("ghp,gk->gpkh", c_re.astype(F32), eye).reshape(gp, sw)
    w_out_im = -jnp.einsum("ghp,gk->gpkh", c_im.astype(F32), eye).reshape(gp, sw)
    out_tiles = sw // MXU_DIM
    kw = gp // out_tiles
    wc = jnp.stack([
        jnp.stack([w_out_re[nt * kw:(nt + 1) * kw, nt * MXU_DIM:(nt + 1) * MXU_DIM],
                   w_out_im[nt * kw:(nt + 1) * kw, nt * MXU_DIM:(nt + 1) * MXU_DIM]])
        for nt in range(out_tiles)]).astype(BF16)
    return dict(a_re=lb_re.reshape(1, gp), a_im=lb_im.reshape(1, gp), wb=wb, wc=wc,
                d=d.astype(F32).reshape(1, sw))


def _diff_lambda(lq1, lk1, lq2, lk2, lam_init):
    l1 = jnp.exp(jnp.sum(lq1 * lk1, axis=1, keepdims=True))
    l2 = jnp.exp(jnp.sum(lq2 * lk2, axis=1, keepdims=True))
    return l1 - l2 + lam_init


def _head_norm(o, g, lam_init):
    ms = jnp.mean(o * o, axis=-1, keepdims=True)
    return o * lax.rsqrt(ms + RMS_EPS) * g * (1.0 - lam_init)


def _attn_prompt_kernel(q_ref, k_ref, v_ref, lq1_ref, lk1_ref, lq2_ref, lk2_ref, g_ref, o_ref,
                        kb_ref, vt_ref, qs_ref, acc_ref, *, tq, qk, heads, lam_init, scale):
    h = pl.program_id(1)
    nq = vt_ref.shape[0]
    seq = nq * tq
    kb_ref[...] = k_ref[pl.ds(h, seq, stride=heads), :].astype(BF16)
    for jb in range(nq):
        vt_ref[jb] = v_ref[pl.ds(jb * tq * heads + h, tq, stride=heads), :].T.astype(BF16)
    q = q_ref[...] * (scale * LOG2E)
    lane = lax.broadcasted_iota(I32, q.shape, 1)
    qs_ref[0] = jnp.where(lane < qk, q, 0.0).astype(BF16)
    qs_ref[1] = jnp.where(lane >= qk, q, 0.0).astype(BF16)
    acc_ref[...] = jnp.zeros(acc_ref.shape, F32)
    key = lax.broadcasted_iota(I32, (tq, tq), 0)
    qry = lax.broadcasted_iota(I32, (tq, tq), 1)
    keep = key <= qry
    m = [[jnp.full((1, tq), NEG_BIG, F32)] * 2 for _ in range(nq)]
    l = [[jnp.zeros((1, tq), F32)] * 2 for _ in range(nq)]
    tasks = [(j, i, c) for j in range(nq) for i in range(j, nq) for c in range(2)]

    def scores(t):
        j, i, c = tasks[t]
        st = _dot_nt(kb_ref[j * tq:(j + 1) * tq, :], qs_ref[c, i * tq:(i + 1) * tq, :])
        return jnp.where(keep, st, NEG_BIG) if i == j else st

    pending = {t: scores(t) for t in range(min(ATT_LOOKAHEAD, len(tasks)))}
    for t, (j, i, c) in enumerate(tasks):
        if t + ATT_LOOKAHEAD < len(tasks):
            pending[t + ATT_LOOKAHEAD] = scores(t + ATT_LOOKAHEAD)
        st = pending.pop(t)
        m_new = jnp.maximum(m[i][c], jnp.max(st, axis=0, keepdims=True))
        alpha = jnp.exp2(m[i][c] - m_new)
        p = jnp.exp2(st - m_new)
        acc_ref[i, c] = alpha * acc_ref[i, c] + _dot(vt_ref[j], p.astype(BF16))
        l[i][c] = alpha * l[i][c] + jnp.sum(p, axis=0, keepdims=True)
        m[i][c] = m_new
    lam = _diff_lambda(lq1_ref[...], lk1_ref[...], lq2_ref[...], lk2_ref[...], lam_init)
    for i in range(nq):
        ot = acc_ref[i, 0] * (1.0 / l[i][0]) - lam * (acc_ref[i, 1] * (1.0 / l[i][1]))
        ms = jnp.mean(ot * ot, axis=0, keepdims=True)
        ot = ot * lax.rsqrt(ms + RMS_EPS) * g_ref[...] * (1.0 - lam_init)
        o_ref[i * tq:(i + 1) * tq, :] = ot.T


def _attn_prompt(q, k, v, lam_p, g_col, *, heads, lam_init):
    nb, seq, aw = q.shape
    hd = aw // heads
    qk = hd // 2
    tq = min(ATT_ROWS, seq)
    nq = seq // tq
    kern = functools.partial(_attn_prompt_kernel, tq=tq, qk=qk, heads=heads, lam_init=lam_init, scale=qk ** -0.5)
    small = pl.BlockSpec((1, qk), lambda b, h: (0, 0))
    return pl.pallas_call(
        kern,
        grid=(nb, heads),
        in_specs=[
            pl.BlockSpec((None, seq, hd), lambda b, h: (b, 0, h)),
            pl.BlockSpec((None, seq * heads, hd), lambda b, h: (b, 0, 0)),
            pl.BlockSpec((None, seq * heads, hd), lambda b, h: (b, 0, 0)),
            small, small, small, small,
            pl.BlockSpec((hd, 1), lambda b, h: (0, 0)),
        ],
        out_specs=pl.BlockSpec((seq, hd), lambda b, h: (b, h)),
        out_shape=jax.ShapeDtypeStruct((nb * seq, aw), F32),
        scratch_shapes=[pltpu.VMEM((seq, hd), BF16), pltpu.VMEM((nq, hd, tq), BF16),
                        pltpu.VMEM((2, seq, hd), BF16), pltpu.VMEM((nq, 2, hd, tq), F32)],
        compiler_params=_params("arbitrary", "arbitrary"),
        name="diff_attn_prompt",
    )(q, k, v, *lam_p, g_col)


def _attn_sample_kernel(pt_ref, q_ref, ks_ref, vs_ref, lq1_ref, lk1_ref, lq2_ref, lk2_ref, g_ref, ck_hbm, cv_hbm,
                        o_ref, kbuf, vbuf, sem, *, n_pages, page, heads, qk, lam_init, scale):
    b = pl.program_id(0)
    n_seq = pl.num_programs(0)
    slot = b % 2
    hd = 2 * qk
    aw = heads * hd
    n_maps = 2 * heads
    page_rows = page * heads
    seq_rows = n_pages * page_rows

    def page_copies(seq, buf, j):
        src = pl.ds(pl.multiple_of(pt_ref[seq * n_pages + j] * page_rows, page_rows), page_rows)
        dst = pl.ds(buf * seq_rows + j * page_rows, page_rows)
        return (pltpu.make_async_copy(ck_hbm.at[src, :], kbuf.at[dst, :], sem.at[buf]),
                pltpu.make_async_copy(cv_hbm.at[src, :], vbuf.at[dst, :], sem.at[buf]))

    def start_fetch(seq, buf):
        for j in range(n_pages):
            for cp in page_copies(seq, buf, j):
                cp.start()

    @pl.when(b == 0)
    def _():
        start_fetch(0, 0)

    @pl.when(b + 1 < n_seq)
    def _():
        start_fetch(b + 1, 1 - slot)

    q = q_ref[...] * scale
    row = lax.broadcasted_iota(I32, (n_maps, aw), 0)
    lane = lax.broadcasted_iota(I32, (n_maps, aw), 1)
    own = (lane // hd == row % heads) & ((lane // qk) % 2 == row // heads)
    qm = jnp.where(own, jnp.broadcast_to(q, (n_maps, aw)), 0.0)
    qmb = qm.astype(BF16)
    for j in range(n_pages):
        for cp in page_copies(b, slot, j):
            cp.wait()

    def group_rows(buf_ref, g):
        n_tok = min(ATT_PAGE_GROUP, n_pages - g) * page
        base = slot * seq_rows + g * page_rows
        return jnp.concatenate([buf_ref[pl.ds(base + h, n_tok, stride=heads), :] for h in range(heads)],
                               axis=1).astype(BF16)

    groups = range(0, n_pages, ATT_PAGE_GROUP)
    s_pages = [_dot_nt(qmb, group_rows(kbuf, g)) for g in groups]
    s_self = jnp.sum(qm * ks_ref[...], axis=1, keepdims=True)
    m = s_self
    for s in s_pages:
        m = jnp.maximum(m, jnp.max(s, axis=1, keepdims=True))
    p_self = jnp.exp(s_self - m)
    p_pages = [jnp.exp(s - m) for s in s_pages]
    l = p_self
    for p in p_pages:
        l = l + jnp.sum(p, axis=1, keepdims=True)
    inv = 1.0 / l
    lam = _diff_lambda(lq1_ref[...], lk1_ref[...], lq2_ref[...], lk2_ref[...], lam_init)

    def mix(p):
        pn = p * inv
        return pn - lam * pltpu.roll(pn, heads, 0)

    acc = jnp.broadcast_to(mix(p_self), (n_maps, aw)) * vs_ref[...]
    for p, g in zip(p_pages, groups):
        acc = acc + _dot(mix(p).astype(BF16), group_rows(vbuf, g))
    pick = (row < heads) & (lane // hd == row)
    o = jnp.sum(jnp.where(pick, acc, 0.0), axis=0, keepdims=True)
    g = g_ref[...]
    for h in range(heads):
        o_ref[:, h * hd:(h + 1) * hd] = _head_norm(o[:, h * hd:(h + 1) * hd], g, lam_init)


def _attn_sample(q, k_new, v_new, cache_k, cache_v, page_ids, lam_p, g, *, page, heads, lam_init):
    bs, aw = q.shape
    n_pages = page_ids.shape[1]
    hd = aw // heads
    qk = hd // 2
    kern = functools.partial(_attn_sample_kernel, n_pages=n_pages, page=page, heads=heads, qk=qk,
                             lam_init=lam_init, scale=qk ** -0.5)
    row_spec = pl.BlockSpec((None, 1, aw), lambda b, pt: (b, 0, 0))
    small = pl.BlockSpec((1, qk), lambda b, pt: (0, 0))
    buf_rows = 2 * n_pages * page * heads
    grid_spec = pltpu.PrefetchScalarGridSpec(
        num_scalar_prefetch=1,
        grid=(bs,),
        in_specs=[row_spec, row_spec, row_spec, small, small, small, small,
                  pl.BlockSpec((1, hd), lambda b, pt: (0, 0)),
                  pl.BlockSpec(memory_space=pl.ANY), pl.BlockSpec(memory_space=pl.ANY)],
        out_specs=row_spec,
        scratch_shapes=[pltpu.VMEM((buf_rows, hd), F32), pltpu.VMEM((buf_rows, hd), F32),
                        pltpu.SemaphoreType.DMA((2,))],
    )
    out = pl.pallas_call(
        kern,
        grid_spec=grid_spec,
        out_shape=jax.ShapeDtypeStruct((bs, 1, aw), F32),
        compiler_params=_params("arbitrary"),
        name="diff_attn_sample",
    )(page_ids.reshape(-1).astype(I32), q.reshape(bs, 1, aw), k_new.reshape(bs, 1, aw),
      v_new.reshape(bs, 1, aw), *lam_p, g, cache_k, cache_v)
    return out.reshape(bs, aw)


def _layer_norm(z, g, b):
    mu = jnp.mean(z, axis=-1, keepdims=True)
    zc = z - mu
    var = jnp.mean(zc * zc, axis=-1, keepdims=True)
    return zc * lax.rsqrt(var + LN_EPS) * g + b


def _out_kernel(xp_ref, sp_ref, ap_ref, xs_ref, ss_ref, as_ref, wo_ref, g_ref, b_ref, wrh_ref, wrl_ref,
                br_ref, h_ref, e_ref, gt_ref, *, n_prompt, sw, alpha, top_k):
    i = pl.program_id(0)
    is_s = i == n_prompt
    rows = h_ref.shape[0]
    halves = [slice(r, r + rows // 2) for r in (0, rows // 2)]
    mixes = []
    for hs in halves:
        s = jnp.where(is_s, ss_ref[hs, :], sp_ref[hs, :]).astype(BF16)
        a = jnp.where(is_s, as_ref[hs, :], ap_ref[hs, :]).astype(BF16)
        mixes.append(_dot(s, wo_ref[:sw, :]) + _dot(a, wo_ref[sw:, :]))
    for hs, mix in zip(halves, mixes):
        x = jnp.where(is_s, xs_ref[hs, :], xp_ref[hs, :])
        h = _layer_norm(alpha * x + mix, g_ref[...], b_ref[...])
        h_ref[hs, :] = h
        h_hi = h.astype(BF16)
        h_lo = (h - h_hi.astype(F32)).astype(BF16)
        lg = (_dot_nt(wrh_ref[...], h_hi) + _dot_nt(wrl_ref[...], h_hi) + _dot_nt(wrh_ref[...], h_lo)
              + br_ref[...])
        eio = lax.broadcasted_iota(I32, lg.shape, 0)
        n_exp = lg.shape[0]
        vals, idxs = [], []
        for _ in range(top_k):
            m = jnp.max(lg, axis=0, keepdims=True)
            idx = jnp.min(jnp.where(lg == m, eio, n_exp), axis=0, keepdims=True)
            vals.append(m)
            idxs.append(idx)
            lg = jnp.where(eio == idx, NEG_BIG, lg)
        tv = jnp.concatenate(vals, axis=0)
        pe = jnp.exp(tv - tv[0:1, :])
        gt_ref[:, hs] = pe / jnp.sum(pe, axis=0, keepdims=True)
        e_ref[:, hs] = jnp.concatenate(idxs, axis=0)


def _out_route(xp, sp, ap, xs, ss, as_, w_out_bf, g, b, wr_hi, wr_lo, br, *, alpha):
    tp, d = xp.shape
    sw = sp.shape[1]
    rows = OUT_ROWS
    n_prompt = tp // rows
    total = tp + rows
    n_exp = wr_hi.shape[0]
    pidx = lambda i: (jnp.minimum(i, n_prompt - 1), 0)
    const = lambda i: (0, 0)
    kern = functools.partial(_out_kernel, n_prompt=n_prompt, sw=sw, alpha=alpha, top_k=TOP_K)
    return pl.pallas_call(
        kern,
        grid=(n_prompt + 1,),
        in_specs=[
            pl.BlockSpec((rows, d), pidx), pl.BlockSpec((rows, sw), pidx), pl.BlockSpec((rows, d - sw), pidx),
            pl.BlockSpec((rows, d), const), pl.BlockSpec((rows, sw), const), pl.BlockSpec((rows, d - sw), const),
            pl.BlockSpec(w_out_bf.shape, const), pl.BlockSpec((1, d), const), pl.BlockSpec((1, d), const),
            pl.BlockSpec((n_exp, d), const), pl.BlockSpec((n_exp, d), const), pl.BlockSpec((n_exp, 1), const),
        ],
        out_specs=[
            pl.BlockSpec((rows, d), lambda i: (i, 0)),
            pl.BlockSpec((TOP_K, rows), lambda i: (0, i)),
            pl.BlockSpec((TOP_K, rows), lambda i: (0, i)),
        ],
        out_shape=[
            jax.ShapeDtypeStruct((total, d), F32),
            jax.ShapeDtypeStruct((TOP_K, total), I32),
            jax.ShapeDtypeStruct((TOP_K, total), F32),
        ],
        compiler_params=_params("arbitrary"),
        name="out_proj_route",
    )(xp, sp, ap, xs, ss, as_, w_out_bf, g, b, wr_hi, wr_lo, br)


def _tile_plan(e_tile, tile_base, n_tok, n_exp):
    k_, tt = e_tile.shape
    valid = (tile_base + lax.broadcasted_iota(I32, (1, tt), 1)) < n_tok
    eio = lax.broadcasted_iota(I32, (n_exp, tt), 0)
    onehot = [jnp.where((eio == e_tile[k:k + 1, :]) & valid, 1.0, 0.0) for k in range(k_)]
    count_k = [jnp.sum(o, axis=1, keepdims=True) for o in onehot]
    count = functools.reduce(lambda a, b: a + b, count_k)
    padded = jnp.floor((count + (ROUTE_CHUNK - 1)) * (1.0 / ROUTE_CHUNK)) * ROUTE_CHUNK
    r = lax.broadcasted_iota(I32, (n_exp, n_exp), 0)
    c = lax.broadcasted_iota(I32, (n_exp, n_exp), 1)
    padded_row = jnp.sum(jnp.where(r == c, jnp.broadcast_to(padded, (n_exp, n_exp)), 0.0), axis=0, keepdims=True)
    offset = jnp.sum(jnp.where(c < r, jnp.broadcast_to(padded_row, (n_exp, n_exp)), 0.0), axis=1, keepdims=True)
    earlier = (lax.broadcasted_iota(I32, (tt, tt), 0) < lax.broadcasted_iota(I32, (tt, tt), 1)).astype(BF16)
    base = offset
    pos = []
    for k in range(k_):
        before = _dot(onehot[k].astype(BF16), earlier) + base
        pos.append(jnp.where(valid, jnp.sum(onehot[k] * before, axis=0, keepdims=True), -1.0))
        base = base + count_k[k]
    return pos


def _block_relative(pos, c):
    rel = pos - float(c * ROUTE_ROWS)
    return jnp.where((rel >= 0.0) & (rel < float(ROUTE_ROWS)), rel, -1.0).astype(BF16)


def _segment_copies(src_ref, dst_ref, n, make_copy):
    def body(c, carry):
        make_copy(pl.multiple_of(src_ref[0, c], ROUTE_CHUNK), pl.multiple_of(dst_ref[0, c], ROUTE_CHUNK)).start()
        return carry

    lax.fori_loop(0, n, body, 0)


def _wait_chunks(n, make_copy):
    def wait_many(c, carry):
        make_copy(ROUTE_WAIT * ROUTE_CHUNK).wait()
        return carry

    def wait_one(c, carry):
        make_copy(ROUTE_CHUNK).wait()
        return carry

    lax.fori_loop(0, n // ROUTE_WAIT, wait_many, 0)
    lax.fori_loop(0, n % ROUTE_WAIT, wait_one, 0)


def _dispatch_kernel(nchunk_ref, zlo_ref, zhi_ref, src_ref, dst_ref, h_ref, e_ref, x_hbm, pos_ref, xt_ref, zero_ref,
                     sem, *, n_tok, n_exp, block_rows):
    i = pl.program_id(0)
    tt = h_ref.shape[0]
    npos = xt_ref.shape[1]

    @pl.when(i == 0)
    def _():
        zero_ref[...] = jnp.zeros(zero_ref.shape, F32)

        def zero_block(b):
            rows0 = pl.multiple_of(b * block_rows, block_rows)
            return pltpu.make_async_copy(zero_ref, x_hbm.at[pl.ds(rows0, block_rows), :], sem)

        def start(b, c):
            zero_block(b).start()
            return c

        def wait(b, c):
            zero_block(b).wait()
            return c

        for e in range(n_exp + 1):
            lax.fori_loop(zlo_ref[e], zhi_ref[e], start, 0)
        for e in range(n_exp + 1):
            lax.fori_loop(zlo_ref[e], zhi_ref[e], wait, 0)

    slot = i % 2
    pos = _tile_plan(e_ref[...], i * tt, n_tok, n_exp)
    pos_ref[...] = jnp.concatenate(pos, axis=0)
    hb = h_ref[...].astype(BF16)
    rio = lax.broadcasted_iota(I32, (ROUTE_ROWS, tt), 0).astype(F32).astype(BF16)
    for c in range(npos // ROUTE_ROWS):
        rel = [_block_relative(p, c) for p in pos]
        sel = functools.reduce(lambda a, b: a | b, [rio == r for r in rel])
        one_hot = jnp.where(sel, jnp.ones((), BF16), jnp.zeros((), BF16))
        xt_ref[slot, c * ROUTE_ROWS:(c + 1) * ROUTE_ROWS, :] = _dot(one_hot, hb)

    def copy(src_row, dst_row, rows=ROUTE_CHUNK):
        return pltpu.make_async_copy(xt_ref.at[slot, pl.ds(src_row, rows), :], x_hbm.at[pl.ds(dst_row, rows), :], sem)

    @pl.when(i > 0)
    def _():
        _wait_chunks(nchunk_ref[i - 1], lambda rows: copy(0, 0, rows))

    _segment_copies(src_ref, dst_ref, nchunk_ref[i], copy)

    @pl.when(i == pl.num_programs(0) - 1)
    def _():
        _wait_chunks(nchunk_ref[i], lambda rows: copy(0, 0, rows))


def _dispatch(h_all, e_idx, tables, *, n_tok, n_exp, n_blocks):
    t_pad, d = h_all.shape
    tt = OUT_ROWS
    npos = TOP_K * tt + n_exp * ROUTE_CHUNK
    kern = functools.partial(_dispatch_kernel, n_tok=n_tok, n_exp=n_exp, block_rows=MOE_ROWS)
    chunk_list = pl.BlockSpec((None, 1, npos // ROUTE_CHUNK), lambda i, *_: (i, 0, 0), memory_space=pltpu.SMEM)
    grid_spec = pltpu.PrefetchScalarGridSpec(
        num_scalar_prefetch=3,
        grid=(t_pad // tt,),
        in_specs=[chunk_list, chunk_list,
                  pl.BlockSpec((tt, d), lambda i, *_: (i, 0)),
                  pl.BlockSpec((TOP_K, tt), lambda i, *_: (0, i))],
        out_specs=[pl.BlockSpec(memory_space=pl.ANY), pl.BlockSpec((TOP_K, tt), lambda i, *_: (0, i))],
        scratch_shapes=[pltpu.VMEM((2, npos, d), F32), pltpu.VMEM((MOE_ROWS, d), F32), pltpu.SemaphoreType.DMA],
    )
    return pl.pallas_call(
        kern,
        grid_spec=grid_spec,
        out_shape=[jax.ShapeDtypeStruct((n_blocks * MOE_ROWS, d), F32), jax.ShapeDtypeStruct((TOP_K, t_pad), F32)],
        compiler_params=_params("arbitrary"),
        name="moe_dispatch",
    )(tables["nchunk"], tables["zlo"], tables["zhi"], tables["src"], tables["dst"], h_all, e_idx)


def _moe_kernel(be_ref, nu_ref, slot_ref, next_ref, x_ref, wgu_hbm, bgu_ref, wdn_hbm, bdn_ref, y_ref,
                wgu_f, wdn_f, wgu_bf, wdn_bf, wsem, *, de):
    i = pl.program_id(0)

    def fetch(e, slot):
        return (pltpu.make_async_copy(wgu_hbm.at[e], wgu_f.at[slot], wsem.at[slot]),
                pltpu.make_async_copy(wdn_hbm.at[e], wdn_f.at[slot], wsem.at[slot]))

    @pl.when(i >= nu_ref[0])
    def _():
        y_ref[...] = jnp.zeros(y_ref.shape, F32)

    @pl.when(i < nu_ref[0])
    def _():
        slot = slot_ref[i]

        @pl.when(i == 0)
        def _():
            for cp in fetch(be_ref[0], 0):
                cp.start()

        @pl.when((i == 0) | (be_ref[i] != be_ref[jnp.maximum(i - 1, 0)]))
        def _():
            for cp in fetch(be_ref[i], slot):
                cp.wait()
            wgu_bf[...] = wgu_f[slot].astype(BF16)
            wdn_bf[...] = wdn_f[slot].astype(BF16)

            @pl.when(next_ref[i] >= 0)
            def _():
                for cp in fetch(next_ref[i], 1 - slot):
                    cp.start()

        x = x_ref[...].astype(BF16)
        cw = MOE_HIDDEN_CHUNK
        n_chunks = de // cw

        def up(c):
            lo, hi = c * cw, (c + 1) * cw
            return (_dot(x, wgu_bf[:, lo:hi]) + bgu_ref[:, lo:hi],
                    _dot(x, wgu_bf[:, de + lo:de + hi]) + bgu_ref[:, de + lo:de + hi])

        pending = {c: up(c) for c in range(min(MOE_LOOKAHEAD, n_chunks))}
        y = bdn_ref[...]
        for c in range(n_chunks):
            if c + MOE_LOOKAHEAD < n_chunks:
                pending[c + MOE_LOOKAHEAD] = up(c + MOE_LOOKAHEAD)
            glu, lin = pending.pop(c)
            glu = jnp.minimum(glu, SWIGLU_LIMIT)
            lin = jnp.clip(lin, -SWIGLU_LIMIT, SWIGLU_LIMIT)
            act = glu * jax.nn.sigmoid(SWIGLU_ALPHA * glu) * (lin + 1.0)
            y = y + _dot(act.astype(BF16), wdn_bf[c * cw:(c + 1) * cw, :])
        y_ref[...] = y


def _moe(x_rows, block_expert, n_used, w_gu, b_gu, w_dn, b_dn):
    rows = MOE_ROWS
    n_blocks = block_expert.shape[0]
    n_exp, d, de2 = w_gu.shape
    de = de2 // 2
    kern = functools.partial(_moe_kernel, de=de)
    idx = jnp.arange(n_blocks, dtype=I32)
    in_use = idx < n_used[0]
    changed = jnp.concatenate([jnp.ones((1,), bool), block_expert[1:] != block_expert[:-1]])
    w_slot = ((jnp.cumsum(changed.astype(I32)) - 1) % 2).astype(I32)
    later = (block_expert[None, :] > block_expert[:, None]) & in_use[None, :]
    next_e = jnp.min(jnp.where(later, block_expert[None, :], n_exp), axis=1)
    next_e = jnp.where(next_e < n_exp, next_e, -1).astype(I32)

    def used(i, nu):
        return jnp.minimum(i, nu[0] - 1)

    def expert(i, be, nu, *_):
        return (be[used(i, nu)], 0, 0)

    grid_spec = pltpu.PrefetchScalarGridSpec(
        num_scalar_prefetch=4,
        grid=(n_blocks,),
        in_specs=[
            pl.BlockSpec((rows, d), lambda i, be, nu, *_: (used(i, nu), 0)),
            pl.BlockSpec(memory_space=pl.ANY),
            pl.BlockSpec((None, 1, de2), expert),
            pl.BlockSpec(memory_space=pl.ANY),
            pl.BlockSpec((None, 1, d), expert),
        ],
        out_specs=pl.BlockSpec((rows, d), lambda i, *_: (i, 0)),
        scratch_shapes=[pltpu.VMEM((2, d, de2), F32), pltpu.VMEM((2, de, d), F32),
                        pltpu.VMEM((d, de2), BF16), pltpu.VMEM((de, d), BF16), pltpu.SemaphoreType.DMA((2,))],
    )
    return pl.pallas_call(
        kern,
        grid_spec=grid_spec,
        out_shape=jax.ShapeDtypeStruct(x_rows.shape, F32),
        compiler_params=_params("arbitrary"),
        name="moe_experts",
    )(block_expert, n_used, w_slot, next_e, x_rows, w_gu, b_gu.reshape(n_exp, 1, de2), w_dn,
      b_dn.reshape(n_exp, 1, d))


def _route_tables(e_idx, n_tok, n_exp):
    k_, t_pad = e_idx.shape
    tt = OUT_ROWS
    rows = MOE_ROWS
    tiles = t_pad // tt
    valid = jnp.arange(t_pad, dtype=I32) < n_tok
    e_m = jnp.where(valid[None, :], e_idx, -1).reshape(k_, tiles, tt, 1)
    cnt = jnp.sum((e_m == jnp.arange(n_exp, dtype=I32)).astype(I32), axis=(0, 2))
    cpad = ((cnt + ROUTE_CHUNK - 1) // ROUTE_CHUNK) * ROUTE_CHUNK
    off = jnp.cumsum(cpad, axis=1) - cpad
    total = jnp.sum(cpad, axis=0)
    region = ((total + rows - 1) // rows) * rows
    pend = jnp.cumsum(region)
    pstart = pend - region
    gbase = pstart[None, :] + jnp.cumsum(cpad, axis=0) - cpad
    n_blocks = -(-(k_ * n_tok + n_exp * (tiles * (ROUTE_CHUNK - 1) + rows - 1)) // rows)
    block_start = jnp.arange(n_blocks, dtype=I32) * rows
    block_expert = jnp.minimum(jnp.sum((pend[None, :] <= block_start[:, None]).astype(I32), axis=1), n_exp - 1)
    n_used = (pend[-1] // rows).astype(I32)
    zlo = jnp.concatenate([(pstart + total) // rows, n_used[None]]).astype(I32)
    zhi = jnp.concatenate([pend // rows, jnp.full((1,), n_blocks, I32)]).astype(I32)
    nch = cpad // ROUTE_CHUNK
    cum = jnp.cumsum(nch, axis=1)
    c_id = jnp.arange((k_ * tt + n_exp * ROUTE_CHUNK) // ROUTE_CHUNK, dtype=I32)
    owner = jnp.sum((cum[:, None, :] <= c_id[None, :, None]).astype(I32), axis=2)
    pick = (jnp.minimum(owner, n_exp - 1)[:, :, None] == jnp.arange(n_exp, dtype=I32)).astype(I32)
    within = (c_id[None, :] - jnp.sum(pick * (cum - nch)[:, None, :], axis=2)) * ROUTE_CHUNK
    src = jnp.sum(pick * off[:, None, :], axis=2) + within
    dst = jnp.sum(pick * gbase[:, None, :], axis=2) + within
    tables = dict(nchunk=cum[:, -1].astype(I32), src=src[:, None, :].astype(I32), dst=dst[:, None, :].astype(I32),
                  zlo=zlo, zhi=zhi)
    return tables, block_expert.astype(I32), n_used.reshape(1), n_blocks


def _final_kernel(nchunk_ref, srcc_ref, dstc_ref, srcn_ref, dstn_ref, pos_ref, gt_ref, h_ref, g_ref, b_ref, y_hbm,
                  op_ref, os_ref, yt_ref, sem, *, n_prompt, alpha):
    i = pl.program_id(0)
    n_tiles = pl.num_programs(0)
    tt = h_ref.shape[0]
    npos = yt_ref.shape[1]
    slot = i % 2

    def fetch(buf):
        def copy(local_row, global_row, rows=ROUTE_CHUNK):
            return pltpu.make_async_copy(y_hbm.at[pl.ds(global_row, rows), :],
                                         yt_ref.at[buf, pl.ds(local_row, rows), :], sem.at[buf])
        return copy

    @pl.when(i == 0)
    def _():
        yt_ref[...] = jnp.zeros(yt_ref.shape, F32)
        _segment_copies(srcc_ref, dstc_ref, nchunk_ref[0], fetch(0))

    @pl.when(i + 1 < n_tiles)
    def _():
        _segment_copies(srcn_ref, dstn_ref, nchunk_ref[jnp.minimum(i + 1, n_tiles - 1)], fetch(1 - slot))

    top_k = pos_ref.shape[0]
    rows = jnp.concatenate([pos_ref[...], gt_ref[...], jnp.zeros((LANES - 2 * top_k, tt), F32)], axis=0)
    cols = rows.T

    _wait_chunks(nchunk_ref[i], lambda rows: fetch(slot)(0, 0, rows))
    f = jnp.zeros((tt, h_ref.shape[1]), F32)
    lio = lax.broadcasted_iota(I32, (tt, ROUTE_ROWS), 1).astype(F32).astype(BF16)
    gate_cols = [cols[:, top_k + k:top_k + k + 1].astype(BF16) for k in range(top_k)]
    for c in range(npos // ROUTE_ROWS):
        w = jnp.zeros((tt, ROUTE_ROWS), BF16)
        for k in range(top_k):
            w = jnp.where(lio == _block_relative(cols[:, k:k + 1], c), gate_cols[k], w)
        f = f + _dot(w, yt_ref[slot, c * ROUTE_ROWS:(c + 1) * ROUTE_ROWS, :].astype(BF16))
    out = _layer_norm(alpha * h_ref[...] + f, g_ref[...], b_ref[...])

    @pl.when(i < n_prompt)
    def _():
        op_ref[...] = out

    @pl.when(i == n_prompt)
    def _():
        os_ref[...] = out[:os_ref.shape[0], :]


def _final(y_rows, pos, gates, h_all, g, b, tables, *, tp, ts, n_exp, alpha):
    tt = OUT_ROWS
    t_pad, d = h_all.shape
    n_prompt = tp // tt
    npos = TOP_K * tt + n_exp * ROUTE_CHUNK
    const = lambda i, *_: (0, 0)
    kern = functools.partial(_final_kernel, n_prompt=n_prompt, alpha=alpha)
    n_tiles = t_pad // tt
    n_chunks = npos // ROUTE_CHUNK
    list_cur = pl.BlockSpec((None, 1, n_chunks), lambda i, *_: (i, 0, 0), memory_space=pltpu.SMEM)
    list_nxt = pl.BlockSpec((None, 1, n_chunks), lambda i, *_: (jnp.minimum(i + 1, n_tiles - 1), 0, 0),
                            memory_space=pltpu.SMEM)
    grid_spec = pltpu.PrefetchScalarGridSpec(
        num_scalar_prefetch=1,
        grid=(n_tiles,),
        in_specs=[list_cur, list_cur, list_nxt, list_nxt,
                  pl.BlockSpec((TOP_K, tt), lambda i, *_: (0, i)),
                  pl.BlockSpec((TOP_K, tt), lambda i, *_: (0, i)),
                  pl.BlockSpec((tt, d), lambda i, *_: (i, 0)),
                  pl.BlockSpec((1, d), const), pl.BlockSpec((1, d), const),
                  pl.BlockSpec(memory_space=pl.ANY)],
        out_specs=[pl.BlockSpec((tt, d), lambda i, *_: (jnp.minimum(i, n_prompt - 1), 0)),
                   pl.BlockSpec((ts, d), const)],
        scratch_shapes=[pltpu.VMEM((2, npos, d), F32), pltpu.SemaphoreType.DMA((2,))],
    )
    return pl.pallas_call(
        kern,
        grid_spec=grid_spec,
        out_shape=[jax.ShapeDtypeStruct((tp, d), F32), jax.ShapeDtypeStruct((ts, d), F32)],
        compiler_params=_params("arbitrary"),
        name="combine_ln",
    )(tables["nchunk"], tables["src"], tables["dst"], tables["src"], tables["dst"], pos, gates, h_all, g, b,
      y_rows)


def _pad_rows(a, rows):
    return jnp.pad(a, ((0, rows - a.shape[0]), (0, 0)))


def kernel(x_prompt, x_sample, cache_k, cache_v, state_ssm_re, state_ssm_im, page_table, w_in, w_out, ssm_a_re, ssm_a_im, ssm_log_dt, ssm_b_re, ssm_b_im, ssm_c_re, ssm_c_im, ssm_d, w_glu, b_glu, lambda_q1, lambda_k1, lambda_q2, lambda_k2, subln_g, ln1_g, ln1_b, w_router, b_router, w_gate_up, b_gate_up, w_down, b_down, ln2_g, ln2_b):
    depth = w_in.shape[0]
    bp, lp, d = x_prompt.shape
    bs, ls, _ = x_sample.shape
    n_phys, page, heads, hd = cache_k.shape[1:]
    assert ls == 1 and hd == LANES
    qk = hd // 2
    aw = heads * hd
    groups, n_state = ssm_a_re.shape[1:]
    sw = groups * ssm_b_re.shape[-1]
    gp = groups * n_state
    n_exp = w_router.shape[2]
    past_len = page_table.shape[1] * page
    alpha = (2 * depth) ** 0.25
    tp = bp * lp
    n_tok = tp + bs

    cos_p, sin_p = _rope_tables(jnp.arange(lp, dtype=I32), qk // 2)
    cos_s, sin_s = _rope_tables(jnp.full((bs,), past_len, I32), qk // 2)
    ck = cache_k.reshape(depth * n_phys * page * heads, hd)
    cv = cache_v.reshape(depth * n_phys * page * heads, hd)

    hp = x_prompt
    hs = x_sample.reshape(1, bs, d)
    outs = [[] for _ in range(8)]
    for l in range(depth):
        lam_init = 0.8 - 0.6 * math.exp(-0.3 * l)
        w_in_bf = w_in[l].astype(BF16)
        w_out_bf = w_out[l].astype(BF16)
        w_glu_bf = w_glu[l].astype(BF16)
        bg = b_glu[l].reshape(1, 2 * sw)
        lam_p = [p[l].reshape(1, qk).astype(F32) for p in (lambda_q1, lambda_k1, lambda_q2, lambda_k2)]
        prm = _ssm_params(ssm_a_re[l], ssm_a_im[l], ssm_log_dt[l], ssm_b_re[l], ssm_b_im[l],
                          ssm_c_re[l], ssm_c_im[l], ssm_d[l])

        tt = min(PROJ_ROWS // bp, lp)
        u_p, q_p, k_p, v_p = _project(hp, w_in_bf, cos_p, sin_p, tt=tt, sw=sw, aw=aw, half=qk // 2)
        zero = jnp.zeros((bp, gp), F32)
        ssm_p, sre_p, sim_p = _ssm(u_p, zero, zero, prm, nb=bp, steps=min(SSM_STEPS, lp), w_glu_bf=w_glu_bf,
                                   b_glu=bg)
        att_p = _attn_prompt(q_p, k_p, v_p, lam_p, subln_g[l].reshape(hd, 1), heads=heads, lam_init=lam_init)

        u_s, q_s, k_s, v_s = _project(hs, w_in_bf, cos_s, sin_s, tt=bs, sw=sw, aw=aw, half=qk // 2)
        ssm_s, sre_s, sim_s = _ssm(u_s, state_ssm_re[l].reshape(bs, gp), state_ssm_im[l].reshape(bs, gp), prm,
                                   nb=bs, steps=1, w_glu_bf=w_glu_bf, b_glu=bg)
        att_s = _attn_sample(q_s.reshape(bs, aw), k_s.reshape(bs, aw), v_s.reshape(bs, aw), ck, cv,
                             page_table + l * n_phys, lam_p, subln_g[l].reshape(1, hd), page=page, heads=heads,
                             lam_init=lam_init)

        wr = w_router[l].T.astype(F32)
        wr_hi = wr.astype(BF16)
        wr_lo = (wr - wr_hi.astype(F32)).astype(BF16)
        h_all, e_idx, gates = _out_route(
            hp.reshape(tp, d), ssm_p.reshape(tp, sw), att_p, _pad_rows(hs.reshape(bs, d), OUT_ROWS),
            _pad_rows(ssm_s, OUT_ROWS), _pad_rows(att_s, OUT_ROWS), w_out_bf, ln1_g[l].reshape(1, d),
            ln1_b[l].reshape(1, d), wr_hi, wr_lo, b_router[l].reshape(n_exp, 1), alpha=alpha)
        tables, block_expert, n_used, n_blocks = _route_tables(e_idx, n_tok, n_exp)
        x_rows, pos = _dispatch(h_all, e_idx, tables, n_tok=n_tok, n_exp=n_exp, n_blocks=n_blocks)
        y_rows = _moe(x_rows, block_expert, n_used, w_gate_up[l], b_gate_up[l], w_down[l], b_down[l])
        y_p, y_s = _final(y_rows, pos, gates, h_all, ln2_g[l].reshape(1, d), ln2_b[l].reshape(1, d), tables,
                          tp=tp, ts=bs, n_exp=n_exp, alpha=alpha)
        hp = y_p.reshape(bp, lp, d)
        hs = y_s.reshape(1, bs, d)

        for lst, val in zip(outs, (
                k_p.reshape(bp, lp, heads, hd), v_p.reshape(bp, lp, heads, hd),
                sre_p.reshape(bp, groups, n_state), sim_p.reshape(bp, groups, n_state),
                k_s.reshape(bs, ls, heads, hd), v_s.reshape(bs, ls, heads, hd),
                sre_s.reshape(bs, groups, n_state), sim_s.reshape(bs, groups, n_state))):
            lst.append(val)

    return (hp, hs.reshape(bs, ls, d)) + tuple(jnp.stack(o) for o in outs)
```
